```python
import jax, jax.numpy as jnp
from jax import lax
import numpy as np

D_MODEL = 2048
BATCH = 4
SEQ = 4096
DEPTH = 1

PLE_DIM = 256
MIX_WIDTH = D_MODEL
RMS_EPS = 1e-6
RW_WIDTH = MIX_WIDTH // 2
RW_HEAD = 64
RW_HEADS = RW_WIDTH // RW_HEAD
RW_DECAY_LORA = 64
RW_AAA_LORA = 64
RW_GATE_LORA = 160
RW_GN_EPS = 64e-5
RW_COLS = 3 * RW_WIDTH + RW_DECAY_LORA + RW_AAA_LORA + RW_GATE_LORA
NSA_WIDTH = MIX_WIDTH - RW_WIDTH
NSA_HEAD = 64
NSA_HEADS = NSA_WIDTH // NSA_HEAD
NSA_KV_GROUPS = 4
NSA_REP = NSA_HEADS // NSA_KV_GROUPS
NSA_KV = NSA_KV_GROUPS * NSA_HEAD
CMP_LEN = 32
CMP_STRIDE = 16
CMP_HIDDEN = 256
SLC_BLOCK = 64
N_SELECT = 16
WINDOW = 512
Q_BLOCK = 64
NSA_COLS = NSA_WIDTH + 6 * NSA_KV + 3 * NSA_HEADS
IN_WIDTH = RW_COLS + NSA_COLS
N_EXPERTS = 32
TOP_K = 4
D_EXPERT = D_MODEL
SWIGLU_LIMIT = 7.0
SWIGLU_ALPHA = 1.702
EXPERT_ROWS = 512

kernel_name = "hymba_rwkv7_nsa_moe_ple_block"


def rmsnorm(x, g, eps=RMS_EPS):
    xf = x.astype(jnp.float32)
    y = xf * lax.rsqrt(jnp.mean(xf * xf, axis=-1, keepdims=True) + eps)
    return (y * g.astype(jnp.float32)).astype(x.dtype)


def split_cols(u, widths):
    return jnp.split(u, [int(o) for o in np.cumsum(widths)[:-1]], axis=-1)


def masked_softmax(s, mask):
    s = jnp.where(mask, s.astype(jnp.float32), -jnp.inf)
    m = jnp.max(s, axis=-1, keepdims=True)
    e = jnp.exp(s - jnp.where(jnp.isfinite(m), m, 0.0))
    return e / jnp.maximum(jnp.sum(e, axis=-1, keepdims=True), 1e-30)


def rwkv7_mixer(u, mu, w0, w2, a0, a2, g2, k_k, k_a, r_k, lnx_w, lnx_b):
    B, T, _ = u.shape
    f32 = jnp.float32
    shifted = jnp.pad(u, ((0, 0), (1, 0), (0, 0)))[:, :-1]
    u = u + (shifted - u) * mu
    r, k, v, xw, xa, xg = split_cols(u, [RW_WIDTH, RW_WIDTH, RW_WIDTH, RW_DECAY_LORA, RW_AAA_LORA, RW_GATE_LORA])
    w = -jax.nn.softplus(-(w0 + jnp.tanh(xw) @ w2)) - 0.5
    a = jax.nn.sigmoid(a0 + xa @ a2)
    g = jax.nn.sigmoid(xg) @ g2
    hd = lambda t: t.reshape(B, T, RW_HEADS, RW_HEAD)
    kk = hd(k * k_k).astype(f32)
    kk = kk / jnp.maximum(jnp.sqrt(jnp.sum(kk * kk, axis=-1, keepdims=True)), 1e-12)
    k = k * (1.0 + (a - 1.0) * k_a)
    r_h, k_h, v_h, a_h = hd(r).astype(f32), hd(k).astype(f32), hd(v).astype(f32), hd(a).astype(f32)
    decay = jnp.exp(-jnp.exp(hd(w).astype(f32)))
    a_vec, b_vec = -kk, kk * a_h

    def step(S, inp):
        r_t, w_t, k_t, v_t, a_t, b_t = inp
        sa = jnp.einsum('bhij,bhj->bhi', S, a_t)
        S = S * w_t[:, :, None, :] + sa[..., :, None] * b_t[..., None, :] + v_t[..., :, None] * k_t[..., None, :]
        return S, jnp.einsum('bhij,bhj->bhi', S, r_t)

    seqs = tuple(jnp.moveaxis(t, 1, 0) for t in (r_h, decay, k_h, v_h, a_vec, b_vec))
    S0 = jnp.zeros((B, RW_HEADS, RW_HEAD, RW_HEAD), f32)
    _, y = lax.scan(step, S0, seqs)
    y = jnp.moveaxis(y, 0, 1)
    mean = jnp.mean(y, axis=-1, keepdims=True)
    var = jnp.mean(jnp.square(y - mean), axis=-1, keepdims=True)
    y = ((y - mean) * lax.rsqrt(var + RW_GN_EPS)).reshape(B, T, RW_WIDTH) * lnx_w + lnx_b
    bonus = jnp.sum(r_h * k_h * r_k, axis=-1, keepdims=True) * v_h
    y = (y + bonus.reshape(B, T, RW_WIDTH)) * g
    return y.astype(u.dtype)


def nsa_mixer(u, q_norm, k_norm, pos_k, pos_v, ck_w1, ck_b1, ck_w2, ck_b2, cv_w1, cv_b1, cv_w2, cv_b2):
    B, T, _ = u.shape
    G, R, dk = NSA_KV_GROUPS, NSA_REP, NSA_HEAD
    q, kc, vc, ks, vs, kw, vw, gates = split_cols(u, [NSA_WIDTH] + [NSA_KV] * 6 + [3 * NSA_HEADS])
    q = rmsnorm(q.reshape(B, T, G, R, dk), q_norm)
    kc, vc, ks, vs, kw, vw = (t.reshape(B, T, G, dk) for t in (kc, vc, ks, vs, kw, vw))
    ks = rmsnorm(ks, k_norm[1])
    kw = rmsnorm(kw, k_norm[2])
    gates = jax.nn.sigmoid(gates.reshape(B, T, G, R, 3))
    scale = dk ** -0.5

    n_cmp = (T - CMP_LEN) // CMP_STRIDE + 1
    cmp_start = jnp.arange(n_cmp) * CMP_STRIDE
    idx = cmp_start[:, None] + jnp.arange(CMP_LEN)[None, :]

    def compress(t, pos, w1, b1, w2, b2):
        blk = t[:, idx] + pos[None, None, :, None, :]
        blk = blk.transpose(0, 1, 3, 2, 4).reshape(B, n_cmp, G, CMP_LEN * dk)
        return jax.nn.gelu(blk @ w1 + b1) @ w2 + b2

    Kc = rmsnorm(compress(kc, pos_k, ck_w1, ck_b1, ck_w2, ck_b2), k_norm[0])
    Vc = compress(vc, pos_v, cv_w1, cv_b1, cv_w2, cv_b2)
    cmp_end = cmp_start + CMP_LEN - 1

    n_slc = T // SLC_BLOCK
    n_sel = min(N_SELECT, n_slc)
    slc_start = jnp.arange(n_slc) * SLC_BLOCK
    overlap = ((cmp_start[:, None] < slc_start[None, :] + SLC_BLOCK)
               & (cmp_start[:, None] + CMP_LEN > slc_start[None, :])).astype(jnp.float32)
    Ks = ks.reshape(B, n_slc, SLC_BLOCK, G, dk).transpose(0, 3, 1, 2, 4)
    Vs = vs.reshape(B, n_slc, SLC_BLOCK, G, dk).transpose(0, 3, 1, 2, 4)
    b_ix = jnp.arange(B)[:, None, None]
    g_ix = jnp.arange(G)[None, :, None]
    Kw_pad = jnp.pad(kw, ((0, 0), (WINDOW, 0), (0, 0), (0, 0)))
    Vw_pad = jnp.pad(vw, ((0, 0), (WINDOW, 0), (0, 0), (0, 0)))

    def block(s):
        t_pos = s + jnp.arange(Q_BLOCK)
        qb = lax.dynamic_slice_in_dim(q, s, Q_BLOCK, axis=1)
        gb = lax.dynamic_slice_in_dim(gates, s, Q_BLOCK, axis=1)
        sc = jnp.einsum('bqgrd,bngd->bgrqn', qb, Kc) * scale
        p_c = masked_softmax(sc, cmp_end[None, :] <= t_pos[:, None])
        o_c = jnp.einsum('bgrqn,bngd->bqgrd', p_c.astype(Vc.dtype), Vc)
        imp = jnp.einsum('bgrqn,nj->bgqj', p_c, overlap)
        jj = jnp.arange(n_slc)[None, :]
        qblk = (t_pos // SLC_BLOCK)[:, None]
        forced = (jj == 0) | (jj == qblk) | (jj == qblk - 1)
        imp = jnp.where(forced, jnp.inf, imp)
        imp = jnp.where(slc_start[None, :] <= t_pos[:, None], imp, -jnp.inf)
        top_v, top_i = lax.top_k(imp, n_sel)
        blk_ok = top_v > -jnp.inf
        flat_i = top_i.reshape(B, G, Q_BLOCK * n_sel)
        Kg = Ks[b_ix, g_ix, flat_i].reshape(B, G, Q_BLOCK, n_sel * SLC_BLOCK, dk)
        Vg = Vs[b_ix, g_ix, flat_i].reshape(B, G, Q_BLOCK, n_sel * SLC_BLOCK, dk)
        kpos = top_i[..., None] * SLC_BLOCK + jnp.arange(SLC_BLOCK)
        ok = (blk_ok[..., None] & (kpos <= t_pos[None, None, :, None, None])).reshape(B, G, Q_BLOCK, -1)
        sc = jnp.einsum('bqgrd,bgqkd->bgrqk', qb, Kg) * scale
        p_s = masked_softmax(sc, ok[:, :, None])
        o_s = jnp.einsum('bgrqk,bgqkd->bqgrd', p_s.astype(Vg.dtype), Vg)
        Kwb = lax.dynamic_slice_in_dim(Kw_pad, s, WINDOW + Q_BLOCK, axis=1)
        Vwb = lax.dynamic_slice_in_dim(Vw_pad, s, WINDOW + Q_BLOCK, axis=1)
        wpos = s - WINDOW + jnp.arange(WINDOW + Q_BLOCK)
        wmask = ((wpos[None, :] <= t_pos[:, None]) & (wpos[None, :] > t_pos[:, None] - WINDOW)
                 & (wpos[None, :] >= 0))
        sc = jnp.einsum('bqgrd,bkgd->bgrqk', qb, Kwb) * scale
        p_w = masked_softmax(sc, wmask)
        o_w = jnp.einsum('bgrqk,bkgd->bqgrd', p_w.astype(Vwb.dtype), Vwb)
        return gb[..., 0:1] * o_c + gb[..., 1:2] * o_s + gb[..., 2:3] * o_w

    out = lax.map(block, jnp.arange(T // Q_BLOCK) * Q_BLOCK)
    return out.transpose(1, 0, 2, 3, 4, 5).reshape(B, T, NSA_WIDTH).astype(u.dtype)


def moe(h, router_w, router_b, w1, b1, w2, b2):
    B, T, D = h.shape
    xf = h.reshape(-1, D)
    N = xf.shape[0]
    logits = (xf @ router_w + router_b).astype(jnp.float32)
    top_v, top_e = lax.top_k(logits, TOP_K)
    gate = jax.nn.softmax(top_v, axis=-1)
    e_flat = top_e.reshape(-1)
    tok = jnp.repeat(jnp.arange(N), TOP_K)
    order = jnp.argsort(e_flat)
    e_sorted, tok_sorted, gw_sorted = e_flat[order], tok[order], gate.reshape(-1)[order]
    counts = jnp.bincount(e_flat, length=N_EXPERTS)
    start = jnp.cumsum(counts) - counts
    padded = (counts + EXPERT_ROWS - 1) // EXPERT_ROWS * EXPERT_ROWS
    pend = jnp.cumsum(padded)
    pstart = pend - padded
    dest = pstart[e_sorted] + (jnp.arange(N * TOP_K) - start[e_sorted])
    n_rows = (N * TOP_K + EXPERT_ROWS - 1) // EXPERT_ROWS * EXPERT_ROWS + N_EXPERTS * EXPERT_ROWS
    n_blk = n_rows // EXPERT_ROWS
    x_rows = jnp.zeros((n_rows, D), h.dtype).at[dest].set(xf[tok_sorted])
    blk_e = jnp.clip(jnp.searchsorted(pend, jnp.arange(n_blk) * EXPERT_ROWS, side='right'), 0, N_EXPERTS - 1)

    def expert_block(args):
        xb, e = args
        hh = xb @ w1[e] + b1[e]
        gt = jnp.minimum(hh[:, :D_EXPERT], SWIGLU_LIMIT)
        lin = jnp.clip(hh[:, D_EXPERT:], -SWIGLU_LIMIT, SWIGLU_LIMIT)
        act = gt * jax.nn.sigmoid(SWIGLU_ALPHA * gt) * (lin + 1.0)
        return act @ w2[e] + b2[e]

    y_rows = lax.map(expert_block, (x_rows.reshape(n_blk, EXPERT_ROWS, D), blk_e)).reshape(n_rows, D)
    y = jnp.zeros((N, D), jnp.float32).at[tok_sorted].add(y_rows[dest].astype(jnp.float32) * gw_sorted[:, None])
    return y.astype(h.dtype).reshape(B, T, D)


def setup_inputs(seed: int = 0) -> dict:
    key = jax.random.key(seed)
    ks = iter(jax.random.split(key, 40))
    nrm = lambda shape, s: jax.random.normal(next(ks), shape, jnp.float32) * s
    gain = lambda shape: 1.0 + nrm(shape, 0.02)
    uni = lambda shape, lo, hi: jax.random.uniform(next(ks), shape, jnp.float32, lo, hi)
    L, D = DEPTH, D_MODEL
    return {
        "x": nrm((BATCH, SEQ, D), 1.0),
        "p": nrm((DEPTH, BATCH, SEQ, PLE_DIM), 1.0),
        "mix_norm_g": gain((L, D)),
        "w_in": nrm((L, D, IN_WIDTH), D ** -0.5),
        "rw_mu": uni((L, RW_COLS), 0.0, 1.0),
        "rw_w0": uni((L, RW_WIDTH), -4.0, -0.5),
        "rw_w2": nrm((L, RW_DECAY_LORA, RW_WIDTH), 0.1),
        "rw_a0": nrm((L, RW_WIDTH), 0.1),
        "rw_a2": nrm((L, RW_AAA_LORA, RW_WIDTH), 0.5 * RW_AAA_LORA ** -0.5),
        "rw_g2": nrm((L, RW_GATE_LORA, RW_WIDTH), RW_GATE_LORA ** -0.5),
        "rw_k_k": 0.85 + nrm((L, RW_WIDTH), 0.02),
        "rw_k_a": gain((L, RW_WIDTH)),
        "rw_r_k": nrm((L, RW_HEADS, RW_HEAD), 0.1),
        "rw_lnx_w": gain((L, RW_WIDTH)),
        "rw_lnx_b": nrm((L, RW_WIDTH), 0.02),
        "nsa_q_norm": gain((L, NSA_HEAD)),
        "nsa_k_norm": gain((L, 3, NSA_HEAD)),
        "cmp_pos_k": nrm((L, CMP_LEN, NSA_HEAD), 0.02),
        "cmp_pos_v": nrm((L, CMP_LEN, NSA_HEAD), 0.02),
        "cmp_k_w1": nrm((L, CMP_LEN * NSA_HEAD, CMP_HIDDEN), (CMP_LEN * NSA_HEAD) ** -0.5),
        "cmp_k_b1": nrm((L, CMP_HIDDEN), 0.02),
        "cmp_k_w2": nrm((L, CMP_HIDDEN, NSA_HEAD), CMP_HIDDEN ** -0.5),
        "cmp_k_b2": nrm((L, NSA_HEAD), 0.02),
        "cmp_v_w1": nrm((L, CMP_LEN * NSA_HEAD, CMP_HIDDEN), (CMP_LEN * NSA_HEAD) ** -0.5),
        "cmp_v_b1": nrm((L, CMP_HIDDEN), 0.02),
        "cmp_v_w2": nrm((L, CMP_HIDDEN, NSA_HEAD), CMP_HIDDEN ** -0.5),
        "cmp_v_b2": nrm((L, NSA_HEAD), 0.02),
        "w_out": nrm((L, MIX_WIDTH, D), MIX_WIDTH ** -0.5),
        "moe_norm_g": gain((L, D)),
        "router_w": nrm((L, D, N_EXPERTS), D ** -0.5),
        "router_b": nrm((L, N_EXPERTS), 0.01),
        "moe_w1": nrm((L, N_EXPERTS, D, 2 * D_EXPERT), D ** -0.5),
        "moe_b1": nrm((L, N_EXPERTS, 2 * D_EXPERT), 0.02),
        "moe_w2": nrm((L, N_EXPERTS, D_EXPERT, D), D_EXPERT ** -0.5),
        "moe_b2": nrm((L, N_EXPERTS, D), 0.02),
        "ple_norm_g": gain((L, D)),
        "ple_w": nrm((L, PLE_DIM, D), PLE_DIM ** -0.5),
        "ple_gate_w": nrm((L, D, D), D ** -0.5),
    }


def reference(x, p, mix_norm_g, w_in, rw_mu, rw_w0, rw_w2, rw_a0, rw_a2, rw_g2, rw_k_k, rw_k_a, rw_r_k,
              rw_lnx_w, rw_lnx_b, nsa_q_norm, nsa_k_norm, cmp_pos_k, cmp_pos_v, cmp_k_w1, cmp_k_b1, cmp_k_w2,
              cmp_k_b2, cmp_v_w1, cmp_v_b1, cmp_v_w2, cmp_v_b2, w_out, moe_norm_g, router_w, router_b,
              moe_w1, moe_b1, moe_w2, moe_b2, ple_norm_g, ple_w, ple_gate_w):
    h = x
    for i in range(DEPTH):
        u = rmsnorm(h, mix_norm_g[i]) @ w_in[i]
        y_rw = rwkv7_mixer(u[..., :RW_COLS], rw_mu[i], rw_w0[i], rw_w2[i], rw_a0[i], rw_a2[i], rw_g2[i],
                           rw_k_k[i], rw_k_a[i], rw_r_k[i], rw_lnx_w[i], rw_lnx_b[i])
        y_nsa = nsa_mixer(u[..., RW_COLS:], nsa_q_norm[i], nsa_k_norm[i], cmp_pos_k[i], cmp_pos_v[i],
                          cmp_k_w1[i], cmp_k_b1[i], cmp_k_w2[i], cmp_k_b2[i],
                          cmp_v_w1[i], cmp_v_b1[i], cmp_v_w2[i], cmp_v_b2[i])
        h = h + jnp.concatenate([y_rw, y_nsa], axis=-1) @ w_out[i]
        h = h + moe(rmsnorm(h, moe_norm_g[i]), router_w[i], router_b[i], moe_w1[i], moe_b1[i], moe_w2[i], moe_b2[i])
        gate = jax.nn.sigmoid(rmsnorm(h, ple_norm_g[i]) @ ple_gate_w[i])
        h = h + gate * (p[i] @ ple_w[i])
    return h
```

```python
import functools

import jax
import jax.numpy as jnp
from jax import lax
from jax.experimental import pallas as pl
from jax.experimental.pallas import tpu as pltpu

F32 = jnp.float32
BF16 = jnp.bfloat16
I32 = jnp.int32
HI = lax.Precision.HIGHEST

LANES = 128
VMEM_LIMIT = 56 * 1024 * 1024

D_MODEL = 2048
RMS_EPS = 1e-6
RW_WIDTH = 1024
RW_HEAD = 64
RW_LORA = (64, 64, 160)
RW_GN_EPS = 64e-5
NSA_WIDTH = 1024
NSA_HEAD = 64
NSA_G = 4
NSA_R = 4
NSA_KV = NSA_G * NSA_HEAD
CMP_LEN = 32
CMP_STRIDE = 16
CMP_HIDDEN = 256
SLC_BLOCK = 64
N_SELECT = 16
WINDOW = 512
N_EXPERTS = 32
TOP_K = 4
D_EXPERT = 2048
SWIGLU_LIMIT = 7.0
SWIGLU_ALPHA = 1.702
PLE_DIM = 256

COL_RKV = 0
COL_Q = 3072
COL_KV = 4096
COL_MISC = 5632
MISC_W = 512
MISC_GATE = 384
IN_COLS = 6144

RW_HG = 4
RW_LW = RW_HG * RW_HEAD
RW_C = 64
NEG = -1e30


def _cp(sem, vmem=VMEM_LIMIT):
    return pltpu.CompilerParams(dimension_semantics=sem, vmem_limit_bytes=vmem)


def _dot(a, b, prec=None):
    return jnp.dot(a, b, preferred_element_type=F32, precision=prec)


def _dot_nt(a, b, prec=None):
    return lax.dot_general(a, b, (((1,), (1,)), ((), ())), preferred_element_type=F32, precision=prec)


def _iota(shape, dim):
    return lax.broadcasted_iota(I32, shape, dim)


def _seg_matrix(n, seg, scale):
    same = (_iota((n, n), 0) // seg) == (_iota((n, n), 1) // seg)
    return jnp.where(same, scale, 0.0).astype(F32)


def _inproj_kernel(x_ref, g_ref, w_ref, o_ref, xn_ref):
    @pl.when(pl.program_id(1) == 0)
    def _():
        x = x_ref[...]
        ms = jnp.mean(x * x, axis=-1, keepdims=True)
        xn_ref[...] = (x * lax.rsqrt(ms + RMS_EPS) * g_ref[...]).astype(BF16)

    o_ref[...] = _dot(xn_ref[...], w_ref[...])


def _inproj(x2, g, w_bf, tm=512, tn=512):
    n, d = x2.shape
    nc = w_bf.shape[1]
    return pl.pallas_call(
        _inproj_kernel,
        out_shape=jax.ShapeDtypeStruct((n, nc), F32),
        grid=(n // tm, nc // tn),
        in_specs=[
            pl.BlockSpec((tm, d), lambda i, j: (i, 0)),
            pl.BlockSpec((1, d), lambda i, j: (0, 0)),
            pl.BlockSpec((d, tn), lambda i, j: (0, j)),
        ],
        out_specs=pl.BlockSpec((tm, tn), lambda i, j: (i, j)),
        scratch_shapes=[pltpu.VMEM((tm, d), BF16)],
        compiler_params=_cp(("arbitrary", "arbitrary")),
        name="inproj",
    )(x2, g, w_bf)


def _softplus(z):
    return jnp.maximum(z, 0.0) + jnp.log(1.0 + jnp.exp(-jnp.abs(z)))


def _rwkv_kernel(r_ref, k_ref, v_ref, m_ref, mur_ref, muk_ref, muv_ref, mum_ref, w2_ref, a2_ref, g2_ref,
                 w0_ref, a0_ref, kk_ref, ka_ref, rk_ref, lw_ref, lb_ref, o_ref,
                 st_ref, pr_ref, pk_ref, pv_ref, pm_ref, *, nchunk):
    tt = nchunk * RW_C
    lw = RW_LW

    @pl.when(pl.program_id(2) == 0)
    def _():
        st_ref[...] = jnp.zeros_like(st_ref)
        pr_ref[...] = jnp.zeros_like(pr_ref)
        pk_ref[...] = jnp.zeros_like(pk_ref)
        pv_ref[...] = jnp.zeros_like(pv_ref)
        pm_ref[...] = jnp.zeros_like(pm_ref)

    def shift_lerp(x, prev_ref, mu):
        rolled = pltpu.roll(x, 1, 0)
        first = _iota(x.shape, 0) == 0
        sh = jnp.where(first, prev_ref[0:1, :], rolled)
        prev_ref[0:1, :] = x[tt - 1:tt, :]
        return x + (sh - x) * mu

    r = shift_lerp(r_ref[0], pr_ref, mur_ref[...])
    k = shift_lerp(k_ref[0], pk_ref, muk_ref[...])
    v = shift_lerp(v_ref[0], pv_ref, muv_ref[...])
    m = shift_lerp(m_ref[0], pm_ref, mum_ref[...])

    lw_dec = _dot(jnp.tanh(m).astype(BF16), w2_ref[...])
    lw_a = _dot(m.astype(BF16), a2_ref[...])
    g = _dot(jax.nn.sigmoid(m).astype(BF16), g2_ref[...])
    w_log = -_softplus(-(w0_ref[...] + lw_dec)) - 0.5
    ld = -jnp.exp(w_log)
    a = jax.nn.sigmoid(a0_ref[...] + lw_a)

    ones_seg = _seg_matrix(lw, RW_HEAD, 1.0)
    kk = k * kk_ref[...]
    kk = kk / jnp.maximum(jnp.sqrt(_dot(kk * kk, ones_seg, HI)), 1e-12)
    k2 = k * (1.0 + (a - 1.0) * ka_ref[...])
    av = -kk
    bv = kk * a

    ri = _iota((lw, lw), 0)
    ci = _iota((lw, lw), 1)
    same_blk = (ri // RW_C) == (ci // RW_C)
    m_strict = same_blk & (ci < ri)
    m_lower = same_blk & (ci <= ri)
    eye = (ri == ci).astype(F32)
    tri = (_iota((RW_C, RW_C), 1) <= _iota((RW_C, RW_C), 0)).astype(F32)
    lane_head = _iota((1, lw), 1) // RW_HEAD
    hmask = [(lane_head == h).astype(F32) for h in range(RW_HG)]

    def stack_heads(x):
        return jnp.concatenate([x * hmask[h] for h in range(RW_HG)], axis=0)

    def unstack_heads(xs):
        out = xs[0:RW_C] * hmask[0]
        for h in range(1, RW_HG):
            out = out + xs[h * RW_C:(h + 1) * RW_C] * hmask[h]
        return out

    ys = []
    st = st_ref[...]
    for c in range(nchunk):
        sl = slice(c * RW_C, (c + 1) * RW_C)
        ld_c = ld[sl]
        cs = _dot(tri, ld_c, HI)
        w_in = jnp.exp(cs)
        w_inv = jnp.exp(-cs)
        w_prev = jnp.exp(cs - ld_c)
        at = av[sl] * w_prev
        bt = bv[sl] * w_inv
        kt = k2[sl] * w_inv
        rt = r[sl] * w_in
        v_c = v[sl]
        at_s = stack_heads(at).astype(BF16)
        rt_s = stack_heads(rt).astype(BF16)
        bt4 = jnp.concatenate([bt] * RW_HG, axis=0).astype(BF16)
        kt4 = jnp.concatenate([kt] * RW_HG, axis=0).astype(BF16)
        v4 = jnp.concatenate([v_c] * RW_HG, axis=0).astype(BF16)
        a_ab = jnp.where(m_strict, _dot_nt(at_s, bt4), 0.0)
        a_ak = jnp.where(m_strict, _dot_nt(at_s, kt4), 0.0)
        a_rb = jnp.where(m_lower, _dot_nt(rt_s, bt4), 0.0)
        a_rk = jnp.where(m_lower, _dot_nt(rt_s, kt4), 0.0)
        pw = a_ab
        inv = eye + a_ab
        for _ in range(RW_C.bit_length() - 2):
            pw = _dot(pw, pw, HI)
            inv = inv + _dot(inv, pw, HI)
        st_b = st.astype(BF16)
        x0 = _dot_nt(at_s, st_b) + _dot(a_ak.astype(BF16), v4)
        u_s = _dot(inv, x0, HI)
        y_s = _dot_nt(rt_s, st_b) + _dot(a_rb.astype(BF16), u_s.astype(BF16)) + _dot(a_rk.astype(BF16), v4)
        ys.append(unstack_heads(y_s))
        u_c = unstack_heads(u_s)
        w_last = w_in[RW_C - 1:RW_C, :]
        lhs = jnp.concatenate([u_c.T, v_c.T], axis=1).astype(BF16)
        rhs = jnp.concatenate([bt * w_last, kt * w_last], axis=0).astype(BF16)
        st = st * w_last + jnp.where(same_blk, _dot(lhs, rhs), 0.0)
    st_ref[...] = st
    y = jnp.concatenate(ys, axis=0) if nchunk > 1 else ys[0]

    avg_seg = ones_seg * (1.0 / RW_HEAD)
    mean = _dot(y, avg_seg, HI)
    yc = y - mean
    var = _dot(yc * yc, avg_seg, HI)
    yn = yc * lax.rsqrt(var + RW_GN_EPS) * lw_ref[...] + lb_ref[...]
    bonus = _dot(r * k2 * rk_ref[...], ones_seg, HI) * v
    o_ref[0] = ((yn + bonus) * g).astype(o_ref.dtype)


def _rwkv(u3, mu_p, w2p, a2p, g2p, w0, a0, k_k, k_a, r_k, lnx_w, lnx_b, nchunk=2):
    b, t, _ = u3.shape
    tt = nchunk * RW_C
    lw = RW_LW
    nhg = RW_WIDTH // lw
    col = lambda base: (lambda bi, h, ti: (bi, ti, base + h))
    vec = lambda base: (lambda bi, h, ti: (0, base + h))
    par = pl.BlockSpec((1, lw), vec(0))
    return pl.pallas_call(
        functools.partial(_rwkv_kernel, nchunk=nchunk),
        out_shape=jax.ShapeDtypeStruct((b, t, RW_WIDTH), BF16),
        grid=(b, nhg, t // tt),
        in_specs=[
            pl.BlockSpec((1, tt, lw), col(0)),
            pl.BlockSpec((1, tt, lw), col(nhg)),
            pl.BlockSpec((1, tt, lw), col(2 * nhg)),
            pl.BlockSpec((1, tt, MISC_W), lambda bi, h, ti: (bi, ti, COL_MISC // MISC_W)),
            pl.BlockSpec((1, lw), vec(0)),
            pl.BlockSpec((1, lw), vec(nhg)),
            pl.BlockSpec((1, lw), vec(2 * nhg)),
            pl.BlockSpec((1, MISC_W), lambda bi, h, ti: (0, COL_MISC // MISC_W)),
            pl.BlockSpec((MISC_W, lw), lambda bi, h, ti: (0, h)),
            pl.BlockSpec((MISC_W, lw), lambda bi, h, ti: (0, h)),
            pl.BlockSpec((MISC_W, lw), lambda bi, h, ti: (0, h)),
            par, par, par, par, par, par, par,
        ],
        out_specs=pl.BlockSpec((1, tt, lw), lambda bi, h, ti: (bi, ti, h)),
        scratch_shapes=[
            pltpu.VMEM((lw, lw), F32),
            pltpu.VMEM((8, lw), F32),
            pltpu.VMEM((8, lw), F32),
            pltpu.VMEM((8, lw), F32),
            pltpu.VMEM((8, MISC_W), F32),
        ],
        compiler_params=_cp(("arbitrary", "arbitrary", "arbitrary")),
        name="rwkv7",
    )(u3, u3, u3, u3, mu_p, mu_p, mu_p, mu_p, w2p, a2p, g2p, w0, a0, k_k, k_a, r_k, lnx_w, lnx_b)


def _pad_rows(w, rows, offset):
    return jnp.zeros((rows, w.shape[1]), w.dtype).at[offset:offset + w.shape[0]].set(w)


def _rwkv_params(mu, w0, w2, a0, a2, g2, k_k, k_a, r_k, lnx_w, lnx_b):
    mu_p = jnp.zeros((1, IN_COLS), F32)
    mu_p = mu_p.at[0, COL_RKV:COL_RKV + 3 * RW_WIDTH].set(mu[:3 * RW_WIDTH])
    mu_p = mu_p.at[0, COL_MISC:COL_MISC + sum(RW_LORA)].set(mu[3 * RW_WIDTH:])
    o1, o2 = RW_LORA[0], RW_LORA[0] + RW_LORA[1]
    row = lambda p: p.reshape(1, RW_WIDTH).astype(F32)
    return (mu_p, _pad_rows(w2, MISC_W, 0).astype(BF16), _pad_rows(a2, MISC_W, o1).astype(BF16),
            _pad_rows(g2, MISC_W, o2).astype(BF16), row(w0), row(a0), row(k_k), row(k_a), row(r_k),
            row(lnx_w), row(lnx_b))


def _nsa_prep_kernel(q_ref, ks_ref, kw_ref, gt_ref, qg_ref, ksg_ref, kwg_ref, qo_ref, kso_ref, kwo_ref, go_ref):
    avg = _seg_matrix(NSA_KV, NSA_HEAD, 1.0 / NSA_HEAD)

    def head_norm(x, g):
        ms = _dot(x * x, avg, HI)
        return x * lax.rsqrt(ms + RMS_EPS) * g

    scale = NSA_HEAD ** -0.5
    for s in range(NSA_WIDTH // NSA_KV):
        sl = slice(s * NSA_KV, (s + 1) * NSA_KV)
        qo_ref[:, sl] = (head_norm(q_ref[:, sl], qg_ref[...]) * scale).astype(qo_ref.dtype)
    kso_ref[...] = head_norm(ks_ref[...], ksg_ref[...]).astype(kso_ref.dtype)
    kwo_ref[...] = head_norm(kw_ref[...], kwg_ref[...]).astype(kwo_ref.dtype)
    go_ref[...] = jax.nn.sigmoid(gt_ref[...])


def _nsa_prep(u2, q_gain, ks_gain, kw_gain, tm=512):
    n = u2.shape[0]
    kvb = COL_KV // NSA_KV
    gain = pl.BlockSpec((1, NSA_KV), lambda i: (0, 0))
    return pl.pallas_call(
        _nsa_prep_kernel,
        out_shape=(jax.ShapeDtypeStruct((n, NSA_WIDTH), BF16), jax.ShapeDtypeStruct((n, NSA_KV), BF16),
                   jax.ShapeDtypeStruct((n, NSA_KV), BF16), jax.ShapeDtypeStruct((n, LANES), F32)),
        grid=(n // tm,),
        in_specs=[
            pl.BlockSpec((tm, NSA_WIDTH), lambda i: (i, COL_Q // NSA_WIDTH)),
            pl.BlockSpec((tm, NSA_KV), lambda i: (i, kvb + 2)),
            pl.BlockSpec((tm, NSA_KV), lambda i: (i, kvb + 4)),
            pl.BlockSpec((tm, LANES), lambda i: (i, (COL_MISC + MISC_GATE) // LANES)),
            gain, gain, gain,
        ],
        out_specs=(pl.BlockSpec((tm, NSA_WIDTH), lambda i: (i, 0)), pl.BlockSpec((tm, NSA_KV), lambda i: (i, 0)),
                   pl.BlockSpec((tm, NSA_KV), lambda i: (i, 0)), pl.BlockSpec((tm, LANES), lambda i: (i, 0))),
        compiler_params=_cp(("arbitrary",)),
        name="nsa_prep",
    )(u2, u2, u2, u2, q_gain, ks_gain, kw_gain)


def _nsa_cmp_kernel(zk_ref, zv_ref, pk_ref, pv_ref, kw1_ref, kb1_ref, kw2_ref, kb2_ref, vw1_ref, vb1_ref,
                    vw2_ref, vb2_ref, kn_ref, ko_ref, vo_ref):
    nz = zk_ref.shape[2]
    half = kw1_ref.shape[0] // 2

    def compress(z, pos, w1_ref, b1, w2, b2):
        top = _dot(z, w1_ref[0:half, :])
        bot = _dot(z, w1_ref[half:, :])
        c1 = _dot(jnp.broadcast_to(pos, (8, pos.shape[1])).astype(BF16), w1_ref[...])[0:1] + b1
        hid = jax.nn.gelu(top + pltpu.roll(bot, nz - 1, 0) + c1)
        return _dot(hid.astype(BF16), w2) + b2

    kc = compress(zk_ref[0, 0], pk_ref[...], kw1_ref, kb1_ref[...], kw2_ref[...], kb2_ref[...])
    ms = jnp.mean(kc * kc, axis=-1, keepdims=True)
    ko_ref[0, 0] = (kc * lax.rsqrt(ms + RMS_EPS) * kn_ref[...]).astype(ko_ref.dtype)
    vc = compress(zv_ref[0, 0], pv_ref[...], vw1_ref, vb1_ref[...], vw2_ref[...], vb2_ref[...])
    vo_ref[0, 0] = vc.astype(vo_ref.dtype)


def _nsa_cmp(zk, zv, pos_k, pos_v, kw1, kb1, kw2, kb2, vw1, vb1, vw2, vb2, kn0):
    b, g, nz, zw = zk.shape
    full = lambda a: pl.BlockSpec(a.shape, lambda bi, gi: (0,) * a.ndim)
    zspec = pl.BlockSpec((1, 1, nz, zw), lambda bi, gi: (bi, gi, 0, 0))
    ospec = pl.BlockSpec((1, 1, nz, NSA_HEAD), lambda bi, gi: (bi, gi, 0, 0))
    params = (pos_k, pos_v, kw1, kb1, kw2, kb2, vw1, vb1, vw2, vb2, kn0)
    return pl.pallas_call(
        _nsa_cmp_kernel,
        out_shape=(jax.ShapeDtypeStruct((b, g, nz, NSA_HEAD), BF16),) * 2,
        grid=(b, g),
        in_specs=[zspec, zspec] + [full(a) for a in params],
        out_specs=(ospec, ospec),
        compiler_params=_cp(("arbitrary", "arbitrary")),
        name="nsa_compress",
    )(zk, zv, *params)


def _nsa_select_kernel(q_ref, kc_ref, vc_ref, oc_ref, sel_ref, *, n_sel):
    tq = q_ref.shape[3]
    nz = kc_ref.shape[2]
    ns = sel_ref.shape[3]
    t0 = pl.program_id(2) * tq
    n_idx = _iota((nz, tq), 0)
    t_idx = t0 + _iota((nz, tq), 1)
    cmask = (n_idx * CMP_STRIDE + (CMP_LEN - 1) <= t_idx) & (n_idx < nz - 1)
    kc = kc_ref[0, 0]
    vc = vc_ref[0, 0]
    psum = jnp.zeros((nz, tq), F32)
    for r in range(NSA_R):
        s = jnp.where(cmask, _dot_nt(kc, q_ref[0, 0, r]), -jnp.inf)
        mx = jnp.max(s, axis=0, keepdims=True)
        e = jnp.exp(s - jnp.where(mx == -jnp.inf, 0.0, mx))
        p = e / jnp.maximum(jnp.sum(e, axis=0, keepdims=True), 1e-30)
        psum = psum + p
        oc_ref[0, 0, r] = _dot(p.T.astype(BF16), vc)
    jb = _iota((ns, nz), 0) * SLC_BLOCK
    cb = _iota((ns, nz), 1) * CMP_STRIDE
    overlap = ((cb < jb + SLC_BLOCK) & (cb + CMP_LEN > jb) & (_iota((ns, nz), 1) < nz - 1)).astype(F32)
    imp = _dot(overlap, psum, HI)
    j_idx = _iota((ns, tq), 0)
    t_q = t0 + _iota((ns, tq), 1)
    qblk = t_q // SLC_BLOCK
    forced = (j_idx == 0) | (j_idx == qblk) | (j_idx == qblk - 1)
    imp = jnp.where(forced, jnp.inf, imp)
    imp = jnp.where(j_idx * SLC_BLOCK <= t_q, imp, -jnp.inf)
    rank = jnp.zeros((ns, tq), F32)
    for i in range(ns):
        row = imp[i:i + 1, :]
        before = (row > imp) | ((row == imp) & (j_idx > i))
        rank = rank + before.astype(F32)
    sel = ((rank < n_sel) & (imp > -jnp.inf)).astype(F32)
    sel_ref[0, 0] = sel.T


def _nsa_select(q5, kc, vc, tq=128):
    b, g, r, t, dk = q5.shape
    nz = kc.shape[2]
    ns = t // SLC_BLOCK
    return pl.pallas_call(
        functools.partial(_nsa_select_kernel, n_sel=min(N_SELECT, ns)),
        out_shape=(jax.ShapeDtypeStruct((b, g, r, t, dk), F32), jax.ShapeDtypeStruct((b, g, t, ns), F32)),
        grid=(b, g, t // tq),
        in_specs=[
            pl.BlockSpec((1, 1, r, tq, dk), lambda bi, gi, i: (bi, gi, 0, i, 0)),
            pl.BlockSpec((1, 1, nz, dk), lambda bi, gi, i: (bi, gi, 0, 0)),
            pl.BlockSpec((1, 1, nz, dk), lambda bi, gi, i: (bi, gi, 0, 0)),
        ],
        out_specs=(pl.BlockSpec((1, 1, r, tq, dk), lambda bi, gi, i: (bi, gi, 0, i, 0)),
                   pl.BlockSpec((1, 1, tq, ns), lambda bi, gi, i: (bi, gi, i, 0))),
        compiler_params=_cp(("arbitrary", "arbitrary", "arbitrary")),
        name="nsa_select",
    )(q5, kc, vc)


def _nsa_attn_kernel(q_ref, ks_ref, vs_ref, kw_ref, vw_ref, sel_ref, oc_ref, gt_ref, o_ref,
                     m_ref, l_ref, acc_ref, *, kt, kwt):
    r, tq, dk = q_ref.shape[2:]
    ns = sel_ref.shape[3]
    rows = r * tq
    i = pl.program_id(2)
    t0 = i * tq
    q2 = q_ref[0, 0].reshape(rows, dk)
    tpos = t0 + _iota((rows, 1), 0) % tq
    sel_t = sel_ref[0, 0].astype(BF16)

    def reset():
        m_ref[...] = jnp.full_like(m_ref, NEG)
        l_ref[...] = jnp.zeros_like(l_ref)
        acc_ref[...] = jnp.zeros_like(acc_ref)

    def online_update(s, ok, v):
        s = jnp.where(ok, s, NEG)
        m_old = m_ref[...]
        m_new = jnp.maximum(m_old, jnp.max(s, axis=1, keepdims=True))
        alpha = jnp.exp(m_old - m_new)
        p = jnp.where(ok, jnp.exp(s - m_new), 0.0)
        l_ref[...] = alpha * l_ref[...] + jnp.sum(p, axis=1, keepdims=True)
        acc_ref[...] = alpha * acc_ref[...] + _dot(p.astype(BF16), v)
        m_ref[...] = m_new

    def result():
        return acc_ref[...] / jnp.maximum(l_ref[...], 1e-30)

    reset()

    def sel_body(j, carry):
        k = ks_ref[0, 0, pl.ds(pl.multiple_of(j * kt, kt), kt), :]
        v = vs_ref[0, 0, pl.ds(pl.multiple_of(j * kt, kt), kt), :]
        s = _dot_nt(q2, k)
        blk = j * (kt // SLC_BLOCK) + _iota((ns, kt), 1) // SLC_BLOCK
        expand = (_iota((ns, kt), 0) == blk).astype(BF16)
        bm = _dot(sel_t, expand)
        bm = jnp.concatenate([bm] * r, axis=0)
        kpos = j * kt + _iota((1, kt), 1)
        online_update(s, (bm > 0.5) & (kpos <= tpos), v)
        return carry

    lax.fori_loop(0, (t0 + tq + kt - 1) // kt, sel_body, 0)
    o_s = result()

    reset()

    def win_body(j, carry):
        k = kw_ref[0, 0, pl.ds(pl.multiple_of(j * kwt, kwt), kwt), :]
        v = vw_ref[0, 0, pl.ds(pl.multiple_of(j * kwt, kwt), kwt), :]
        s = _dot_nt(q2, k)
        kpos = j * kwt + _iota((1, kwt), 1)
        online_update(s, (kpos <= tpos) & (kpos > tpos - WINDOW), v)
        return carry

    lo = jnp.maximum(t0 - (WINDOW - 1), 0) // kwt
    lax.fori_loop(lo, (t0 + tq - 1) // kwt + 1, win_body, 0)
    o_w = result()

    gt = gt_ref[0, 0].reshape(rows, 3)
    out = gt[:, 0:1] * oc_ref[0, 0].reshape(rows, dk) + gt[:, 1:2] * o_s + gt[:, 2:3] * o_w
    o_ref[0, 0] = out.reshape(r, tq, dk).astype(o_ref.dtype)


def _nsa_attn(q5, ks, vs, kw, vw, sel, oc, gates, tq=128):
    b, g, r, t, dk = q5.shape
    ns = sel.shape[3]
    kt = min(256, t)
    kwt = min(128, t)
    qspec = pl.BlockSpec((1, 1, r, tq, dk), lambda bi, gi, i: (bi, gi, 0, i, 0))
    kvspec = pl.BlockSpec((1, 1, t, dk), lambda bi, gi, i: (bi, gi, 0, 0))
    rows = r * tq
    return pl.pallas_call(
        functools.partial(_nsa_attn_kernel, kt=kt, kwt=kwt),
        out_shape=jax.ShapeDtypeStruct((b, g, r, t, dk), BF16),
        grid=(b, g, t // tq),
        in_specs=[
            qspec, kvspec, kvspec, kvspec, kvspec,
            pl.BlockSpec((1, 1, tq, ns), lambda bi, gi, i: (bi, gi, i, 0)),
            qspec,
            pl.BlockSpec((1, 1, r, tq, 3), lambda bi, gi, i: (bi, gi, 0, i, 0)),
        ],
        out_specs=qspec,
        scratch_shapes=[pltpu.VMEM((rows, 1), F32), pltpu.VMEM((rows, 1), F32), pltpu.VMEM((rows, dk), F32)],
        compiler_params=_cp(("arbitrary", "arbitrary", "arbitrary")),
        name="nsa_attention",
    )(q5, ks, vs, kw, vw, sel, oc, gates)


def _nsa(u3, q_norm, k_norm, pos_k, pos_v, ck_w1, ck_b1, ck_w2, ck_b2, cv_w1, cv_b1, cv_w2, cv_b2):
    b, t, _ = u3.shape
    n = b * t
    g, r, dk = NSA_G, NSA_R, NSA_HEAD
    tile_g = lambda p: jnp.tile(p.astype(F32), g).reshape(1, g * dk)
    qn, ksn, kwn, gsig = _nsa_prep(u3.reshape(n, IN_COLS), tile_g(q_norm), tile_g(k_norm[1]), tile_g(k_norm[2]))
    per_group = lambda a: a.reshape(b, t, g, dk).transpose(0, 2, 1, 3)
    kv = lambda idx: u3[..., COL_KV + idx * NSA_KV:COL_KV + (idx + 1) * NSA_KV]
    q5 = qn.reshape(b, t, g, r, dk).transpose(0, 2, 3, 1, 4)
    nz = t // CMP_STRIDE
    zk = per_group(kv(0).astype(BF16)).reshape(b, g, nz, CMP_STRIDE * dk)
    zv = per_group(kv(1).astype(BF16)).reshape(b, g, nz, CMP_STRIDE * dk)
    row = lambda p: p.reshape(1, -1).astype(F32)
    kc, vc = _nsa_cmp(zk, zv, row(pos_k), row(pos_v), ck_w1.astype(BF16), row(ck_b1), ck_w2.astype(BF16), row(ck_b2),
                      cv_w1.astype(BF16), row(cv_b1), cv_w2.astype(BF16), row(cv_b2), row(k_norm[0]))
    oc, sel = _nsa_select(q5, kc, vc)
    gates = gsig[:, :g * r * 3].reshape(b, t, g, r, 3).transpose(0, 2, 3, 1, 4)
    o5 = _nsa_attn(q5, per_group(ksn), per_group(kv(3).astype(BF16)), per_group(kwn), per_group(kv(5).astype(BF16)),
                   sel, oc, gates)
    return o5.transpose(0, 3, 1, 2, 4).reshape(b, t, NSA_WIDTH)


def _outproj_kernel(x_ref, yr_ref, yn_ref, w_ref, g_ref, rw_ref, rb_ref,
                    h_ref, xn_ref, e_ref, gate_ref, rank_ref, cnt_ref, cnt_acc):
    tm = x_ref.shape[0]

    @pl.when(pl.program_id(0) == 0)
    def _():
        cnt_acc[...] = jnp.zeros_like(cnt_acc)

    h = x_ref[...] + _dot(yr_ref[...], w_ref[0:RW_WIDTH, :]) + _dot(yn_ref[...], w_ref[RW_WIDTH:, :])
    h_ref[...] = h
    ms = jnp.mean(h * h, axis=-1, keepdims=True)
    xn = h * lax.rsqrt(ms + RMS_EPS) * g_ref[...]
    xn_ref[...] = xn
    logits = _dot(xn, rw_ref[...], HI) + rb_ref[...]
    lane = _iota((tm, LANES), 1).astype(F32)
    work = logits
    top_e, top_v = [], []
    for _ in range(TOP_K):
        mx = jnp.max(work, axis=1, keepdims=True)
        idx = jnp.min(jnp.where(work == mx, lane, float(LANES)), axis=1, keepdims=True)
        top_e.append(idx)
        top_v.append(mx)
        work = jnp.where(lane == idx, -jnp.inf, work)
    ex = [jnp.exp(v - top_v[0]) for v in top_v]
    den = ex[0] + ex[1] + ex[2] + ex[3]
    multihot = jnp.zeros((tm, LANES), F32)
    for e in top_e:
        multihot = multihot + (lane == e).astype(F32)
    strict = (_iota((tm, tm), 1) < _iota((tm, tm), 0)).astype(BF16)
    before = cnt_acc[...] + _dot(strict, multihot.astype(BF16))
    e_out = jnp.zeros((tm, LANES), F32)
    g_out = jnp.zeros((tm, LANES), F32)
    r_out = jnp.zeros((tm, LANES), F32)
    for k in range(TOP_K):
        slot = lane == float(k)
        rank_k = jnp.sum(jnp.where(lane == top_e[k], before, 0.0), axis=1, keepdims=True)
        e_out = jnp.where(slot, top_e[k], e_out)
        g_out = jnp.where(slot, ex[k] / den, g_out)
        r_out = jnp.where(slot, rank_k, r_out)
    e_ref[...] = e_out.astype(I32)
    gate_ref[...] = g_out
    rank_ref[...] = r_out.astype(I32)
    cnt_acc[...] = cnt_acc[...] + jnp.sum(multihot, axis=0, keepdims=True)
    cnt_ref[...] = cnt_acc[...].astype(I32)


def _outproj(x2, y_rw, y_nsa, w_out_bf, g, rw_p, rb_p, tm=256):
    n, d = x2.shape
    rowblk = lambda w: pl.BlockSpec((tm, w), lambda i: (i, 0))
    full = lambda a: pl.BlockSpec(a.shape, lambda i: (0,) * a.ndim)
    return pl.pallas_call(
        _outproj_kernel,
        out_shape=(jax.ShapeDtypeStruct((n, d), F32), jax.ShapeDtypeStruct((n, d), F32),
                   jax.ShapeDtypeStruct((n, LANES), I32), jax.ShapeDtypeStruct((n, LANES), F32),
                   jax.ShapeDtypeStruct((n, LANES), I32), jax.ShapeDtypeStruct((1, LANES), I32)),
        grid=(n // tm,),
        in_specs=[rowblk(d), rowblk(RW_WIDTH), rowblk(NSA_WIDTH), full(w_out_bf), full(g), full(rw_p), full(rb_p)],
        out_specs=(rowblk(d), rowblk(d), rowblk(LANES), rowblk(LANES), rowblk(LANES),
                   pl.BlockSpec((1, LANES), lambda i: (0, 0))),
        scratch_shapes=[pltpu.VMEM((1, LANES), F32)],
        compiler_params=_cp(("arbitrary",)),
        name="outproj_router",
    )(x2, y_rw, y_nsa, w_out_bf, g, rw_p, rb_p)


def _gather_kernel(row_tok, n_act, x_hbm, o_ref, sem):
    tg = o_ref.shape[0]
    base = pl.program_id(0) * tg

    def row_copy(r):
        return pltpu.make_async_copy(x_hbm.at[pl.ds(row_tok[base + r], 1), :], o_ref.at[pl.ds(r, 1), :], sem)

    @pl.when(base < n_act[0])
    def _():
        def issue(r, c):
            row_copy(r).start()
            return c

        def drain(r, c):
            row_copy(r).wait()
            return c

        lax.fori_loop(0, tg, issue, 0)
        lax.fori_loop(0, tg, drain, 0)

    @pl.when(base >= n_act[0])
    def _():
        o_ref[...] = jnp.zeros_like(o_ref)


def _gather_rows(row_tok, n_act, xn, n_rows, tg=256):
    d = xn.shape[1]
    return pl.pallas_call(
        _gather_kernel,
        out_shape=jax.ShapeDtypeStruct((n_rows, d), xn.dtype),
        grid_spec=pltpu.PrefetchScalarGridSpec(
            num_scalar_prefetch=2,
            grid=(n_rows // tg,),
            in_specs=[pl.BlockSpec(memory_space=pl.ANY)],
            out_specs=pl.BlockSpec((tg, d), lambda i, rt, na: (i, 0)),
            scratch_shapes=[pltpu.SemaphoreType.DMA],
        ),
        compiler_params=_cp(("arbitrary",)),
        name="moe_gather",
    )(row_tok, n_act, xn)


def _gm1_kernel(e_s, j_s, rt_s, valid_s, first_s, rto_s, jo_s, x_ref, wg_ref, wl_ref, bg_ref, bl_ref, o_ref,
                wgb, wlb):
    s = pl.program_id(0)

    @pl.when(valid_s[s] == 1)
    def _():
        @pl.when(first_s[s] == 1)
        def _():
            wgb[...] = wg_ref[0].astype(BF16)
            wlb[...] = wl_ref[0].astype(BF16)

        x = x_ref[...].astype(BF16)
        gt = jnp.minimum(_dot(x, wgb[...]) + bg_ref[0], SWIGLU_LIMIT)
        lin = jnp.clip(_dot(x, wlb[...]) + bl_ref[0], -SWIGLU_LIMIT, SWIGLU_LIMIT)
        o_ref[...] = (gt * jax.nn.sigmoid(SWIGLU_ALPHA * gt) * (lin + 1.0)).astype(o_ref.dtype)

    @pl.when(valid_s[s] == 0)
    def _():
        o_ref[...] = jnp.zeros_like(o_ref)


def _gm2_kernel(e_s, j_s, rt_s, valid_s, first_s, rto_s, jo_s, x_ref, w_ref, b_ref, o_ref, wb):
    s = pl.program_id(0)

    @pl.when(valid_s[s] == 1)
    def _():
        @pl.when(first_s[s] == 1)
        def _():
            wb[...] = w_ref[0].astype(BF16)

        o_ref[...] = _dot(x_ref[...], wb[...]) + b_ref[0]

    @pl.when(valid_s[s] == 0)
    def _():
        o_ref[...] = jnp.zeros_like(o_ref)


def _schedule(counts, tr, n_col_tiles, max_tiles):
    tiles_e = (counts + tr - 1) // tr
    tile_end = jnp.cumsum(tiles_e)
    tile_start = tile_end - tiles_e
    steps_e = tiles_e * n_col_tiles
    step_end = jnp.cumsum(steps_e)
    total = step_end[-1]
    n_steps = max_tiles * n_col_tiles
    step = jnp.arange(n_steps, dtype=I32)
    s = jnp.minimum(step, total - 1)
    e = jnp.searchsorted(step_end, s, side="right").astype(I32)
    local = s - (step_end[e] - steps_e[e])
    te = jnp.maximum(tiles_e[e], 1)
    j = (local // te).astype(I32)
    rt = (tile_start[e] + local % te).astype(I32)
    valid = step < total
    first = ((local % te) == 0).astype(I32)
    spare = step - total
    rt_out = jnp.where(valid, rt, tile_end[-1] + spare // n_col_tiles).astype(I32)
    j_out = jnp.where(valid, j, spare % n_col_tiles).astype(I32)
    return (e, j, rt, valid.astype(I32), first, rt_out, j_out), tile_start * tr, tile_end[-1] * tr


def _grouped_mlp(x_rows, sched1, sched2, w1, b1, w2, b2, tr, tn1, tn2):
    n_rows, d = x_rows.shape
    de = w2.shape[1]
    nt1 = de // tn1
    b1r = b1.reshape(N_EXPERTS, 1, 2 * de)
    b2r = b2.reshape(N_EXPERTS, 1, d)
    act = pl.pallas_call(
        _gm1_kernel,
        out_shape=jax.ShapeDtypeStruct((n_rows, de), BF16),
        grid_spec=pltpu.PrefetchScalarGridSpec(
            num_scalar_prefetch=7,
            grid=(sched1[0].shape[0],),
            in_specs=[
                pl.BlockSpec((tr, d), lambda s, e, j, rt, v, f, ro, jo: (rt[s], 0)),
                pl.BlockSpec((1, d, tn1), lambda s, e, j, rt, v, f, ro, jo: (e[s], 0, j[s])),
                pl.BlockSpec((1, d, tn1), lambda s, e, j, rt, v, f, ro, jo: (e[s], 0, nt1 + j[s])),
                pl.BlockSpec((1, 1, tn1), lambda s, e, j, rt, v, f, ro, jo: (e[s], 0, j[s])),
                pl.BlockSpec((1, 1, tn1), lambda s, e, j, rt, v, f, ro, jo: (e[s], 0, nt1 + j[s])),
            ],
            out_specs=pl.BlockSpec((tr, tn1), lambda s, e, j, rt, v, f, ro, jo: (ro[s], jo[s])),
            scratch_shapes=[pltpu.VMEM((d, tn1), BF16), pltpu.VMEM((d, tn1), BF16)],
        ),
        compiler_params=_cp(("arbitrary",)),
        name="moe_up",
    )(*sched1, x_rows, w1, w1, b1r, b1r)
    return pl.pallas_call(
        _gm2_kernel,
        out_shape=jax.ShapeDtypeStruct((n_rows, d), F32),
        grid_spec=pltpu.PrefetchScalarGridSpec(
            num_scalar_prefetch=7,
            grid=(sched2[0].shape[0],),
            in_specs=[
                pl.BlockSpec((tr, de), lambda s, e, j, rt, v, f, ro, jo: (rt[s], 0)),
                pl.BlockSpec((1, de, tn2), lambda s, e, j, rt, v, f, ro, jo: (e[s], 0, j[s])),
                pl.BlockSpec((1, 1, tn2), lambda s, e, j, rt, v, f, ro, jo: (e[s], 0, j[s])),
            ],
            out_specs=pl.BlockSpec((tr, tn2), lambda s, e, j, rt, v, f, ro, jo: (ro[s], jo[s])),
            scratch_shapes=[pltpu.VMEM((de, tn2), BF16)],
        ),
        compiler_params=_cp(("arbitrary",)),
        name="moe_down",
    )(*sched2, act, w2, b2r)


def _ple_kernel(dest, h_ref, gate_ref, p_ref, y_hbm, g_ref, gw_ref, pw_ref, o_ref, buf, sem):
    tm = h_ref.shape[0]
    base = pl.program_id(0) * tm

    def row_copy(t, k):
        return pltpu.make_async_copy(y_hbm.at[pl.ds(dest[(base + t) * TOP_K + k], 1), :],
                                     buf.at[k, pl.ds(t, 1), :], sem)

    def issue(t, c):
        for k in range(TOP_K):
            row_copy(t, k).start()
        return c

    def drain(t, c):
        for k in range(TOP_K):
            row_copy(t, k).wait()
        return c

    lax.fori_loop(0, tm, issue, 0)
    lax.fori_loop(0, tm, drain, 0)
    gate = gate_ref[...]
    h = h_ref[...]
    for k in range(TOP_K):
        h = h + gate[:, k:k + 1] * buf[k]
    ms = jnp.mean(h * h, axis=-1, keepdims=True)
    hn = (h * lax.rsqrt(ms + RMS_EPS) * g_ref[...]).astype(BF16)
    pgate = jax.nn.sigmoid(_dot(hn, gw_ref[...]))
    o_ref[...] = h + pgate * _dot(p_ref[...].astype(BF16), pw_ref[...])


def _combine_ple(dest, h1, gate, p2, y_rows, g, gw_bf, pw_bf, tm=128):
    n, d = h1.shape
    rowblk = lambda w: pl.BlockSpec((tm, w), lambda i, ds: (i, 0))
    full = lambda a: pl.BlockSpec(a.shape, lambda i, ds: (0,) * a.ndim)
    return pl.pallas_call(
        _ple_kernel,
        out_shape=jax.ShapeDtypeStruct((n, d), F32),
        grid_spec=pltpu.PrefetchScalarGridSpec(
            num_scalar_prefetch=1,
            grid=(n // tm,),
            in_specs=[rowblk(d), rowblk(LANES), rowblk(p2.shape[1]), pl.BlockSpec(memory_space=pl.ANY),
                      full(g), full(gw_bf), full(pw_bf)],
            out_specs=rowblk(d),
            scratch_shapes=[pltpu.VMEM((TOP_K, tm, d), F32), pltpu.SemaphoreType.DMA],
        ),
        compiler_params=_cp(("arbitrary",)),
        name="combine_ple",
    )(dest, h1, gate, p2, y_rows, g, gw_bf, pw_bf)


MOE_TR = 512
MOE_TN1 = 512
MOE_TN2 = 1024


def _layer(x, p, mix_norm_g, w_in, rw, nsa, w_out, moe_norm_g, router_w, router_b, moe_w1, moe_b1, moe_w2, moe_b2,
           ple_norm_g, ple_w, ple_gate_w):
    b, t, d = x.shape
    n = b * t
    x2 = x.reshape(n, d)
    row = lambda v: v.reshape(1, -1).astype(F32)
    rw_cols = 3 * RW_WIDTH + sum(RW_LORA)
    w_p = jnp.zeros((d, IN_COLS), BF16)
    w_p = w_p.at[:, COL_RKV:COL_RKV + 3 * RW_WIDTH].set(w_in[:, :3 * RW_WIDTH].astype(BF16))
    w_p = w_p.at[:, COL_MISC:COL_MISC + sum(RW_LORA)].set(w_in[:, 3 * RW_WIDTH:rw_cols].astype(BF16))
    w_p = w_p.at[:, COL_Q:COL_Q + NSA_WIDTH].set(w_in[:, rw_cols:rw_cols + NSA_WIDTH].astype(BF16))
    kv0 = rw_cols + NSA_WIDTH
    w_p = w_p.at[:, COL_KV:COL_KV + 6 * NSA_KV].set(w_in[:, kv0:kv0 + 6 * NSA_KV].astype(BF16))
    n_gate = 3 * NSA_G * NSA_R
    w_p = w_p.at[:, COL_MISC + MISC_GATE:COL_MISC + MISC_GATE + n_gate].set(w_in[:, kv0 + 6 * NSA_KV:].astype(BF16))
    u = _inproj(x2, row(mix_norm_g), w_p)
    u3 = u.reshape(b, t, IN_COLS)
    y_rw = _rwkv(u3, *_rwkv_params(*rw))
    y_nsa = _nsa(u3, *nsa)
    out = _tail(x2, y_rw.reshape(n, RW_WIDTH), y_nsa.reshape(n, NSA_WIDTH), p.reshape(n, PLE_DIM), w_out, moe_norm_g,
                router_w, router_b, moe_w1, moe_b1, moe_w2, moe_b2, ple_norm_g, ple_w, ple_gate_w)
    return out.reshape(b, t, d)


def _tail(x2, y_rw, y_nsa, p2, w_out, moe_norm_g, router_w, router_b, moe_w1, moe_b1, moe_w2, moe_b2,
          ple_norm_g, ple_w, ple_gate_w):
    n, d = x2.shape
    row = lambda v: v.reshape(1, -1).astype(F32)
    rw_p = jnp.zeros((d, LANES), F32).at[:, :N_EXPERTS].set(router_w)
    rb_p = jnp.full((1, LANES), NEG, F32).at[0, :N_EXPERTS].set(router_b)
    h1, xn, top_e, gate, rank, counts = _outproj(x2, y_rw, y_nsa, w_out.astype(BF16), row(moe_norm_g), rw_p, rb_p)
    counts = counts[0, :N_EXPERTS]
    max_tiles = (n * TOP_K) // MOE_TR + N_EXPERTS
    n_rows = max_tiles * MOE_TR
    sched1, pstart, n_act = _schedule(counts, MOE_TR, D_EXPERT // MOE_TN1, max_tiles)
    sched2, _, _ = _schedule(counts, MOE_TR, d // MOE_TN2, max_tiles)
    top_e = top_e[:, :TOP_K]
    dest = (pstart[top_e] + rank[:, :TOP_K]).astype(I32)
    tok = jnp.broadcast_to(jnp.arange(n, dtype=I32)[:, None], (n, TOP_K))
    row_tok = jnp.zeros((n_rows,), I32).at[dest.reshape(-1)].set(tok.reshape(-1))
    x_rows = _gather_rows(row_tok, n_act.reshape(1).astype(I32), xn, n_rows)
    y_rows = _grouped_mlp(x_rows, sched1, sched2, moe_w1, moe_b1, moe_w2, moe_b2, MOE_TR, MOE_TN1, MOE_TN2)
    return _combine_ple(dest.reshape(-1), h1, gate, p2, y_rows, row(ple_norm_g),
                        ple_gate_w.astype(BF16), ple_w.astype(BF16))


def kernel(x, p, mix_norm_g, w_in, rw_mu, rw_w0, rw_w2, rw_a0, rw_a2, rw_g2, rw_k_k, rw_k_a, rw_r_k, rw_lnx_w,
           rw_lnx_b, nsa_q_norm, nsa_k_norm, cmp_pos_k, cmp_pos_v, cmp_k_w1, cmp_k_b1, cmp_k_w2, cmp_k_b2, cmp_v_w1,
           cmp_v_b1, cmp_v_w2, cmp_v_b2, w_out, moe_norm_g, router_w, router_b, moe_w1, moe_b1, moe_w2, moe_b2,
           ple_norm_g, ple_w, ple_gate_w):
    h = x
    for i in range(mix_norm_g.shape[0]):
        rw = (rw_mu[i], rw_w0[i], rw_w2[i], rw_a0[i], rw_a2[i], rw_g2[i], rw_k_k[i], rw_k_a[i], rw_r_k[i],
              rw_lnx_w[i], rw_lnx_b[i])
        nsa = (nsa_q_norm[i], nsa_k_norm[i], cmp_pos_k[i], cmp_pos_v[i], cmp_k_w1[i], cmp_k_b1[i], cmp_k_w2[i],
               cmp_k_b2[i], cmp_v_w1[i], cmp_v_b1[i], cmp_v_w2[i], cmp_v_b2[i])
        h = _layer(h, p[i], mix_norm_g[i], w_in[i], rw, nsa, w_out[i], moe_norm_g[i], router_w[i], router_b[i],
                   moe_w1[i], moe_b1[i], moe_w2[i], moe_b2[i], ple_norm_g[i], ple_w[i], ple_gate_w[i])
    return h
```

```python
import functools

import jax
import jax.numpy as jnp
from jax import lax
from jax.experimental import pallas as pl
from jax.experimental.pallas import tpu as pltpu

F32 = jnp.float32
BF16 = jnp.bfloat16
I32 = jnp.int32
HI = lax.Precision.HIGHEST

LANES = 128
VMEM_LIMIT = 56 * 1024 * 1024

D_MODEL = 2048
RMS_EPS = 1e-6
RW_WIDTH = 1024
RW_HEAD = 64
RW_LORA = (64, 64, 160)
RW_GN_EPS = 64e-5
NSA_WIDTH = 1024
NSA_HEAD = 64
NSA_G = 4
NSA_R = 4
NSA_KV = NSA_G * NSA_HEAD
CMP_LEN = 32
CMP_STRIDE = 16
CMP_HIDDEN = 256
SLC_BLOCK = 64
N_SELECT = 16
WINDOW = 512
N_EXPERTS = 32
TOP_K = 4
D_EXPERT = 2048
SWIGLU_LIMIT = 7.0
SWIGLU_ALPHA = 1.702
PLE_DIM = 256

COL_RKV = 0
COL_Q = 3072
COL_KV = 4096
COL_MISC = 5632
MISC_W = 512
MISC_GATE = 384
IN_COLS = 6144

RW_HG = 4
RW_LW = RW_HG * RW_HEAD
RW_C = 64
NSA_KT = 256
NSA_KWT = 128
NEG = -1e30


def _cp(sem, vmem=VMEM_LIMIT):
    return pltpu.CompilerParams(dimension_semantics=sem, vmem_limit_bytes=vmem)


def _dot(a, b, prec=None):
    return jnp.dot(a, b, preferred_element_type=F32, precision=prec)


def _dot_nt(a, b, prec=None):
    return lax.dot_general(a, b, (((1,), (1,)), ((), ())), preferred_element_type=F32, precision=prec)


def _iota(shape, dim):
    return lax.broadcasted_iota(I32, shape, dim)


def _seg_matrix(n, seg, scale):
    same = (_iota((n, n), 0) // seg) == (_iota((n, n), 1) // seg)
    return jnp.where(same, scale, 0.0).astype(F32)


def _inproj_kernel(x_ref, g_ref, w_ref, o_ref, xn_ref):
    @pl.when(pl.program_id(1) == 0)
    def _():
        x = x_ref[...]
        ms = jnp.mean(x * x, axis=-1, keepdims=True)
        xn_ref[...] = (x * lax.rsqrt(ms + RMS_EPS) * g_ref[...]).astype(BF16)

    o_ref[...] = _dot(xn_ref[...], w_ref[...])


def _inproj(x2, g, w_bf, tm=512, tn=512):
    n, d = x2.shape
    nc = w_bf.shape[1]
    return pl.pallas_call(
        _inproj_kernel,
        out_shape=jax.ShapeDtypeStruct((n, nc), F32),
        grid=(n // tm, nc // tn),
        in_specs=[
            pl.BlockSpec((tm, d), lambda i, j: (i, 0)),
            pl.BlockSpec((1, d), lambda i, j: (0, 0)),
            pl.BlockSpec((d, tn), lambda i, j: (0, j)),
        ],
        out_specs=pl.BlockSpec((tm, tn), lambda i, j: (i, j)),
        scratch_shapes=[pltpu.VMEM((tm, d), BF16)],
        compiler_params=_cp(("arbitrary", "arbitrary")),
        name="inproj",
    )(x2, g, w_bf)


def _softplus(z):
    return jnp.maximum(z, 0.0) + jnp.log(1.0 + jnp.exp(-jnp.abs(z)))


def _rwkv_kernel(r_ref, k_ref, v_ref, m_ref, mur_ref, muk_ref, muv_ref, mum_ref, w2_ref, a2_ref, g2_ref,
                 w0_ref, a0_ref, kk_ref, ka_ref, rk_ref, lw_ref, lb_ref, o_ref,
                 st_ref, pr_ref, pk_ref, pv_ref, pm_ref, *, nchunk):
    tt = nchunk * RW_C
    lw = RW_LW

    @pl.when(pl.program_id(2) == 0)
    def _():
        st_ref[...] = jnp.zeros_like(st_ref)
        pr_ref[...] = jnp.zeros_like(pr_ref)
        pk_ref[...] = jnp.zeros_like(pk_ref)
        pv_ref[...] = jnp.zeros_like(pv_ref)
        pm_ref[...] = jnp.zeros_like(pm_ref)

    def shift_lerp(x, prev_ref, mu):
        rolled = pltpu.roll(x, 1, 0)
        first = _iota(x.shape, 0) == 0
        sh = jnp.where(first, prev_ref[0:1, :], rolled)
        prev_ref[0:1, :] = x[tt - 1:tt, :]
        return x + (sh - x) * mu

    r = shift_lerp(r_ref[0], pr_ref, mur_ref[...])
    k = shift_lerp(k_ref[0], pk_ref, muk_ref[...])
    v = shift_lerp(v_ref[0], pv_ref, muv_ref[...])
    m = shift_lerp(m_ref[0], pm_ref, mum_ref[...])

    lw_dec = _dot(jnp.tanh(m).astype(BF16), w2_ref[...])
    lw_a = _dot(m.astype(BF16), a2_ref[...])
    g = _dot(jax.nn.sigmoid(m).astype(BF16), g2_ref[...])
    w_log = -_softplus(-(w0_ref[...] + lw_dec)) - 0.5
    ld = -jnp.exp(w_log)
    a = jax.nn.sigmoid(a0_ref[...] + lw_a)

    ones_seg = _seg_matrix(lw, RW_HEAD, 1.0)
    kk = k * kk_ref[...]
    kk = kk / jnp.maximum(jnp.sqrt(_dot(kk * kk, ones_seg, HI)), 1e-12)
    k2 = k * (1.0 + (a - 1.0) * ka_ref[...])
    av = -kk
    bv = kk * a

    ri = _iota((lw, lw), 0)
    ci = _iota((lw, lw), 1)
    same_blk = (ri // RW_C) == (ci // RW_C)
    m_strict = same_blk & (ci < ri)
    m_lower = same_blk & (ci <= ri)
    eye = (ri == ci).astype(F32)
    tri = (_iota((RW_C, RW_C), 1) <= _iota((RW_C, RW_C), 0)).astype(F32)
    lane_head = _iota((1, lw), 1) // RW_HEAD
    hmask = [(lane_head == h).astype(F32) for h in range(RW_HG)]

    def stack_heads(x):
        return jnp.concatenate([x * hmask[h] for h in range(RW_HG)], axis=0)

    def unstack_heads(xs):
        out = xs[0:RW_C] * hmask[0]
        for h in range(1, RW_HG):
            out = out + xs[h * RW_C:(h + 1) * RW_C] * hmask[h]
        return out

    ys = []
    st = st_ref[...]
    for c in range(nchunk):
        sl = slice(c * RW_C, (c + 1) * RW_C)
        ld_c = ld[sl]
        cs = _dot(tri, ld_c, HI)
        w_in = jnp.exp(cs)
        w_inv = jnp.exp(-cs)
        w_prev = jnp.exp(cs - ld_c)
        at = av[sl] * w_prev
        bt = bv[sl] * w_inv
        kt = k2[sl] * w_inv
        rt = r[sl] * w_in
        v_c = v[sl]
        at_s = stack_heads(at).astype(BF16)
        rt_s = stack_heads(rt).astype(BF16)
        bt4 = jnp.concatenate([bt] * RW_HG, axis=0).astype(BF16)
        kt4 = jnp.concatenate([kt] * RW_HG, axis=0).astype(BF16)
        v4 = jnp.concatenate([v_c] * RW_HG, axis=0).astype(BF16)
        a_ab = jnp.where(m_strict, _dot_nt(at_s, bt4), 0.0)
        a_ak = jnp.where(m_strict, _dot_nt(at_s, kt4), 0.0)
        a_rb = jnp.where(m_lower, _dot_nt(rt_s, bt4), 0.0)
        a_rk = jnp.where(m_lower, _dot_nt(rt_s, kt4), 0.0)
        pw = a_ab
        inv = eye + a_ab
        for _ in range(RW_C.bit_length() - 2):
            pw_b = pw.astype(BF16)
            pw = _dot(pw_b, pw_b)
            inv = inv + _dot(inv.astype(BF16), pw.astype(BF16))
        st_b = st.astype(BF16)
        x0 = _dot_nt(at_s, st_b) + _dot(a_ak.astype(BF16), v4)
        u_s = _dot(inv.astype(BF16), x0.astype(BF16))
        y_s = _dot_nt(rt_s, st_b) + _dot(a_rb.astype(BF16), u_s.astype(BF16)) + _dot(a_rk.astype(BF16), v4)
        ys.append(unstack_heads(y_s))
        u_c = unstack_heads(u_s)
        w_last = w_in[RW_C - 1:RW_C, :]
        lhs = jnp.concatenate([u_c.T, v_c.T], axis=1).astype(BF16)
        rhs = jnp.concatenate([bt * w_last, kt * w_last], axis=0).astype(BF16)
        st = st * w_last + jnp.where(same_blk, _dot(lhs, rhs), 0.0)
    st_ref[...] = st
    y = jnp.concatenate(ys, axis=0) if nchunk > 1 else ys[0]

    avg_seg = ones_seg * (1.0 / RW_HEAD)
    mean = _dot(y, avg_seg, HI)
    yc = y - mean
    var = _dot(yc * yc, avg_seg, HI)
    yn = yc * lax.rsqrt(var + RW_GN_EPS) * lw_ref[...] + lb_ref[...]
    bonus = _dot(r * k2 * rk_ref[...], ones_seg, HI) * v
    o_ref[0] = ((yn + bonus) * g).astype(o_ref.dtype)


def _rwkv(u3, mu_p, w2p, a2p, g2p, w0, a0, k_k, k_a, r_k, lnx_w, lnx_b, nchunk=2):
    b, t, _ = u3.shape
    tt = nchunk * RW_C
    lw = RW_LW
    nhg = RW_WIDTH // lw
    col = lambda base: (lambda bi, h, ti: (bi, ti, base + h))
    vec = lambda base: (lambda bi, h, ti: (0, base + h))
    par = pl.BlockSpec((1, lw), vec(0))
    return pl.pallas_call(
        functools.partial(_rwkv_kernel, nchunk=nchunk),
        out_shape=jax.ShapeDtypeStruct((b, t, RW_WIDTH), BF16),
        grid=(b, nhg, t // tt),
        in_specs=[
            pl.BlockSpec((1, tt, lw), col(0)),
            pl.BlockSpec((1, tt, lw), col(nhg)),
            pl.BlockSpec((1, tt, lw), col(2 * nhg)),
            pl.BlockSpec((1, tt, MISC_W), lambda bi, h, ti: (bi, ti, COL_MISC // MISC_W)),
            pl.BlockSpec((1, lw), vec(0)),
            pl.BlockSpec((1, lw), vec(nhg)),
            pl.BlockSpec((1, lw), vec(2 * nhg)),
            pl.BlockSpec((1, MISC_W), lambda bi, h, ti: (0, COL_MISC // MISC_W)),
            pl.BlockSpec((MISC_W, lw), lambda bi, h, ti: (0, h)),
            pl.BlockSpec((MISC_W, lw), lambda bi, h, ti: (0, h)),
            pl.BlockSpec((MISC_W, lw), lambda bi, h, ti: (0, h)),
            par, par, par, par, par, par, par,
        ],
        out_specs=pl.BlockSpec((1, tt, lw), lambda bi, h, ti: (bi, ti, h)),
        scratch_shapes=[
            pltpu.VMEM((lw, lw), F32),
            pltpu.VMEM((8, lw), F32),
            pltpu.VMEM((8, lw), F32),
            pltpu.VMEM((8, lw), F32),
            pltpu.VMEM((8, MISC_W), F32),
        ],
        compiler_params=_cp(("arbitrary", "arbitrary", "arbitrary")),
        name="rwkv7",
    )(u3, u3, u3, u3, mu_p, mu_p, mu_p, mu_p, w2p, a2p, g2p, w0, a0, k_k, k_a, r_k, lnx_w, lnx_b)


def _pad_rows(w, rows, offset):
    return jnp.zeros((rows, w.shape[1]), w.dtype).at[offset:offset + w.shape[0]].set(w)


def _rwkv_params(mu, w0, w2, a0, a2, g2, k_k, k_a, r_k, lnx_w, lnx_b):
    mu_p = jnp.zeros((1, IN_COLS), F32)
    mu_p = mu_p.at[0, COL_RKV:COL_RKV + 3 * RW_WIDTH].set(mu[:3 * RW_WIDTH])
    mu_p = mu_p.at[0, COL_MISC:COL_MISC + sum(RW_LORA)].set(mu[3 * RW_WIDTH:])
    o1, o2 = RW_LORA[0], RW_LORA[0] + RW_LORA[1]
    row = lambda p: p.reshape(1, RW_WIDTH).astype(F32)
    return (mu_p, _pad_rows(w2, MISC_W, 0).astype(BF16), _pad_rows(a2, MISC_W, o1).astype(BF16),
            _pad_rows(g2, MISC_W, o2).astype(BF16), row(w0), row(a0), row(k_k), row(k_a), row(r_k),
            row(lnx_w), row(lnx_b))


def _nsa_prep_kernel(q_ref, ks_ref, kw_ref, gt_ref, qg_ref, ksg_ref, kwg_ref, qo_ref, kso_ref, kwo_ref, go_ref):
    avg = _seg_matrix(NSA_KV, NSA_HEAD, 1.0 / NSA_HEAD)

    def head_norm(x, g):
        ms = _dot(x * x, avg, HI)
        return x * lax.rsqrt(ms + RMS_EPS) * g

    scale = NSA_HEAD ** -0.5
    for s in range(NSA_WIDTH // NSA_KV):
        sl = slice(s * NSA_KV, (s + 1) * NSA_KV)
        qo_ref[:, sl] = (head_norm(q_ref[:, sl], qg_ref[...]) * scale).astype(qo_ref.dtype)
    kso_ref[...] = head_norm(ks_ref[...], ksg_ref[...]).astype(kso_ref.dtype)
    kwo_ref[...] = head_norm(kw_ref[...], kwg_ref[...]).astype(kwo_ref.dtype)
    go_ref[...] = jax.nn.sigmoid(gt_ref[...])


def _nsa_prep(u2, q_gain, ks_gain, kw_gain, tm=512):
    n = u2.shape[0]
    kvb = COL_KV // NSA_KV
    gain = pl.BlockSpec((1, NSA_KV), lambda i: (0, 0))
    return pl.pallas_call(
        _nsa_prep_kernel,
        out_shape=(jax.ShapeDtypeStruct((n, NSA_WIDTH), BF16), jax.ShapeDtypeStruct((n, NSA_KV), BF16),
                   jax.ShapeDtypeStruct((n, NSA_KV), BF16), jax.ShapeDtypeStruct((n, LANES), F32)),
        grid=(n // tm,),
        in_specs=[
            pl.BlockSpec((tm, NSA_WIDTH), lambda i: (i, COL_Q // NSA_WIDTH)),
            pl.BlockSpec((tm, NSA_KV), lambda i: (i, kvb + 2)),
            pl.BlockSpec((tm, NSA_KV), lambda i: (i, kvb + 4)),
            pl.BlockSpec((tm, LANES), lambda i: (i, (COL_MISC + MISC_GATE) // LANES)),
            gain, gain, gain,
        ],
        out_specs=(pl.BlockSpec((tm, NSA_WIDTH), lambda i: (i, 0)), pl.BlockSpec((tm, NSA_KV), lambda i: (i, 0)),
                   pl.BlockSpec((tm, NSA_KV), lambda i: (i, 0)), pl.BlockSpec((tm, LANES), lambda i: (i, 0))),
        compiler_params=_cp(("arbitrary",)),
        name="nsa_prep",
    )(u2, u2, u2, u2, q_gain, ks_gain, kw_gain)


def _nsa_cmp_kernel(zk_ref, zv_ref, pk_ref, pv_ref, kw1_ref, kb1_ref, kw2_ref, kb2_ref, vw1_ref, vb1_ref,
                    vw2_ref, vb2_ref, kn_ref, ko_ref, vo_ref):
    nz = zk_ref.shape[2]
    half = kw1_ref.shape[0] // 2

    def compress(z, pos, w1_ref, b1, w2, b2):
        top = _dot(z, w1_ref[0:half, :])
        bot = _dot(z, w1_ref[half:, :])
        c1 = _dot(jnp.broadcast_to(pos, (8, pos.shape[1])).astype(BF16), w1_ref[...])[0:1] + b1
        hid = jax.nn.gelu(top + pltpu.roll(bot, nz - 1, 0) + c1)
        return _dot(hid.astype(BF16), w2) + b2

    kc = compress(zk_ref[0, 0], pk_ref[...], kw1_ref, kb1_ref[...], kw2_ref[...], kb2_ref[...])
    ms = jnp.mean(kc * kc, axis=-1, keepdims=True)
    ko_ref[0, 0] = (kc * lax.rsqrt(ms + RMS_EPS) * kn_ref[...]).astype(ko_ref.dtype)
    vc = compress(zv_ref[0, 0], pv_ref[...], vw1_ref, vb1_ref[...], vw2_ref[...], vb2_ref[...])
    vo_ref[0, 0] = vc.astype(vo_ref.dtype)


def _nsa_cmp(zk, zv, pos_k, pos_v, kw1, kb1, kw2, kb2, vw1, vb1, vw2, vb2, kn0):
    b, g, nz, zw = zk.shape
    full = lambda a: pl.BlockSpec(a.shape, lambda bi, gi: (0,) * a.ndim)
    zspec = pl.BlockSpec((1, 1, nz, zw), lambda bi, gi: (bi, gi, 0, 0))
    ospec = pl.BlockSpec((1, 1, nz, NSA_HEAD), lambda bi, gi: (bi, gi, 0, 0))
    params = (pos_k, pos_v, kw1, kb1, kw2, kb2, vw1, vb1, vw2, vb2, kn0)
    return pl.pallas_call(
        _nsa_cmp_kernel,
        out_shape=(jax.ShapeDtypeStruct((b, g, nz, NSA_HEAD), BF16),) * 2,
        grid=(b, g),
        in_specs=[zspec, zspec] + [full(a) for a in params],
        out_specs=(ospec, ospec),
        compiler_params=_cp(("arbitrary", "arbitrary")),
        name="nsa_compress",
    )(zk, zv, *params)


def _nsa_select_kernel(q_ref, kc_ref, vc_ref, oc_ref, sel_ref, *, n_sel):
    tq = q_ref.shape[4]
    nz = kc_ref.shape[2]
    ns = sel_ref.shape[2]
    t0 = pl.program_id(2) * tq
    n_idx = _iota((nz, tq), 0)
    t_idx = t0 + _iota((nz, tq), 1)
    cmask = (n_idx * CMP_STRIDE + (CMP_LEN - 1) <= t_idx) & (n_idx < nz - 1)
    kc = kc_ref[0, 0]
    vc_t = vc_ref[0, 0]
    psum = jnp.zeros((nz, tq), F32)
    for r in range(NSA_R):
        s = jnp.where(cmask, _dot(kc, q_ref[0, 0, r]), -jnp.inf)
        mx = jnp.max(s, axis=0, keepdims=True)
        e = jnp.exp(s - jnp.where(mx == -jnp.inf, 0.0, mx))
        p = e / jnp.maximum(jnp.sum(e, axis=0, keepdims=True), 1e-30)
        psum = psum + p
        oc_ref[0, 0, r] = _dot(vc_t, p.astype(BF16))
    jb = _iota((ns, nz), 0) * SLC_BLOCK
    cb = _iota((ns, nz), 1) * CMP_STRIDE
    overlap = ((cb < jb + SLC_BLOCK) & (cb + CMP_LEN > jb) & (_iota((ns, nz), 1) < nz - 1)).astype(F32)
    imp = _dot(overlap, psum, HI)
    j_idx = _iota((ns, tq), 0)
    t_q = t0 + _iota((ns, tq), 1)
    qblk = t_q // SLC_BLOCK
    forced = (j_idx == 0) | (j_idx == qblk) | (j_idx == qblk - 1)
    imp = jnp.where(forced, jnp.inf, imp)
    imp = jnp.where(j_idx * SLC_BLOCK <= t_q, imp, -jnp.inf)
    rank = jnp.zeros((ns, tq), F32)
    for i in range(ns):
        row = imp[i:i + 1, :]
        before = (row > imp) | ((row == imp) & (j_idx > i))
        rank = rank + before.astype(F32)
    sel_ref[0, 0] = ((rank < n_sel) & (imp > -jnp.inf)).astype(F32)


def _nsa_select(q_t, kc, vc_t, tq=128):
    b, g, r, dk, t = q_t.shape
    nz = kc.shape[2]
    ns = t // SLC_BLOCK
    qspec = pl.BlockSpec((1, 1, r, dk, tq), lambda bi, gi, i: (bi, gi, 0, 0, i))
    return pl.pallas_call(
        functools.partial(_nsa_select_kernel, n_sel=min(N_SELECT, ns)),
        out_shape=(jax.ShapeDtypeStruct((b, g, r, dk, t), F32), jax.ShapeDtypeStruct((b, g, ns, t), F32)),
        grid=(b, g, t // tq),
        in_specs=[
            qspec,
            pl.BlockSpec((1, 1, nz, dk), lambda bi, gi, i: (bi, gi, 0, 0)),
            pl.BlockSpec((1, 1, dk, nz), lambda bi, gi, i: (bi, gi, 0, 0)),
        ],
        out_specs=(qspec, pl.BlockSpec((1, 1, ns, tq), lambda bi, gi, i: (bi, gi, 0, i))),
        compiler_params=_cp(("arbitrary", "arbitrary", "arbitrary")),
        name="nsa_select",
    )(q_t, kc, vc_t)


def _nsa_attn_kernel(q_ref, ks_ref, vs_ref, kw_ref, vw_ref, sel_ref, oc_ref, gt_ref, o_ref,
                     m_ref, l_ref, acc_ref):
    r, dk, tq = q_ref.shape[2:]
    kt = vs_ref.shape[4]
    kwt = vw_ref.shape[4]
    t0 = pl.program_id(2) * tq
    q_t = jnp.concatenate([q_ref[0, 0, h] for h in range(r)], axis=1)
    tpos = t0 + _iota((1, tq), 1)

    def reset():
        m_ref[...] = jnp.full_like(m_ref, NEG)
        l_ref[...] = jnp.zeros_like(l_ref)
        acc_ref[...] = jnp.zeros_like(acc_ref)

    def online_update(s, ok, v_t):
        ps, alphas = [], []
        for h in range(r):
            s_h = jnp.where(ok, s[:, h * tq:(h + 1) * tq], NEG)
            m_old = m_ref[h]
            m_new = jnp.maximum(m_old, jnp.max(s_h, axis=0, keepdims=True))
            alpha = jnp.exp(m_old - m_new)
            p = jnp.where(ok, jnp.exp(s_h - m_new), 0.0)
            l_ref[h] = alpha * l_ref[h] + jnp.sum(p, axis=0, keepdims=True)
            m_ref[h] = m_new
            ps.append(p.astype(BF16))
            alphas.append(alpha)
        acc_ref[...] = jnp.concatenate(alphas, axis=1) * acc_ref[...] + _dot(v_t, jnp.concatenate(ps, axis=1))

    def result():
        l_all = jnp.concatenate([l_ref[h] for h in range(r)], axis=1)
        return acc_ref[...] / jnp.maximum(l_all, 1e-30)

    reset()
    nb = kt // SLC_BLOCK

    def sel_body(j, carry):
        k = ks_ref[0, 0, pl.ds(pl.multiple_of(j * kt, kt), kt), :]
        s = _dot(k, q_t)
        bm = jnp.concatenate([jnp.broadcast_to(sel_ref[0, 0, pl.ds(j * nb + c, 1), :], (SLC_BLOCK, tq))
                              for c in range(nb)], axis=0)
        kpos = j * kt + _iota((kt, tq), 0)
        online_update(s, (bm > 0.5) & (kpos <= tpos), vs_ref[0, 0, j])
        return carry

    lax.fori_loop(0, (t0 + tq + kt - 1) // kt, sel_body, 0)
    o_s = result()

    reset()

    def win_body(j, carry):
        k = kw_ref[0, 0, pl.ds(pl.multiple_of(j * kwt, kwt), kwt), :]
        s = _dot(k, q_t)
        kpos = j * kwt + _iota((kwt, tq), 0)
        online_update(s, (kpos <= tpos) & (kpos > tpos - WINDOW), vw_ref[0, 0, j])
        return carry

    lo = jnp.maximum(t0 - (WINDOW - 1), 0) // kwt
    lax.fori_loop(lo, (t0 + tq - 1) // kwt + 1, win_body, 0)
    o_w = result()

    for h in range(r):
        sl = slice(h * tq, (h + 1) * tq)
        gt = gt_ref[0, 0, h]
        out = gt[0:1] * oc_ref[0, 0, h] + gt[1:2] * o_s[:, sl] + gt[2:3] * o_w[:, sl]
        o_ref[0, 0, h] = out.astype(o_ref.dtype)


def _nsa_attn(q_t, ks, vs_t, kw, vw_t, sel_t, oc_t, gates_t, tq=128):
    b, g, r, dk, t = q_t.shape
    ns = sel_t.shape[2]
    qspec = pl.BlockSpec((1, 1, r, dk, tq), lambda bi, gi, i: (bi, gi, 0, 0, i))
    kspec = pl.BlockSpec((1, 1, t, dk), lambda bi, gi, i: (bi, gi, 0, 0))
    vspec = lambda v: pl.BlockSpec((1, 1) + v.shape[2:], lambda bi, gi, i: (bi, gi, 0, 0, 0))
    return pl.pallas_call(
        _nsa_attn_kernel,
        out_shape=jax.ShapeDtypeStruct((b, g, r, dk, t), BF16),
        grid=(b, g, t // tq),
        in_specs=[
            qspec, kspec, vspec(vs_t), kspec, vspec(vw_t),
            pl.BlockSpec((1, 1, ns, tq), lambda bi, gi, i: (bi, gi, 0, i)),
            qspec,
            pl.BlockSpec((1, 1, r, 3, tq), lambda bi, gi, i: (bi, gi, 0, 0, i)),
        ],
        out_specs=qspec,
        scratch_shapes=[pltpu.VMEM((r, 1, tq), F32), pltpu.VMEM((r, 1, tq), F32), pltpu.VMEM((dk, r * tq), F32)],
        compiler_params=_cp(("arbitrary", "arbitrary", "arbitrary")),
        name="nsa_attention",
    )(q_t, ks, vs_t, kw, vw_t, sel_t, oc_t, gates_t)


def _nsa(u3, q_norm, k_norm, pos_k, pos_v, ck_w1, ck_b1, ck_w2, ck_b2, cv_w1, cv_b1, cv_w2, cv_b2):
    b, t, _ = u3.shape
    n = b * t
    g, r, dk = NSA_G, NSA_R, NSA_HEAD
    tile_g = lambda p: jnp.tile(p.astype(F32), g).reshape(1, g * dk)
    qn, ksn, kwn, gsig = _nsa_prep(u3.reshape(n, IN_COLS), tile_g(q_norm), tile_g(k_norm[1]), tile_g(k_norm[2]))
    per_group = lambda a: a.reshape(b, t, g, dk).transpose(0, 2, 1, 3)
    kv = lambda idx: u3[..., COL_KV + idx * NSA_KV:COL_KV + (idx + 1) * NSA_KV]
    q_t = qn.reshape(b, t, g, r, dk).transpose(0, 2, 3, 4, 1)
    nz = t // CMP_STRIDE
    zk = per_group(kv(0).astype(BF16)).reshape(b, g, nz, CMP_STRIDE * dk)
    zv = per_group(kv(1).astype(BF16)).reshape(b, g, nz, CMP_STRIDE * dk)
    row = lambda p: p.reshape(1, -1).astype(F32)
    kc, vc = _nsa_cmp(zk, zv, row(pos_k), row(pos_v), ck_w1.astype(BF16), row(ck_b1), ck_w2.astype(BF16), row(ck_b2),
                      cv_w1.astype(BF16), row(cv_b1), cv_w2.astype(BF16), row(cv_b2), row(k_norm[0]))
    oc_t, sel_t = _nsa_select(q_t, kc, vc.transpose(0, 1, 3, 2))
    gates_t = gsig[:, :g * r * 3].reshape(b, t, g, r, 3).transpose(0, 2, 3, 4, 1)

    def tiled_t(a, tile):
        return a.astype(BF16).reshape(b, t // tile, tile, g, dk).transpose(0, 3, 1, 4, 2)

    o_t = _nsa_attn(q_t, per_group(ksn), tiled_t(kv(3), min(NSA_KT, t)), per_group(kwn), tiled_t(kv(5), min(NSA_KWT, t)),
                    sel_t, oc_t, gates_t)
    return o_t.transpose(0, 4, 1, 2, 3).reshape(b, t, NSA_WIDTH)


def _outproj_kernel(x_ref, yr_ref, yn_ref, w_ref, g_ref, rw_ref, rb_ref,
                    h_ref, xn_ref, e_ref, gate_ref, rank_ref, cnt_ref, cnt_acc):
    tm = x_ref.shape[0]

    @pl.when(pl.program_id(0) == 0)
    def _():
        cnt_acc[...] = jnp.zeros_like(cnt_acc)

    h = x_ref[...] + _dot(yr_ref[...], w_ref[0:RW_WIDTH, :]) + _dot(yn_ref[...], w_ref[RW_WIDTH:, :])
    h_ref[...] = h
    ms = jnp.mean(h * h, axis=-1, keepdims=True)
    xn = h * lax.rsqrt(ms + RMS_EPS) * g_ref[...]
    xn_ref[...] = xn
    logits = _dot(xn, rw_ref[...], HI) + rb_ref[...]
    lane = _iota((tm, LANES), 1).astype(F32)
    work = logits
    top_e, top_v = [], []
    for _ in range(TOP_K):
        mx = jnp.max(work, axis=1, keepdims=True)
        idx = jnp.min(jnp.where(work == mx, lane, float(LANES)), axis=1, keepdims=True)
        top_e.append(idx)
        top_v.append(mx)
        work = jnp.where(lane == idx, -jnp.inf, work)
    ex = [jnp.exp(v - top_v[0]) for v in top_v]
    den = ex[0] + ex[1] + ex[2] + ex[3]
    multihot = jnp.zeros((tm, LANES), F32)
    for e in top_e:
        multihot = multihot + (lane == e).astype(F32)
    strict = (_iota((tm, tm), 1) < _iota((tm, tm), 0)).astype(BF16)
    before = cnt_acc[...] + _dot(strict, multihot.astype(BF16))
    e_out = jnp.zeros((tm, LANES), F32)
    g_out = jnp.zeros((tm, LANES), F32)
    r_out = jnp.zeros((tm, LANES), F32)
    for k in range(TOP_K):
        slot = lane == float(k)
        rank_k = jnp.sum(jnp.where(lane == top_e[k], before, 0.0), axis=1, keepdims=True)
        e_out = jnp.where(slot, top_e[k], e_out)
        g_out = jnp.where(slot, ex[k] / den, g_out)
        r_out = jnp.where(slot, rank_k, r_out)
    e_ref[...] = e_out.astype(I32)
    gate_ref[...] = g_out
    rank_ref[...] = r_out.astype(I32)
    cnt_acc[...] = cnt_acc[...] + jnp.sum(multihot, axis=0, keepdims=True)
    cnt_ref[...] = cnt_acc[...].astype(I32)


def _outproj(x2, y_rw, y_nsa, w_out_bf, g, rw_p, rb_p, tm=256):
    n, d = x2.shape
    rowblk = lambda w: pl.BlockSpec((tm, w), lambda i: (i, 0))
    full = lambda a: pl.BlockSpec(a.shape, lambda i: (0,) * a.ndim)
    return pl.pallas_call(
        _outproj_kernel,
        out_shape=(jax.ShapeDtypeStruct((n, d), F32), jax.ShapeDtypeStruct((n, d), F32),
                   jax.ShapeDtypeStruct((n, LANES), I32), jax.ShapeDtypeStruct((n, LANES), F32),
                   jax.ShapeDtypeStruct((n, LANES), I32), jax.ShapeDtypeStruct((1, LANES), I32)),
        grid=(n // tm,),
        in_specs=[rowblk(d), rowblk(RW_WIDTH), rowblk(NSA_WIDTH), full(w_out_bf), full(g), full(rw_p), full(rb_p)],
        out_specs=(rowblk(d), rowblk(d), rowblk(LANES), rowblk(LANES), rowblk(LANES),
                   pl.BlockSpec((1, LANES), lambda i: (0, 0))),
        scratch_shapes=[pltpu.VMEM((1, LANES), F32)],
        compiler_params=_cp(("arbitrary",)),
        name="outproj_router",
    )(x2, y_rw, y_nsa, w_out_bf, g, rw_p, rb_p)


def _gather_kernel(row_tok, n_act, x_hbm, o_ref, sem):
    tg = o_ref.shape[0]
    base = pl.program_id(0) * tg

    def row_copy(r):
        return pltpu.make_async_copy(x_hbm.at[pl.ds(row_tok[base + r], 1), :], o_ref.at[pl.ds(r, 1), :], sem)

    @pl.when(base < n_act[0])
    def _():
        def issue(r, c):
            row_copy(r).start()
            return c

        def drain(r, c):
            row_copy(r).wait()
            return c

        lax.fori_loop(0, tg, issue, 0)
        lax.fori_loop(0, tg, drain, 0)

    @pl.when(base >= n_act[0])
    def _():
        o_ref[...] = jnp.zeros_like(o_ref)


def _gather_rows(row_tok, n_act, xn, n_rows, tg=256):
    d = xn.shape[1]
    return pl.pallas_call(
        _gather_kernel,
        out_shape=jax.ShapeDtypeStruct((n_rows, d), xn.dtype),
        grid_spec=pltpu.PrefetchScalarGridSpec(
            num_scalar_prefetch=2,
            grid=(n_rows // tg,),
            in_specs=[pl.BlockSpec(memory_space=pl.ANY)],
            out_specs=pl.BlockSpec((tg, d), lambda i, rt, na: (i, 0)),
            scratch_shapes=[pltpu.SemaphoreType.DMA],
        ),
        compiler_params=_cp(("arbitrary",)),
        name="moe_gather",
    )(row_tok, n_act, xn)


def _gm1_kernel(e_s, j_s, rt_s, valid_s, first_s, rto_s, jo_s, x_ref, wg_ref, wl_ref, bg_ref, bl_ref, o_ref,
                wgb, wlb):
    s = pl.program_id(0)

    @pl.when(valid_s[s] == 1)
    def _():
        @pl.when(first_s[s] == 1)
        def _():
            wgb[...] = wg_ref[0].astype(BF16)
            wlb[...] = wl_ref[0].astype(BF16)

        x = x_ref[...].astype(BF16)
        gt = jnp.minimum(_dot(x, wgb[...]) + bg_ref[0], SWIGLU_LIMIT)
        lin = jnp.clip(_dot(x, wlb[...]) + bl_ref[0], -SWIGLU_LIMIT, SWIGLU_LIMIT)
        o_ref[...] = (gt * jax.nn.sigmoid(SWIGLU_ALPHA * gt) * (lin + 1.0)).astype(o_ref.dtype)

    @pl.when(valid_s[s] == 0)
    def _():
        o_ref[...] = jnp.zeros_like(o_ref)


def _gm2_kernel(e_s, j_s, rt_s, valid_s, first_s, rto_s, jo_s, x_ref, w_ref, b_ref, o_ref, wb):
    s = pl.program_id(0)

    @pl.when(valid_s[s] == 1)
    def _():
        @pl.when(first_s[s] == 1)
        def _():
            wb[...] = w_ref[0].astype(BF16)

        o_ref[...] = _dot(x_ref[...], wb[...]) + b_ref[0]

    @pl.when(valid_s[s] == 0)
    def _():
        o_ref[...] = jnp.zeros_like(o_ref)


def _schedule(counts, tr, n_col_tiles, max_tiles):
    tiles_e = (counts + tr - 1) // tr
    tile_end = jnp.cumsum(tiles_e)
    tile_start = tile_end - tiles_e
    steps_e = tiles_e * n_col_tiles
    step_end = jnp.cumsum(steps_e)
    total = step_end[-1]
    n_steps = max_tiles * n_col_tiles
    step = jnp.arange(n_steps, dtype=I32)
    s = jnp.minimum(step, total - 1)
    e = jnp.sum((step_end[None, :] <= s[:, None]).astype(I32), axis=1)
    local = s - (step_end[e] - steps_e[e])
    te = jnp.maximum(tiles_e[e], 1)
    j = (local // te).astype(I32)
    rt = (tile_start[e] + local % te).astype(I32)
    valid = step < total
    first = ((local % te) == 0).astype(I32)
    spare = step - total
    rt_out = jnp.where(valid, rt, tile_end[-1] + spare // n_col_tiles).astype(I32)
    j_out = jnp.where(valid, j, spare % n_col_tiles).astype(I32)
    return (e, j, rt, valid.astype(I32), first, rt_out, j_out), tile_start * tr, tile_end[-1] * tr


def _grouped_mlp(x_rows, sched1, sched2, w1, b1, w2, b2, tr, tn1, tn2):
    n_rows, d = x_rows.shape
    de = w2.shape[1]
    nt1 = de // tn1
    b1r = b1.reshape(N_EXPERTS, 1, 2 * de)
    b2r = b2.reshape(N_EXPERTS, 1, d)
    act = pl.pallas_call(
        _gm1_kernel,
        out_shape=jax.ShapeDtypeStruct((n_rows, de), BF16),
        grid_spec=pltpu.PrefetchScalarGridSpec(
            num_scalar_prefetch=7,
            grid=(sched1[0].shape[0],),
            in_specs=[
                pl.BlockSpec((tr, d), lambda s, e, j, rt, v, f, ro, jo: (rt[s], 0)),
                pl.BlockSpec((1, d, tn1), lambda s, e, j, rt, v, f, ro, jo: (e[s], 0, j[s])),
                pl.BlockSpec((1, d, tn1), lambda s, e, j, rt, v, f, ro, jo: (e[s], 0, nt1 + j[s])),
                pl.BlockSpec((1, 1, tn1), lambda s, e, j, rt, v, f, ro, jo: (e[s], 0, j[s])),
                pl.BlockSpec((1, 1, tn1), lambda s, e, j, rt, v, f, ro, jo: (e[s], 0, nt1 + j[s])),
            ],
            out_specs=pl.BlockSpec((tr, tn1), lambda s, e, j, rt, v, f, ro, jo: (ro[s], jo[s])),
            scratch_shapes=[pltpu.VMEM((d, tn1), BF16), pltpu.VMEM((d, tn1), BF16)],
        ),
        compiler_params=_cp(("arbitrary",)),
        name="moe_up",
    )(*sched1, x_rows, w1, w1, b1r, b1r)
    return pl.pallas_call(
        _gm2_kernel,
        out_shape=jax.ShapeDtypeStruct((n_rows, d), F32),
        grid_spec=pltpu.PrefetchScalarGridSpec(
            num_scalar_prefetch=7,
            grid=(sched2[0].shape[0],),
            in_specs=[
                pl.BlockSpec((tr, de), lambda s, e, j, rt, v, f, ro, jo: (rt[s], 0)),
                pl.BlockSpec((1, de, tn2), lambda s, e, j, rt, v, f, ro, jo: (e[s], 0, j[s])),
                pl.BlockSpec((1, 1, tn2), lambda s, e, j, rt, v, f, ro, jo: (e[s], 0, j[s])),
            ],
            out_specs=pl.BlockSpec((tr, tn2), lambda s, e, j, rt, v, f, ro, jo: (ro[s], jo[s])),
            scratch_shapes=[pltpu.VMEM((de, tn2), BF16)],
        ),
        compiler_params=_cp(("arbitrary",)),
        name="moe_down",
    )(*sched2, act, w2, b2r)


def _ple_kernel(dest, h_ref, gate_ref, p_ref, y_hbm, g_ref, gw_ref, pw_ref, o_ref, buf, sem):
    tm = h_ref.shape[0]
    base = pl.program_id(0) * tm

    def row_copy(t, k):
        return pltpu.make_async_copy(y_hbm.at[pl.ds(dest[(base + t) * TOP_K + k], 1), :],
                                     buf.at[k, pl.ds(t, 1), :], sem)

    def issue(t, c):
        for k in range(TOP_K):
            row_copy(t, k).start()
        return c

    def drain(t, c):
        for k in range(TOP_K):
            row_copy(t, k).wait()
        return c

    lax.fori_loop(0, tm, issue, 0)
    lax.fori_loop(0, tm, drain, 0)
    gate = gate_ref[...]
    h = h_ref[...]
    for k in range(TOP_K):
        h = h + gate[:, k:k + 1] * buf[k]
    ms = jnp.mean(h * h, axis=-1, keepdims=True)
    hn = (h * lax.rsqrt(ms + RMS_EPS) * g_ref[...]).astype(BF16)
    pgate = jax.nn.sigmoid(_dot(hn, gw_ref[...]))
    o_ref[...] = h + pgate * _dot(p_ref[...].astype(BF16), pw_ref[...])


def _combine_ple(dest, h1, gate, p2, y_rows, g, gw_bf, pw_bf, tm=128):
    n, d = h1.shape
    rowblk = lambda w: pl.BlockSpec((tm, w), lambda i, ds: (i, 0))
    full = lambda a: pl.BlockSpec(a.shape, lambda i, ds: (0,) * a.ndim)
    return pl.pallas_call(
        _ple_kernel,
        out_shape=jax.ShapeDtypeStruct((n, d), F32),
        grid_spec=pltpu.PrefetchScalarGridSpec(
            num_scalar_prefetch=1,
            grid=(n // tm,),
            in_specs=[rowblk(d), rowblk(LANES), rowblk(p2.shape[1]), pl.BlockSpec(memory_space=pl.ANY),
                      full(g), full(gw_bf), full(pw_bf)],
            out_specs=rowblk(d),
            scratch_shapes=[pltpu.VMEM((TOP_K, tm, d), F32), pltpu.SemaphoreType.DMA],
        ),
        compiler_params=_cp(("arbitrary",)),
        name="combine_ple",
    )(dest, h1, gate, p2, y_rows, g, gw_bf, pw_bf)


MOE_TR = 512
MOE_TN1 = 512
MOE_TN2 = 1024


def _layer(x, p, mix_norm_g, w_in, rw, nsa, w_out, moe_norm_g, router_w, router_b, moe_w1, moe_b1, moe_w2, moe_b2,
           ple_norm_g, ple_w, ple_gate_w):
    b, t, d = x.shape
    n = b * t
    x2 = x.reshape(n, d)
    row = lambda v: v.reshape(1, -1).astype(F32)
    n_lora = sum(RW_LORA)
    rw_cols = 3 * RW_WIDTH + n_lora
    kv0 = rw_cols + NSA_WIDTH
    n_gate = 3 * NSA_G * NSA_R
    zeros = lambda w: jnp.zeros((d, w), w_in.dtype)
    w_p = jnp.concatenate([
        w_in[:, :3 * RW_WIDTH], w_in[:, rw_cols:kv0], w_in[:, kv0:kv0 + 6 * NSA_KV],
        w_in[:, 3 * RW_WIDTH:rw_cols], zeros(MISC_GATE - n_lora), w_in[:, kv0 + 6 * NSA_KV:],
        zeros(MISC_W - MISC_GATE - n_gate)], axis=1).astype(BF16)
    u = _inproj(x2, row(mix_norm_g), w_p)
    u3 = u.reshape(b, t, IN_COLS)
    y_rw = _rwkv(u3, *_rwkv_params(*rw))
    y_nsa = _nsa(u3, *nsa)
    out = _tail(x2, y_rw.reshape(n, RW_WIDTH), y_nsa.reshape(n, NSA_WIDTH), p.reshape(n, PLE_DIM), w_out, moe_norm_g,
                router_w, router_b, moe_w1, moe_b1, moe_w2, moe_b2, ple_norm_g, ple_w, ple_gate_w)
    return out.reshape(b, t, d)


def _tail(x2, y_rw, y_nsa, p2, w_out, moe_norm_g, router_w, router_b, moe_w1, moe_b1, moe_w2, moe_b2,
          ple_norm_g, ple_w, ple_gate_w):
    n, d = x2.shape
    row = lambda v: v.reshape(1, -1).astype(F32)
    rw_p = jnp.zeros((d, LANES), F32).at[:, :N_EXPERTS].set(router_w)
    rb_p = jnp.full((1, LANES), NEG, F32).at[0, :N_EXPERTS].set(router_b)
    h1, xn, top_e, gate, rank, counts = _outproj(x2, y_rw, y_nsa, w_out.astype(BF16), row(moe_norm_g), rw_p, rb_p)
    counts = counts[0, :N_EXPERTS]
    max_tiles = (n * TOP_K) // MOE_TR + N_EXPERTS
    n_rows = max_tiles * MOE_TR
    sched1, pstart, n_act = _schedule(counts, MOE_TR, D_EXPERT // MOE_TN1, max_tiles)
    sched2, _, _ = _schedule(counts, MOE_TR, d // MOE_TN2, max_tiles)
    top_e = top_e[:, :TOP_K]
    dest = (pstart[top_e] + rank[:, :TOP_K]).astype(I32)
    tok = jnp.broadcast_to(jnp.arange(n, dtype=I32)[:, None], (n, TOP_K))
    row_tok = jnp.zeros((n_rows,), I32).at[dest.reshape(-1)].set(tok.reshape(-1))
    x_rows = _gather_rows(row_tok, n_act.reshape(1).astype(I32), xn, n_rows)
    y_rows = _grouped_mlp(x_rows, sched1, sched2, moe_w1, moe_b1, moe_w2, moe_b2, MOE_TR, MOE_TN1, MOE_TN2)
    return _combine_ple(dest.reshape(-1), h1, gate, p2, y_rows, row(ple_norm_g),
                        ple_gate_w.astype(BF16), ple_w.astype(BF16))


def kernel(x, p, mix_norm_g, w_in, rw_mu, rw_w0, rw_w2, rw_a0, rw_a2, rw_g2, rw_k_k, rw_k_a, rw_r_k, rw_lnx_w,
           rw_lnx_b, nsa_q_norm, nsa_k_norm, cmp_pos_k, cmp_pos_v, cmp_k_w1, cmp_k_b1, cmp_k_w2, cmp_k_b2, cmp_v_w1,
           cmp_v_b1, cmp_v_w2, cmp_v_b2, w_out, moe_norm_g, router_w, router_b, moe_w1, moe_b1, moe_w2, moe_b2,
           ple_norm_g, ple_w, ple_gate_w):
    h = x
    for i in range(mix_norm_g.shape[0]):
        rw = (rw_mu[i], rw_w0[i], rw_w2[i], rw_a0[i], rw_a2[i], rw_g2[i], rw_k_k[i], rw_k_a[i], rw_r_k[i],
              rw_lnx_w[i], rw_lnx_b[i])
        nsa = (nsa_q_norm[i], nsa_k_norm[i], cmp_pos_k[i], cmp_pos_v[i], cmp_k_w1[i], cmp_k_b1[i], cmp_k_w2[i],
               cmp_k_b2[i], cmp_v_w1[i], cmp_v_b1[i], cmp_v_w2[i], cmp_v_b2[i])
        h = _layer(h, p[i], mix_norm_g[i], w_in[i], rw, nsa, w_out[i], moe_norm_g[i], router_w[i], router_b[i],
                   moe_w1[i], moe_b1[i], moe_w2[i], moe_b2[i], ple_norm_g[i], ple_w[i], ple_gate_w[i])
    return h
```

```python
import functools

import jax
import jax.numpy as jnp
from jax import lax
from jax.experimental import pallas as pl
from jax.experimental.pallas import tpu as pltpu

F32 = jnp.float32
BF16 = jnp.bfloat16
I32 = jnp.int32
HI = lax.Precision.HIGHEST

LANES = 128
VMEM_LIMIT = 56 * 1024 * 1024

D_MODEL = 2048
RMS_EPS = 1e-6
RW_WIDTH = 1024
RW_HEAD = 64
RW_LORA = (64, 64, 160)
RW_GN_EPS = 64e-5
NSA_WIDTH = 1024
NSA_HEAD = 64
NSA_G = 4
NSA_R = 4
NSA_KV = NSA_G * NSA_HEAD
CMP_LEN = 32
CMP_STRIDE = 16
CMP_HIDDEN = 256
SLC_BLOCK = 64
N_SELECT = 16
WINDOW = 512
N_EXPERTS = 32
TOP_K = 4
D_EXPERT = 2048
SWIGLU_LIMIT = 7.0
SWIGLU_ALPHA = 1.702
PLE_DIM = 256

COL_RKV = 0
COL_Q = 3072
COL_KV = 4096
COL_MISC = 5632
MISC_W = 512
MISC_GATE = 384
IN_COLS = 6144

RW_HG = 4
RW_LW = RW_HG * RW_HEAD
RW_C = 64
NSA_TQ = 256
NSA_KT = 512
NSA_KWT = 256
NEG = -1e30


def _cp(sem, vmem=VMEM_LIMIT):
    return pltpu.CompilerParams(dimension_semantics=sem, vmem_limit_bytes=vmem)


def _dot(a, b, prec=None):
    return jnp.dot(a, b, preferred_element_type=F32, precision=prec)


def _dot_nt(a, b, prec=None):
    return lax.dot_general(a, b, (((1,), (1,)), ((), ())), preferred_element_type=F32, precision=prec)


def _dot_split(x, m, pieces=2, left=False):
    out, rem = None, x
    for _ in range(pieces):
        part = rem.astype(BF16)
        rem = rem - part.astype(F32)
        term = _dot(m, part) if left else _dot(part, m)
        out = term if out is None else out + term
    return out


def _iota(shape, dim):
    return lax.broadcasted_iota(I32, shape, dim)


def _seg_matrix(n, seg, scale):
    same = (_iota((n, n), 0) // seg) == (_iota((n, n), 1) // seg)
    return jnp.where(same, scale, 0.0).astype(BF16)


def _inproj_kernel(x_ref, g_ref, w_ref, o_ref, xn_ref):
    @pl.when(pl.program_id(1) == 0)
    def _():
        x = x_ref[...]
        ms = jnp.mean(x * x, axis=-1, keepdims=True)
        xn_ref[...] = (x * lax.rsqrt(ms + RMS_EPS) * g_ref[...]).astype(BF16)

    o_ref[...] = _dot(xn_ref[...], w_ref[...])


def _inproj(x2, g, w_bf, tm=512, tn=512):
    n, d = x2.shape
    nc = w_bf.shape[1]
    return pl.pallas_call(
        _inproj_kernel,
        out_shape=jax.ShapeDtypeStruct((n, nc), F32),
        grid=(n // tm, nc // tn),
        in_specs=[
            pl.BlockSpec((tm, d), lambda i, j: (i, 0)),
            pl.BlockSpec((1, d), lambda i, j: (0, 0)),
            pl.BlockSpec((d, tn), lambda i, j: (0, j)),
        ],
        out_specs=pl.BlockSpec((tm, tn), lambda i, j: (i, j)),
        scratch_shapes=[pltpu.VMEM((tm, d), BF16)],
        compiler_params=_cp(("arbitrary", "arbitrary")),
        name="inproj",
    )(x2, g, w_bf)


def _softplus(z):
    return jnp.maximum(z, 0.0) + jnp.log(1.0 + jnp.exp(-jnp.abs(z)))


def _rwkv_kernel(r_ref, k_ref, v_ref, m_ref, mur_ref, muk_ref, muv_ref, mum_ref, w2_ref, a2_ref, g2_ref,
                 w0_ref, a0_ref, kk_ref, ka_ref, rk_ref, lw_ref, lb_ref, o_ref,
                 st_ref, pr_ref, pk_ref, pv_ref, pm_ref, *, nchunk):
    tt = nchunk * RW_C
    lw = RW_LW

    @pl.when(pl.program_id(2) == 0)
    def _():
        st_ref[...] = jnp.zeros_like(st_ref)
        pr_ref[...] = jnp.zeros_like(pr_ref)
        pk_ref[...] = jnp.zeros_like(pk_ref)
        pv_ref[...] = jnp.zeros_like(pv_ref)
        pm_ref[...] = jnp.zeros_like(pm_ref)

    def shift_lerp(x, prev_ref, mu):
        rolled = pltpu.roll(x, 1, 0)
        first = _iota(x.shape, 0) == 0
        sh = jnp.where(first, prev_ref[0:1, :], rolled)
        prev_ref[0:1, :] = x[tt - 1:tt, :]
        return x + (sh - x) * mu

    r = shift_lerp(r_ref[0], pr_ref, mur_ref[...])
    k = shift_lerp(k_ref[0], pk_ref, muk_ref[...])
    v = shift_lerp(v_ref[0], pv_ref, muv_ref[...])
    m = shift_lerp(m_ref[0], pm_ref, mum_ref[...])

    lw_dec = _dot(jnp.tanh(m).astype(BF16), w2_ref[...])
    lw_a = _dot(m.astype(BF16), a2_ref[...])
    g = _dot(jax.nn.sigmoid(m).astype(BF16), g2_ref[...])
    w_log = -_softplus(-(w0_ref[...] + lw_dec)) - 0.5
    ld = -jnp.exp(w_log)
    a = jax.nn.sigmoid(a0_ref[...] + lw_a)

    ones_seg = _seg_matrix(lw, RW_HEAD, 1.0)
    kk = k * kk_ref[...]
    kk = kk / jnp.maximum(jnp.sqrt(_dot_split(kk * kk, ones_seg)), 1e-12)
    k2 = k * (1.0 + (a - 1.0) * ka_ref[...])
    av = -kk
    bv = kk * a

    ri = _iota((lw, lw), 0)
    ci = _iota((lw, lw), 1)
    same_blk = (ri // RW_C) == (ci // RW_C)
    m_strict = same_blk & (ci < ri)
    m_lower = same_blk & (ci <= ri)
    eye = (ri == ci).astype(F32)
    tri = (_iota((RW_C, RW_C), 1) <= _iota((RW_C, RW_C), 0)).astype(BF16)
    lane_head = _iota((1, lw), 1) // RW_HEAD
    hmask = [(lane_head == h).astype(F32) for h in range(RW_HG)]

    def stack_heads(x):
        return jnp.concatenate([x * hmask[h] for h in range(RW_HG)], axis=0)

    def unstack_heads(xs):
        out = xs[0:RW_C] * hmask[0]
        for h in range(1, RW_HG):
            out = out + xs[h * RW_C:(h + 1) * RW_C] * hmask[h]
        return out

    ys = []
    st = st_ref[...]
    for c in range(nchunk):
        sl = slice(c * RW_C, (c + 1) * RW_C)
        ld_c = ld[sl]
        cs = _dot_split(ld_c, tri, pieces=3, left=True)
        w_in = jnp.exp(cs)
        w_inv = jnp.exp(-cs)
        w_prev = jnp.exp(cs - ld_c)
        at = av[sl] * w_prev
        bt = bv[sl] * w_inv
        kt = k2[sl] * w_inv
        rt = r[sl] * w_in
        v_c = v[sl]
        at_s = stack_heads(at).astype(BF16)
        rt_s = stack_heads(rt).astype(BF16)
        bt4 = jnp.concatenate([bt] * RW_HG, axis=0).astype(BF16)
        kt4 = jnp.concatenate([kt] * RW_HG, axis=0).astype(BF16)
        v4 = jnp.concatenate([v_c] * RW_HG, axis=0).astype(BF16)
        a_ab = jnp.where(m_strict, _dot_nt(at_s, bt4), 0.0)
        a_ak = jnp.where(m_strict, _dot_nt(at_s, kt4), 0.0)
        a_rb = jnp.where(m_lower, _dot_nt(rt_s, bt4), 0.0)
        a_rk = jnp.where(m_lower, _dot_nt(rt_s, kt4), 0.0)
        pw = a_ab
        inv = eye + a_ab
        for _ in range(RW_C.bit_length() - 2):
            pw_b = pw.astype(BF16)
            pw = _dot(pw_b, pw_b)
            inv = inv + _dot(inv.astype(BF16), pw.astype(BF16))
        st_b = st.astype(BF16)
        x0 = _dot_nt(at_s, st_b) + _dot(a_ak.astype(BF16), v4)
        u_s = _dot(inv.astype(BF16), x0.astype(BF16))
        y_s = _dot_nt(rt_s, st_b) + _dot(a_rb.astype(BF16), u_s.astype(BF16)) + _dot(a_rk.astype(BF16), v4)
        ys.append(unstack_heads(y_s))
        u_c = unstack_heads(u_s)
        w_last = w_in[RW_C - 1:RW_C, :]
        lhs = jnp.concatenate([u_c.T, v_c.T], axis=1).astype(BF16)
        rhs = jnp.concatenate([bt * w_last, kt * w_last], axis=0).astype(BF16)
        st = st * w_last + jnp.where(same_blk, _dot(lhs, rhs), 0.0)
    st_ref[...] = st
    y = jnp.concatenate(ys, axis=0) if nchunk > 1 else ys[0]

    avg_seg = _seg_matrix(lw, RW_HEAD, 1.0 / RW_HEAD)
    mean = _dot_split(y, avg_seg)
    yc = y - mean
    var = _dot_split(yc * yc, avg_seg)
    yn = yc * lax.rsqrt(var + RW_GN_EPS) * lw_ref[...] + lb_ref[...]
    bonus = _dot_split(r * k2 * rk_ref[...], ones_seg) * v
    o_ref[0] = ((yn + bonus) * g).astype(o_ref.dtype)


def _rwkv(u3, mu_p, w2p, a2p, g2p, w0, a0, k_k, k_a, r_k, lnx_w, lnx_b, nchunk=4):
    b, t, _ = u3.shape
    tt = nchunk * RW_C
    lw = RW_LW
    nhg = RW_WIDTH // lw
    col = lambda base: (lambda bi, h, ti: (bi, ti, base + h))
    vec = lambda base: (lambda bi, h, ti: (0, base + h))
    par = pl.BlockSpec((1, lw), vec(0))
    return pl.pallas_call(
        functools.partial(_rwkv_kernel, nchunk=nchunk),
        out_shape=jax.ShapeDtypeStruct((b, t, RW_WIDTH), BF16),
        grid=(b, nhg, t // tt),
        in_specs=[
            pl.BlockSpec((1, tt, lw), col(0)),
            pl.BlockSpec((1, tt, lw), col(nhg)),
            pl.BlockSpec((1, tt, lw), col(2 * nhg)),
            pl.BlockSpec((1, tt, MISC_W), lambda bi, h, ti: (bi, ti, COL_MISC // MISC_W)),
            pl.BlockSpec((1, lw), vec(0)),
            pl.BlockSpec((1, lw), vec(nhg)),
            pl.BlockSpec((1, lw), vec(2 * nhg)),
            pl.BlockSpec((1, MISC_W), lambda bi, h, ti: (0, COL_MISC // MISC_W)),
            pl.BlockSpec((MISC_W, lw), lambda bi, h, ti: (0, h)),
            pl.BlockSpec((MISC_W, lw), lambda bi, h, ti: (0, h)),
            pl.BlockSpec((MISC_W, lw), lambda bi, h, ti: (0, h)),
            par, par, par, par, par, par, par,
        ],
        out_specs=pl.BlockSpec((1, tt, lw), lambda bi, h, ti: (bi, ti, h)),
        scratch_shapes=[
            pltpu.VMEM((lw, lw), F32),
            pltpu.VMEM((8, lw), F32),
            pltpu.VMEM((8, lw), F32),
            pltpu.VMEM((8, lw), F32),
            pltpu.VMEM((8, MISC_W), F32),
        ],
        compiler_params=_cp(("arbitrary", "arbitrary", "arbitrary")),
        name="rwkv7",
    )(u3, u3, u3, u3, mu_p, mu_p, mu_p, mu_p, w2p, a2p, g2p, w0, a0, k_k, k_a, r_k, lnx_w, lnx_b)


def _pad_rows(w, rows, offset):
    return jnp.zeros((rows, w.shape[1]), w.dtype).at[offset:offset + w.shape[0]].set(w)


def _rwkv_params(mu, w0, w2, a0, a2, g2, k_k, k_a, r_k, lnx_w, lnx_b):
    mu_p = jnp.zeros((1, IN_COLS), F32)
    mu_p = mu_p.at[0, COL_RKV:COL_RKV + 3 * RW_WIDTH].set(mu[:3 * RW_WIDTH])
    mu_p = mu_p.at[0, COL_MISC:COL_MISC + sum(RW_LORA)].set(mu[3 * RW_WIDTH:])
    o1, o2 = RW_LORA[0], RW_LORA[0] + RW_LORA[1]
    row = lambda p: p.reshape(1, RW_WIDTH).astype(F32)
    return (mu_p, _pad_rows(w2, MISC_W, 0).astype(BF16), _pad_rows(a2, MISC_W, o1).astype(BF16),
            _pad_rows(g2, MISC_W, o2).astype(BF16), row(w0), row(a0), row(k_k), row(k_a), row(r_k),
            row(lnx_w), row(lnx_b))


def _nsa_prep_kernel(q_ref, ks_ref, kw_ref, gt_ref, qg_ref, ksg_ref, kwg_ref, qo_ref, kso_ref, kwo_ref, go_ref):
    avg = _seg_matrix(NSA_KV, NSA_HEAD, 1.0 / NSA_HEAD)

    def head_norm(x, g):
        ms = _dot_split(x * x, avg)
        return x * lax.rsqrt(ms + RMS_EPS) * g

    scale = NSA_HEAD ** -0.5
    for s in range(NSA_WIDTH // NSA_KV):
        sl = slice(s * NSA_KV, (s + 1) * NSA_KV)
        qo_ref[:, sl] = (head_norm(q_ref[:, sl], qg_ref[...]) * scale).astype(qo_ref.dtype)
    kso_ref[...] = head_norm(ks_ref[...], ksg_ref[...]).astype(kso_ref.dtype)
    kwo_ref[...] = head_norm(kw_ref[...], kwg_ref[...]).astype(kwo_ref.dtype)
    go_ref[...] = jax.nn.sigmoid(gt_ref[...])


def _nsa_prep(u2, q_gain, ks_gain, kw_gain, tm=512):
    n = u2.shape[0]
    kvb = COL_KV // NSA_KV
    gain = pl.BlockSpec((1, NSA_KV), lambda i: (0, 0))
    return pl.pallas_call(
        _nsa_prep_kernel,
        out_shape=(jax.ShapeDtypeStruct((n, NSA_WIDTH), BF16), jax.ShapeDtypeStruct((n, NSA_KV), BF16),
                   jax.ShapeDtypeStruct((n, NSA_KV), BF16), jax.ShapeDtypeStruct((n, LANES), F32)),
        grid=(n // tm,),
        in_specs=[
            pl.BlockSpec((tm, NSA_WIDTH), lambda i: (i, COL_Q // NSA_WIDTH)),
            pl.BlockSpec((tm, NSA_KV), lambda i: (i, kvb + 2)),
            pl.BlockSpec((tm, NSA_KV), lambda i: (i, kvb + 4)),
            pl.BlockSpec((tm, LANES), lambda i: (i, (COL_MISC + MISC_GATE) // LANES)),
            gain, gain, gain,
        ],
        out_specs=(pl.BlockSpec((tm, NSA_WIDTH), lambda i: (i, 0)), pl.BlockSpec((tm, NSA_KV), lambda i: (i, 0)),
                   pl.BlockSpec((tm, NSA_KV), lambda i: (i, 0)), pl.BlockSpec((tm, LANES), lambda i: (i, 0))),
        compiler_params=_cp(("arbitrary",)),
        name="nsa_prep",
    )(u2, u2, u2, u2, q_gain, ks_gain, kw_gain)


def _nsa_cmp_kernel(zk_ref, zv_ref, pk_ref, pv_ref, kw1_ref, kb1_ref, kw2_ref, kb2_ref, vw1_ref, vb1_ref,
                    vw2_ref, vb2_ref, kn_ref, ko_ref, vo_ref):
    nz = zk_ref.shape[2]
    half = kw1_ref.shape[0] // 2

    def compress(z, pos, w1_ref, b1, w2, b2):
        top = _dot(z, w1_ref[0:half, :])
        bot = _dot(z, w1_ref[half:, :])
        c1 = _dot(jnp.broadcast_to(pos, (8, pos.shape[1])).astype(BF16), w1_ref[...])[0:1] + b1
        hid = jax.nn.gelu(top + pltpu.roll(bot, nz - 1, 0) + c1)
        return _dot(hid.astype(BF16), w2) + b2

    kc = compress(zk_ref[0, 0], pk_ref[...], kw1_ref, kb1_ref[...], kw2_ref[...], kb2_ref[...])
    ms = jnp.mean(kc * kc, axis=-1, keepdims=True)
    ko_ref[0, 0] = (kc * lax.rsqrt(ms + RMS_EPS) * kn_ref[...]).astype(ko_ref.dtype)
    vc = compress(zv_ref[0, 0], pv_ref[...], vw1_ref, vb1_ref[...], vw2_ref[...], vb2_ref[...])
    vo_ref[0, 0] = vc.astype(vo_ref.dtype)


def _nsa_cmp(zk, zv, pos_k, pos_v, kw1, kb1, kw2, kb2, vw1, vb1, vw2, vb2, kn0):
    b, g, nz, zw = zk.shape
    full = lambda a: pl.BlockSpec(a.shape, lambda bi, gi: (0,) * a.ndim)
    zspec = pl.BlockSpec((1, 1, nz, zw), lambda bi, gi: (bi, gi, 0, 0))
    ospec = pl.BlockSpec((1, 1, nz, NSA_HEAD), lambda bi, gi: (bi, gi, 0, 0))
    params = (pos_k, pos_v, kw1, kb1, kw2, kb2, vw1, vb1, vw2, vb2, kn0)
    return pl.pallas_call(
        _nsa_cmp_kernel,
        out_shape=(jax.ShapeDtypeStruct((b, g, nz, NSA_HEAD), BF16),) * 2,
        grid=(b, g),
        in_specs=[zspec, zspec] + [full(a) for a in params],
        out_specs=(ospec, ospec),
        compiler_params=_cp(("arbitrary", "arbitrary")),
        name="nsa_compress",
    )(zk, zv, *params)


def _nsa_select_kernel(q_ref, kc_ref, vc_ref, oc_ref, sel_ref, *, n_sel):
    tq = q_ref.shape[4]
    nz = kc_ref.shape[2]
    ns = sel_ref.shape[2]
    t0 = pl.program_id(2) * tq
    n_idx = _iota((nz, tq), 0)
    t_idx = t0 + _iota((nz, tq), 1)
    cmask = (n_idx * CMP_STRIDE + (CMP_LEN - 1) <= t_idx) & (n_idx < nz - 1)
    kc = kc_ref[0, 0]
    vc_t = vc_ref[0, 0]
    psum = jnp.zeros((nz, tq), F32)
    for r in range(NSA_R):
        s = jnp.where(cmask, _dot(kc, q_ref[0, 0, r]), -jnp.inf)
        mx = jnp.max(s, axis=0, keepdims=True)
        e = jnp.exp(s - jnp.where(mx == -jnp.inf, 0.0, mx))
        p = e / jnp.maximum(jnp.sum(e, axis=0, keepdims=True), 1e-30)
        psum = psum + p
        oc_ref[0, 0, r] = _dot(vc_t, p.astype(BF16))
    jb = _iota((ns, nz), 0) * SLC_BLOCK
    cb = _iota((ns, nz), 1) * CMP_STRIDE
    overlap = ((cb < jb + SLC_BLOCK) & (cb + CMP_LEN > jb) & (_iota((ns, nz), 1) < nz - 1)).astype(BF16)
    imp = _dot_split(psum, overlap, pieces=3, left=True)
    j_idx = _iota((ns, tq), 0)
    t_q = t0 + _iota((ns, tq), 1)
    qblk = t_q // SLC_BLOCK
    forced = (j_idx == 0) | (j_idx == qblk) | (j_idx == qblk - 1)
    imp = jnp.where(forced, jnp.inf, imp)
    imp = jnp.where(j_idx * SLC_BLOCK <= t_q, imp, -jnp.inf)
    rank = jnp.zeros((ns, tq), F32)
    for i in range(ns):
        row = imp[i:i + 1, :]
        before = (row > imp) | ((row == imp) & (j_idx > i))
        rank = rank + before.astype(F32)
    sel_ref[0, 0] = ((rank < n_sel) & (imp > -jnp.inf)).astype(F32)


def _nsa_select(q_t, kc, vc_t, tq=NSA_TQ):
    b, g, r, dk, t = q_t.shape
    nz = kc.shape[2]
    ns = t // SLC_BLOCK
    qspec = pl.BlockSpec((1, 1, r, dk, tq), lambda bi, gi, i: (bi, gi, 0, 0, i))
    return pl.pallas_call(
        functools.partial(_nsa_select_kernel, n_sel=min(N_SELECT, ns)),
        out_shape=(jax.ShapeDtypeStruct((b, g, r, dk, t), F32), jax.ShapeDtypeStruct((b, g, ns, t), F32)),
        grid=(b, g, t // tq),
        in_specs=[
            qspec,
            pl.BlockSpec((1, 1, nz, dk), lambda bi, gi, i: (bi, gi, 0, 0)),
            pl.BlockSpec((1, 1, dk, nz), lambda bi, gi, i: (bi, gi, 0, 0)),
        ],
        out_specs=(qspec, pl.BlockSpec((1, 1, ns, tq), lambda bi, gi, i: (bi, gi, 0, i))),
        compiler_params=_cp(("arbitrary", "arbitrary", "arbitrary")),
        name="nsa_select",
    )(q_t, kc, vc_t)


def _nsa_attn_kernel(q_ref, ks_ref, vs_ref, kw_ref, vw_ref, sel_ref, oc_ref, gt_ref, o_ref,
                     m_ref, l_ref, acc_ref):
    r, dk, tq = q_ref.shape[2:]
    kt = vs_ref.shape[4]
    kwt = vw_ref.shape[4]
    t0 = pl.program_id(2) * tq
    q_t = jnp.concatenate([q_ref[0, 0, h] for h in range(r)], axis=1)
    tpos = t0 + _iota((1, tq), 1)

    def reset():
        m_ref[...] = jnp.full_like(m_ref, NEG)
        l_ref[...] = jnp.zeros_like(l_ref)
        acc_ref[...] = jnp.zeros_like(acc_ref)

    def online_update(s, ok, v_t):
        ps, alphas = [], []
        for h in range(r):
            s_h = jnp.where(ok, s[:, h * tq:(h + 1) * tq], NEG)
            m_old = m_ref[h]
            m_new = jnp.maximum(m_old, jnp.max(s_h, axis=0, keepdims=True))
            alpha = jnp.exp(m_old - m_new)
            p = jnp.where(ok, jnp.exp(s_h - m_new), 0.0)
            l_ref[h] = alpha * l_ref[h] + jnp.sum(p, axis=0, keepdims=True)
            m_ref[h] = m_new
            ps.append(p.astype(BF16))
            alphas.append(alpha)
        acc_ref[...] = jnp.concatenate(alphas, axis=1) * acc_ref[...] + _dot(v_t, jnp.concatenate(ps, axis=1))

    def result():
        l_all = jnp.concatenate([l_ref[h] for h in range(r)], axis=1)
        return acc_ref[...] / jnp.maximum(l_all, 1e-30)

    reset()
    nb = kt // SLC_BLOCK

    def sel_body(j, carry):
        k = ks_ref[0, 0, pl.ds(pl.multiple_of(j * kt, kt), kt), :]
        s = _dot(k, q_t)
        bm = jnp.concatenate([jnp.broadcast_to(sel_ref[0, 0, pl.ds(j * nb + c, 1), :], (SLC_BLOCK, tq))
                              for c in range(nb)], axis=0)
        kpos = j * kt + _iota((kt, tq), 0)
        online_update(s, (bm > 0.5) & (kpos <= tpos), vs_ref[0, 0, j])
        return carry

    lax.fori_loop(0, (t0 + tq + kt - 1) // kt, sel_body, 0)
    o_s = result()

    reset()

    def win_body(j, carry):
        k = kw_ref[0, 0, pl.ds(pl.multiple_of(j * kwt, kwt), kwt), :]
        s = _dot(k, q_t)
        kpos = j * kwt + _iota((kwt, tq), 0)
        online_update(s, (kpos <= tpos) & (kpos > tpos - WINDOW), vw_ref[0, 0, j])
        return carry

    lo = jnp.maximum(t0 - (WINDOW - 1), 0) // kwt
    lax.fori_loop(lo, (t0 + tq - 1) // kwt + 1, win_body, 0)
    o_w = result()

    for h in range(r):
        sl = slice(h * tq, (h + 1) * tq)
        gt = gt_ref[0, 0, h]
        out = gt[0:1] * oc_ref[0, 0, h] + gt[1:2] * o_s[:, sl] + gt[2:3] * o_w[:, sl]
        o_ref[0, 0, h] = out.astype(o_ref.dtype)


def _nsa_attn(q_t, ks, vs_t, kw, vw_t, sel_t, oc_t, gates_t, tq=NSA_TQ):
    b, g, r, dk, t = q_t.shape
    ns = sel_t.shape[2]
    qspec = pl.BlockSpec((1, 1, r, dk, tq), lambda bi, gi, i: (bi, gi, 0, 0, i))
    kspec = pl.BlockSpec((1, 1, t, dk), lambda bi, gi, i: (bi, gi, 0, 0))
    vspec = lambda v: pl.BlockSpec((1, 1) + v.shape[2:], lambda bi, gi, i: (bi, gi, 0, 0, 0))
    return pl.pallas_call(
        _nsa_attn_kernel,
        out_shape=jax.ShapeDtypeStruct((b, g, r, dk, t), BF16),
        grid=(b, g, t // tq),
        in_specs=[
            qspec, kspec, vspec(vs_t), kspec, vspec(vw_t),
            pl.BlockSpec((1, 1, ns, tq), lambda bi, gi, i: (bi, gi, 0, i)),
            qspec,
            pl.BlockSpec((1, 1, r, 3, tq), lambda bi, gi, i: (bi, gi, 0, 0, i)),
        ],
        out_specs=qspec,
        scratch_shapes=[pltpu.VMEM((r, 1, tq), F32), pltpu.VMEM((r, 1, tq), F32), pltpu.VMEM((dk, r * tq), F32)],
        compiler_params=_cp(("arbitrary", "arbitrary", "arbitrary")),
        name="nsa_attention",
    )(q_t, ks, vs_t, kw, vw_t, sel_t, oc_t, gates_t)


def _nsa(u3, q_norm, k_norm, pos_k, pos_v, ck_w1, ck_b1, ck_w2, ck_b2, cv_w1, cv_b1, cv_w2, cv_b2):
    b, t, _ = u3.shape
    n = b * t
    g, r, dk = NSA_G, NSA_R, NSA_HEAD
    tile_g = lambda p: jnp.tile(p.astype(F32), g).reshape(1, g * dk)
    qn, ksn, kwn, gsig = _nsa_prep(u3.reshape(n, IN_COLS), tile_g(q_norm), tile_g(k_norm[1]), tile_g(k_norm[2]))
    per_group = lambda a: a.reshape(b, t, g, dk).transpose(0, 2, 1, 3)
    kv = lambda idx: u3[..., COL_KV + idx * NSA_KV:COL_KV + (idx + 1) * NSA_KV]
    q_t = qn.reshape(b, t, g, r, dk).transpose(0, 2, 3, 4, 1)
    nz = t // CMP_STRIDE
    zk = per_group(kv(0).astype(BF16)).reshape(b, g, nz, CMP_STRIDE * dk)
    zv = per_group(kv(1).astype(BF16)).reshape(b, g, nz, CMP_STRIDE * dk)
    row = lambda p: p.reshape(1, -1).astype(F32)
    kc, vc = _nsa_cmp(zk, zv, row(pos_k), row(pos_v), ck_w1.astype(BF16), row(ck_b1), ck_w2.astype(BF16), row(ck_b2),
                      cv_w1.astype(BF16), row(cv_b1), cv_w2.astype(BF16), row(cv_b2), row(k_norm[0]))
    oc_t, sel_t = _nsa_select(q_t, kc, vc.transpose(0, 1, 3, 2))
    gates_t = gsig[:, :g * r * 3].reshape(b, t, g, r, 3).transpose(0, 2, 3, 4, 1)

    def tiled_t(a, tile):
        return a.astype(BF16).reshape(b, t // tile, tile, g, dk).transpose(0, 3, 1, 4, 2)

    o_t = _nsa_attn(q_t, per_group(ksn), tiled_t(kv(3), min(NSA_KT, t)), per_group(kwn), tiled_t(kv(5), min(NSA_KWT, t)),
                    sel_t, oc_t, gates_t)
    return o_t.transpose(0, 4, 1, 2, 3).reshape(b, t, NSA_WIDTH)


def _outproj_kernel(x_ref, yr_ref, yn_ref, w_ref, g_ref, rw_ref, rb_ref,
                    h_ref, xn_ref, e_ref, gate_ref, rank_ref, cnt_ref, cnt_acc):
    tm = x_ref.shape[0]

    @pl.when(pl.program_id(0) == 0)
    def _():
        cnt_acc[...] = jnp.zeros_like(cnt_acc)

    h = x_ref[...] + _dot(yr_ref[...], w_ref[0:RW_WIDTH, :]) + _dot(yn_ref[...], w_ref[RW_WIDTH:, :])
    h_ref[...] = h
    ms = jnp.mean(h * h, axis=-1, keepdims=True)
    xn = h * lax.rsqrt(ms + RMS_EPS) * g_ref[...]
    xn_ref[...] = xn
    logits = _dot(xn, rw_ref[...], HI) + rb_ref[...]
    lane = _iota((tm, LANES), 1).astype(F32)
    work = logits
    top_e, top_v = [], []
    for _ in range(TOP_K):
        mx = jnp.max(work, axis=1, keepdims=True)
        idx = jnp.min(jnp.where(work == mx, lane, float(LANES)), axis=1, keepdims=True)
        top_e.append(idx)
        top_v.append(mx)
        work = jnp.where(lane == idx, -jnp.inf, work)
    ex = [jnp.exp(v - top_v[0]) for v in top_v]
    den = ex[0] + ex[1] + ex[2] + ex[3]
    multihot = jnp.zeros((tm, LANES), F32)
    for e in top_e:
        multihot = multihot + (lane == e).astype(F32)
    strict = (_iota((tm, tm), 1) < _iota((tm, tm), 0)).astype(BF16)
    before = cnt_acc[...] + _dot(strict, multihot.astype(BF16))
    e_out = jnp.zeros((tm, LANES), F32)
    g_out = jnp.zeros((tm, LANES), F32)
    r_out = jnp.zeros((tm, LANES), F32)
    for k in range(TOP_K):
        slot = lane == float(k)
        rank_k = jnp.sum(jnp.where(lane == top_e[k], before, 0.0), axis=1, keepdims=True)
        e_out = jnp.where(slot, top_e[k], e_out)
        g_out = jnp.where(slot, ex[k] / den, g_out)
        r_out = jnp.where(slot, rank_k, r_out)
    e_ref[...] = e_out.astype(I32)
    gate_ref[...] = g_out
    rank_ref[...] = r_out.astype(I32)
    cnt_acc[...] = cnt_acc[...] + jnp.sum(multihot, axis=0, keepdims=True)
    cnt_ref[...] = cnt_acc[...].astype(I32)


def _outproj(x2, y_rw, y_nsa, w_out_bf, g, rw_p, rb_p, tm=256):
    n, d = x2.shape
    rowblk = lambda w: pl.BlockSpec((tm, w), lambda i: (i, 0))
    full = lambda a: pl.BlockSpec(a.shape, lambda i: (0,) * a.ndim)
    return pl.pallas_call(
        _outproj_kernel,
        out_shape=(jax.ShapeDtypeStruct((n, d), F32), jax.ShapeDtypeStruct((n, d), F32),
                   jax.ShapeDtypeStruct((n, LANES), I32), jax.ShapeDtypeStruct((n, LANES), F32),
                   jax.ShapeDtypeStruct((n, LANES), I32), jax.ShapeDtypeStruct((1, LANES), I32)),
        grid=(n // tm,),
        in_specs=[rowblk(d), rowblk(RW_WIDTH), rowblk(NSA_WIDTH), full(w_out_bf), full(g), full(rw_p), full(rb_p)],
        out_specs=(rowblk(d), rowblk(d), rowblk(LANES), rowblk(LANES), rowblk(LANES),
                   pl.BlockSpec((1, LANES), lambda i: (0, 0))),
        scratch_shapes=[pltpu.VMEM((1, LANES), F32)],
        compiler_params=_cp(("arbitrary",)),
        name="outproj_router",
    )(x2, y_rw, y_nsa, w_out_bf, g, rw_p, rb_p)


GATHER_UNROLL = 8


def _gather_kernel(row_tok, n_act, x_hbm, o_hbm, zero_ref, sems, zsem, *, tg):
    i = pl.program_id(0)
    last = pl.num_programs(0) - 1
    base = i * tg
    slot = i % 2

    def tile_wait(step, s):
        rows = pl.ds(step * tg, tg)
        pltpu.make_async_copy(o_hbm.at[rows, :], o_hbm.at[rows, :], sems.at[s]).wait()

    @pl.when(base < n_act[0])
    def _():
        def issue(r8, c):
            for u in range(GATHER_UNROLL):
                r = base + r8 * GATHER_UNROLL + u
                pltpu.make_async_copy(x_hbm.at[pl.ds(row_tok[r], 1), :], o_hbm.at[pl.ds(r, 1), :], sems.at[slot]).start()
            return c

        lax.fori_loop(0, tg // GATHER_UNROLL, issue, 0)

    @pl.when((i > 0) & (base - tg < n_act[0]))
    def _():
        tile_wait(i - 1, 1 - slot)

    @pl.when((i == last) & (base < n_act[0]))
    def _():
        tile_wait(i, slot)

    @pl.when(base >= n_act[0])
    def _():
        tz = zero_ref.shape[0]
        zero_ref[...] = jnp.zeros_like(zero_ref)
        for c in range(tg // tz):
            cp = pltpu.make_async_copy(zero_ref, o_hbm.at[pl.ds(base + c * tz, tz), :], zsem)
            cp.start()
            cp.wait()


def _gather_rows(row_tok, n_act, xn, n_rows, tg=1024, tz=256):
    d = xn.shape[1]
    return pl.pallas_call(
        functools.partial(_gather_kernel, tg=tg),
        out_shape=jax.ShapeDtypeStruct((n_rows, d), xn.dtype),
        grid_spec=pltpu.PrefetchScalarGridSpec(
            num_scalar_prefetch=2,
            grid=(n_rows // tg,),
            in_specs=[pl.BlockSpec(memory_space=pl.ANY)],
            out_specs=pl.BlockSpec(memory_space=pl.ANY),
            scratch_shapes=[pltpu.VMEM((tz, d), xn.dtype), pltpu.SemaphoreType.DMA((2,)), pltpu.SemaphoreType.DMA],
        ),
        compiler_params=_cp(("arbitrary",)),
        name="moe_gather",
    )(row_tok, n_act, xn)


def _gm1_kernel(e_s, j_s, rt_s, valid_s, first_s, rto_s, jo_s, x_ref, wg_ref, wl_ref, bg_ref, bl_ref, o_ref,
                wgb, wlb):
    s = pl.program_id(0)

    @pl.when(valid_s[s] == 1)
    def _():
        @pl.when(first_s[s] == 1)
        def _():
            wgb[...] = wg_ref[0].astype(BF16)
            wlb[...] = wl_ref[0].astype(BF16)

        x = x_ref[...].astype(BF16)
        gt = jnp.minimum(_dot(x, wgb[...]) + bg_ref[0], SWIGLU_LIMIT)
        lin = jnp.clip(_dot(x, wlb[...]) + bl_ref[0], -SWIGLU_LIMIT, SWIGLU_LIMIT)
        o_ref[...] = (gt * jax.nn.sigmoid(SWIGLU_ALPHA * gt) * (lin + 1.0)).astype(o_ref.dtype)

    @pl.when(valid_s[s] == 0)
    def _():
        o_ref[...] = jnp.zeros_like(o_ref)


def _gm2_kernel(e_s, j_s, rt_s, valid_s, first_s, rto_s, jo_s, x_ref, w_ref, b_ref, o_ref, wb):
    s = pl.program_id(0)

    @pl.when(valid_s[s] == 1)
    def _():
        @pl.when(first_s[s] == 1)
        def _():
            wb[...] = w_ref[0].astype(BF16)

        o_ref[...] = _dot(x_ref[...], wb[...]) + b_ref[0]

    @pl.when(valid_s[s] == 0)
    def _():
        o_ref[...] = jnp.zeros_like(o_ref)


def _schedule(counts, tr, n_col_tiles, max_tiles):
    tiles_e = (counts + tr - 1) // tr
    tile_end = jnp.cumsum(tiles_e)
    tile_start = tile_end - tiles_e
    steps_e = tiles_e * n_col_tiles
    step_end = jnp.cumsum(steps_e)
    total = step_end[-1]
    n_steps = max_tiles * n_col_tiles
    step = jnp.arange(n_steps, dtype=I32)
    s = jnp.minimum(step, total - 1)
    e = jnp.sum((step_end[None, :] <= s[:, None]).astype(I32), axis=1)
    local = s - (step_end[e] - steps_e[e])
    te = jnp.maximum(tiles_e[e], 1)
    j = (local // te).astype(I32)
    rt = (tile_start[e] + local % te).astype(I32)
    valid = step < total
    first = ((local % te) == 0).astype(I32)
    spare = step - total
    rt_out = jnp.where(valid, rt, tile_end[-1] + spare // n_col_tiles).astype(I32)
    j_out = jnp.where(valid, j, spare % n_col_tiles).astype(I32)
    return (e, j, rt, valid.astype(I32), first, rt_out, j_out), tile_start * tr, tile_end[-1] * tr


def _grouped_mlp(x_rows, sched1, sched2, w1, b1, w2, b2, tr, tn1, tn2):
    n_rows, d = x_rows.shape
    de = w2.shape[1]
    nt1 = de // tn1
    b1r = b1.reshape(N_EXPERTS, 1, 2 * de)
    b2r = b2.reshape(N_EXPERTS, 1, d)
    act = pl.pallas_call(
        _gm1_kernel,
        out_shape=jax.ShapeDtypeStruct((n_rows, de), BF16),
        grid_spec=pltpu.PrefetchScalarGridSpec(
            num_scalar_prefetch=7,
            grid=(sched1[0].shape[0],),
            in_specs=[
                pl.BlockSpec((tr, d), lambda s, e, j, rt, v, f, ro, jo: (rt[s], 0)),
                pl.BlockSpec((1, d, tn1), lambda s, e, j, rt, v, f, ro, jo: (e[s], 0, j[s])),
                pl.BlockSpec((1, d, tn1), lambda s, e, j, rt, v, f, ro, jo: (e[s], 0, nt1 + j[s])),
                pl.BlockSpec((1, 1, tn1), lambda s, e, j, rt, v, f, ro, jo: (e[s], 0, j[s])),
                pl.BlockSpec((1, 1, tn1), lambda s, e, j, rt, v, f, ro, jo: (e[s], 0, nt1 + j[s])),
            ],
            out_specs=pl.BlockSpec((tr, tn1), lambda s, e, j, rt, v, f, ro, jo: (ro[s], jo[s])),
            scratch_shapes=[pltpu.VMEM((d, tn1), BF16), pltpu.VMEM((d, tn1), BF16)],
        ),
        compiler_params=_cp(("arbitrary",)),
        name="moe_up",
    )(*sched1, x_rows, w1, w1, b1r, b1r)
    return pl.pallas_call(
        _gm2_kernel,
        out_shape=jax.ShapeDtypeStruct((n_rows, d), F32),
        grid_spec=pltpu.PrefetchScalarGridSpec(
            num_scalar_prefetch=7,
            grid=(sched2[0].shape[0],),
            in_specs=[
                pl.BlockSpec((tr, de), lambda s, e, j, rt, v, f, ro, jo: (rt[s], 0)),
                pl.BlockSpec((1, de, tn2), lambda s, e, j, rt, v, f, ro, jo: (e[s], 0, j[s])),
                pl.BlockSpec((1, 1, tn2), lambda s, e, j, rt, v, f, ro, jo: (e[s], 0, j[s])),
            ],
            out_specs=pl.BlockSpec((tr, tn2), lambda s, e, j, rt, v, f, ro, jo: (ro[s], jo[s])),
            scratch_shapes=[pltpu.VMEM((de, tn2), BF16)],
        ),
        compiler_params=_cp(("arbitrary",)),
        name="moe_down",
    )(*sched2, act, w2, b2r)


def _ple_kernel(dest, h_ref, gate_ref, p_ref, y_hbm, g_ref, gw_ref, pw_ref, o_ref, buf, sems):
    tm = h_ref.shape[0]
    i = pl.program_id(0)
    slot = i % 2

    def fetch(step, s):
        def issue(t2, c):
            for u in range(2):
                t = t2 * 2 + u
                for k in range(TOP_K):
                    pltpu.make_async_copy(y_hbm.at[pl.ds(dest[(step * tm + t) * TOP_K + k], 1), :],
                                          buf.at[s, k, pl.ds(t, 1), :], sems.at[s]).start()
            return c

        lax.fori_loop(0, tm // 2, issue, 0)

    @pl.when(i == 0)
    def _():
        fetch(0, 0)

    @pl.when(i + 1 < pl.num_programs(0))
    def _():
        fetch(i + 1, 1 - slot)

    for k in range(TOP_K):
        pltpu.make_async_copy(y_hbm.at[pl.ds(0, tm), :], buf.at[slot, k], sems.at[slot]).wait()
    gate = gate_ref[...]
    h = h_ref[...]
    for k in range(TOP_K):
        h = h + gate[:, k:k + 1] * buf[slot, k]
    ms = jnp.mean(h * h, axis=-1, keepdims=True)
    hn = (h * lax.rsqrt(ms + RMS_EPS) * g_ref[...]).astype(BF16)
    pgate = jax.nn.sigmoid(_dot(hn, gw_ref[...]))
    o_ref[...] = h + pgate * _dot(p_ref[...].astype(BF16), pw_ref[...])


def _combine_ple(dest, h1, gate, p2, y_rows, g, gw_bf, pw_bf, tm=128):
    n, d = h1.shape
    rowblk = lambda w: pl.BlockSpec((tm, w), lambda i, ds: (i, 0))
    full = lambda a: pl.BlockSpec(a.shape, lambda i, ds: (0,) * a.ndim)
    return pl.pallas_call(
        _ple_kernel,
        out_shape=jax.ShapeDtypeStruct((n, d), F32),
        grid_spec=pltpu.PrefetchScalarGridSpec(
            num_scalar_prefetch=1,
            grid=(n // tm,),
            in_specs=[rowblk(d), rowblk(LANES), rowblk(p2.shape[1]), pl.BlockSpec(memory_space=pl.ANY),
                      full(g), full(gw_bf), full(pw_bf)],
            out_specs=rowblk(d),
            scratch_shapes=[pltpu.VMEM((2, TOP_K, tm, d), F32), pltpu.SemaphoreType.DMA((2,))],
        ),
        compiler_params=_cp(("arbitrary",)),
        name="combine_ple",
    )(dest, h1, gate, p2, y_rows, g, gw_bf, pw_bf)


MOE_TR = 512
MOE_TN1 = 512
MOE_TN2 = 1024


def _layer(x, p, mix_norm_g, w_in, rw, nsa, w_out, moe_norm_g, router_w, router_b, moe_w1, moe_b1, moe_w2, moe_b2,
           ple_norm_g, ple_w, ple_gate_w):
    b, t, d = x.shape
    n = b * t
    x2 = x.reshape(n, d)
    row = lambda v: v.reshape(1, -1).astype(F32)
    n_lora = sum(RW_LORA)
    rw_cols = 3 * RW_WIDTH + n_lora
    kv0 = rw_cols + NSA_WIDTH
    n_gate = 3 * NSA_G * NSA_R
    zeros = lambda w: jnp.zeros((d, w), w_in.dtype)
    w_p = jnp.concatenate([
        w_in[:, :3 * RW_WIDTH], w_in[:, rw_cols:kv0], w_in[:, kv0:kv0 + 6 * NSA_KV],
        w_in[:, 3 * RW_WIDTH:rw_cols], zeros(MISC_GATE - n_lora), w_in[:, kv0 + 6 * NSA_KV:],
        zeros(MISC_W - MISC_GATE - n_gate)], axis=1).astype(BF16)
    u = _inproj(x2, row(mix_norm_g), w_p)
    u3 = u.reshape(b, t, IN_COLS)
    y_rw = _rwkv(u3, *_rwkv_params(*rw))
    y_nsa = _nsa(u3, *nsa)
    out = _tail(x2, y_rw.reshape(n, RW_WIDTH), y_nsa.reshape(n, NSA_WIDTH), p.reshape(n, PLE_DIM), w_out, moe_norm_g,
                router_w, router_b, moe_w1, moe_b1, moe_w2, moe_b2, ple_norm_g, ple_w, ple_gate_w)
    return out.reshape(b, t, d)


def _tail(x2, y_rw, y_nsa, p2, w_out, moe_norm_g, router_w, router_b, moe_w1, moe_b1, moe_w2, moe_b2,
          ple_norm_g, ple_w, ple_gate_w):
    n, d = x2.shape
    row = lambda v: v.reshape(1, -1).astype(F32)
    rw_p = jnp.zeros((d, LANES), F32).at[:, :N_EXPERTS].set(router_w)
    rb_p = jnp.full((1, LANES), NEG, F32).at[0, :N_EXPERTS].set(router_b)
    h1, xn, top_e, gate, rank, counts = _outproj(x2, y_rw, y_nsa, w_out.astype(BF16), row(moe_norm_g), rw_p, rb_p)
    counts = counts[0, :N_EXPERTS]
    max_tiles = (n * TOP_K) // MOE_TR + N_EXPERTS
    n_rows = max_tiles * MOE_TR
    sched1, pstart, n_act = _schedule(counts, MOE_TR, D_EXPERT // MOE_TN1, max_tiles)
    sched2, _, _ = _schedule(counts, MOE_TR, d // MOE_TN2, max_tiles)
    top_e = top_e[:, :TOP_K]
    dest = (pstart[top_e] + rank[:, :TOP_K]).astype(I32)
    tok = jnp.broadcast_to(jnp.arange(n, dtype=I32)[:, None], (n, TOP_K))
    row_tok = jnp.zeros((n_rows,), I32).at[dest.reshape(-1)].set(tok.reshape(-1))
    x_rows = _gather_rows(row_tok, n_act.reshape(1).astype(I32), xn, n_rows)
    y_rows = _grouped_mlp(x_rows, sched1, sched2, moe_w1, moe_b1, moe_w2, moe_b2, MOE_TR, MOE_TN1, MOE_TN2)
    return _combine_ple(dest.reshape(-1), h1, gate, p2, y_rows, row(ple_norm_g),
                        ple_gate_w.astype(BF16), ple_w.astype(BF16))


def kernel(x, p, mix_norm_g, w_in, rw_mu, rw_w0, rw_w2, rw_a0, rw_a2, rw_g2, rw_k_k, rw_k_a, rw_r_k, rw_lnx_w,
           rw_lnx_b, nsa_q_norm, nsa_k_norm, cmp_pos_k, cmp_pos_v, cmp_k_w1, cmp_k_b1, cmp_k_w2, cmp_k_b2, cmp_v_w1,
           cmp_v_b1, cmp_v_w2, cmp_v_b2, w_out, moe_norm_g, router_w, router_b, moe_w1, moe_b1, moe_w2, moe_b2,
           ple_norm_g, ple_w, ple_gate_w):
    h = x
    for i in range(mix_norm_g.shape[0]):
        rw = (rw_mu[i], rw_w0[i], rw_w2[i], rw_a0[i], rw_a2[i], rw_g2[i], rw_k_k[i], rw_k_a[i], rw_r_k[i],
              rw_lnx_w[i], rw_lnx_b[i])
        nsa = (nsa_q_norm[i], nsa_k_norm[i], cmp_pos_k[i], cmp_pos_v[i], cmp_k_w1[i], cmp_k_b1[i], cmp_k_w2[i],
               cmp_k_b2[i], cmp_v_w1[i], cmp_v_b1[i], cmp_v_w2[i], cmp_v_b2[i])
        h = _layer(h, p[i], mix_norm_g[i], w_in[i], rw, nsa, w_out[i], moe_norm_g[i], router_w[i], router_b[i],
                   moe_w1[i], moe_b1[i], moe_w2[i], moe_b2[i], ple_norm_g[i], ple_w[i], ple_gate_w[i])
    return h
```

```python
import functools

import jax
import jax.numpy as jnp
from jax import lax
from jax.experimental import pallas as pl
from jax.experimental.pallas import tpu as pltpu

F32 = jnp.float32
BF16 = jnp.bfloat16
I32 = jnp.int32
HI = lax.Precision.HIGHEST

LANES = 128
VMEM_LIMIT = 56 * 1024 * 1024

D_MODEL = 2048
RMS_EPS = 1e-6
RW_WIDTH = 1024
RW_HEAD = 64
RW_LORA = (64, 64, 160)
RW_GN_EPS = 64e-5
NSA_WIDTH = 1024
NSA_HEAD = 64
NSA_G = 4
NSA_R = 4
NSA_KV = NSA_G * NSA_HEAD
CMP_LEN = 32
CMP_STRIDE = 16
CMP_HIDDEN = 256
SLC_BLOCK = 64
N_SELECT = 16
WINDOW = 512
N_EXPERTS = 32
TOP_K = 4
D_EXPERT = 2048
SWIGLU_LIMIT = 7.0
SWIGLU_ALPHA = 1.702
PLE_DIM = 256

COL_RKV = 0
COL_Q = 3072
COL_KV = 4096
COL_MISC = 5632
MISC_W = 512
MISC_GATE = 384
IN_COLS = 6144

RW_HG = 4
RW_LW = RW_HG * RW_HEAD
RW_C = 64
NSA_TQ = 256
NSA_KT = 512
NSA_KWT = 256
NEG = -1e30
LOG2E = 1.4426950408889634


def _cp(sem, vmem=VMEM_LIMIT):
    return pltpu.CompilerParams(dimension_semantics=sem, vmem_limit_bytes=vmem)


def _dot(a, b, prec=None):
    return jnp.dot(a, b, preferred_element_type=F32, precision=prec)


def _dot_nt(a, b, prec=None):
    return lax.dot_general(a, b, (((1,), (1,)), ((), ())), preferred_element_type=F32, precision=prec)


def _dot_split(x, m, pieces=2, left=False):
    out, rem = None, x
    for _ in range(pieces):
        part = rem.astype(BF16)
        rem = rem - part.astype(F32)
        term = _dot(m, part) if left else _dot(part, m)
        out = term if out is None else out + term
    return out


def _iota(shape, dim):
    return lax.broadcasted_iota(I32, shape, dim)


def _seg_matrix(n, seg, scale):
    same = (_iota((n, n), 0) // seg) == (_iota((n, n), 1) // seg)
    return jnp.where(same, scale, 0.0).astype(BF16)


def _inproj_kernel(x_ref, g_ref, w_ref, o_ref, xn_ref):
    @pl.when(pl.program_id(1) == 0)
    def _():
        x = x_ref[...]
        ms = jnp.mean(x * x, axis=-1, keepdims=True)
        xn_ref[...] = (x * lax.rsqrt(ms + RMS_EPS) * g_ref[...]).astype(BF16)

    o_ref[...] = _dot(xn_ref[...], w_ref[...])


def _inproj(x2, g, w_bf, tm=512, tn=512):
    n, d = x2.shape
    nc = w_bf.shape[1]
    return pl.pallas_call(
        _inproj_kernel,
        out_shape=jax.ShapeDtypeStruct((n, nc), F32),
        grid=(n // tm, nc // tn),
        in_specs=[
            pl.BlockSpec((tm, d), lambda i, j: (i, 0)),
            pl.BlockSpec((1, d), lambda i, j: (0, 0)),
            pl.BlockSpec((d, tn), lambda i, j: (0, j)),
        ],
        out_specs=pl.BlockSpec((tm, tn), lambda i, j: (i, j)),
        scratch_shapes=[pltpu.VMEM((tm, d), BF16)],
        compiler_params=_cp(("arbitrary", "arbitrary")),
        name="inproj",
    )(x2, g, w_bf)


def _softplus(z):
    return jnp.maximum(z, 0.0) + jnp.log(1.0 + jnp.exp(-jnp.abs(z)))


def _rwkv_kernel(r_ref, k_ref, v_ref, m_ref, mur_ref, muk_ref, muv_ref, mum_ref, w2_ref, a2_ref, g2_ref,
                 w0_ref, a0_ref, kk_ref, ka_ref, rk_ref, lw_ref, lb_ref, o_ref,
                 st_ref, pr_ref, pk_ref, pv_ref, pm_ref, *, nchunk):
    tt = nchunk * RW_C
    lw = RW_LW

    @pl.when(pl.program_id(2) == 0)
    def _():
        st_ref[...] = jnp.zeros_like(st_ref)
        pr_ref[...] = jnp.zeros_like(pr_ref)
        pk_ref[...] = jnp.zeros_like(pk_ref)
        pv_ref[...] = jnp.zeros_like(pv_ref)
        pm_ref[...] = jnp.zeros_like(pm_ref)

    def shift_lerp(x, prev_ref, mu):
        rolled = pltpu.roll(x, 1, 0)
        first = _iota(x.shape, 0) == 0
        sh = jnp.where(first, prev_ref[0:1, :], rolled)
        prev_ref[0:1, :] = x[tt - 1:tt, :]
        return x + (sh - x) * mu

    r = shift_lerp(r_ref[0], pr_ref, mur_ref[...])
    k = shift_lerp(k_ref[0], pk_ref, muk_ref[...])
    v = shift_lerp(v_ref[0], pv_ref, muv_ref[...])
    m = shift_lerp(m_ref[0], pm_ref, mum_ref[...])

    lw_dec = _dot(jnp.tanh(m).astype(BF16), w2_ref[...])
    lw_a = _dot(m.astype(BF16), a2_ref[...])
    g = _dot(jax.nn.sigmoid(m).astype(BF16), g2_ref[...])
    w_log = -_softplus(-(w0_ref[...] + lw_dec)) - 0.5
    ld = -jnp.exp(w_log)
    a = jax.nn.sigmoid(a0_ref[...] + lw_a)

    ones_seg = _seg_matrix(lw, RW_HEAD, 1.0)
    kk = k * kk_ref[...]
    kk = kk / jnp.maximum(jnp.sqrt(_dot_split(kk * kk, ones_seg)), 1e-12)
    k2 = k * (1.0 + (a - 1.0) * ka_ref[...])
    av = -kk
    bv = kk * a

    ri = _iota((lw, lw), 0)
    ci = _iota((lw, lw), 1)
    same_blk = (ri // RW_C) == (ci // RW_C)
    m_strict = same_blk & (ci < ri)
    m_lower = same_blk & (ci <= ri)
    eye = (ri == ci).astype(F32)
    tri = (_iota((RW_C, RW_C), 1) <= _iota((RW_C, RW_C), 0)).astype(BF16)
    lane_head = _iota((1, lw), 1) // RW_HEAD
    hmask = [(lane_head == h).astype(F32) for h in range(RW_HG)]

    def stack_heads(x):
        return jnp.concatenate([x * hmask[h] for h in range(RW_HG)], axis=0)

    def unstack_heads(xs):
        out = xs[0:RW_C] * hmask[0]
        for h in range(1, RW_HG):
            out = out + xs[h * RW_C:(h + 1) * RW_C] * hmask[h]
        return out

    def fold_heads(xs):
        out = xs[0:RW_C]
        for h in range(1, RW_HG):
            out = out + xs[h * RW_C:(h + 1) * RW_C]
        return out

    chunks = []
    for c in range(nchunk):
        sl = slice(c * RW_C, (c + 1) * RW_C)
        ld_c = ld[sl]
        cs = _dot_split(ld_c, tri, pieces=3, left=True)
        w_in = jnp.exp(cs)
        w_inv = jnp.exp(-cs)
        w_prev = jnp.exp(cs - ld_c)
        w_last = w_in[RW_C - 1:RW_C, :]
        bt = bv[sl] * w_inv
        kt = k2[sl] * w_inv
        at_s = stack_heads(av[sl] * w_prev)
        rt_s = stack_heads(r[sl] * w_in)
        v_c = v[sl]
        chunks.append(dict(
            at_s=at_s, rt_s=rt_s, at_b=at_s.astype(BF16), rt_b=rt_s.astype(BF16), w_last=w_last,
            bt4=jnp.concatenate([bt] * RW_HG, axis=0).astype(BF16),
            kt4=jnp.concatenate([kt] * RW_HG, axis=0).astype(BF16),
            v4=jnp.concatenate([v_c] * RW_HG, axis=0).astype(BF16),
            bp=(bt * w_last).astype(BF16),
            sv=jnp.where(same_blk, _dot(v_c.T.astype(BF16), (kt * w_last).astype(BF16)), 0.0)))
    for ch in chunks:
        ch["a_ab"] = jnp.where(m_strict, _dot_nt(ch["at_b"], ch["bt4"]), 0.0)
        ch["a_ak"] = jnp.where(m_strict, _dot_nt(ch["at_b"], ch["kt4"]), 0.0).astype(BF16)
        ch["a_rb"] = jnp.where(m_lower, _dot_nt(ch["rt_b"], ch["bt4"]), 0.0).astype(BF16)
        ch["a_rk"] = jnp.where(m_lower, _dot_nt(ch["rt_b"], ch["kt4"]), 0.0).astype(BF16)
        ch["pw"] = ch["a_ab"]
        ch["inv"] = eye + ch["a_ab"]
    for _ in range(RW_C.bit_length() - 2):
        for ch in chunks:
            pw_b = ch["pw"].astype(BF16)
            ch["pw"] = _dot(pw_b, pw_b)
        for ch in chunks:
            ch["inv"] = ch["inv"] + _dot(ch["inv"].astype(BF16), ch["pw"].astype(BF16))
    for ch in chunks:
        inv_b = ch["inv"].astype(BF16)
        ia = _dot(inv_b, ch["at_b"])
        u_const = _dot(inv_b, _dot(ch["a_ak"], ch["v4"]).astype(BF16))
        g_mat = ch["rt_s"] + _dot(ch["a_rb"], ia.astype(BF16))
        y_const = _dot(ch["a_rb"], u_const.astype(BF16)) + _dot(ch["a_rk"], ch["v4"])
        ch["ia_u"] = fold_heads(ia).astype(BF16)
        ch["g_u"] = fold_heads(g_mat).astype(BF16)
        ch["u_const_t"] = unstack_heads(u_const).T
        ch["y_const"] = unstack_heads(y_const)

    ys = []
    st = st_ref[...]
    for ch in chunks:
        st_b = st.astype(BF16)
        ys.append(_dot_nt(ch["g_u"], st_b) + ch["y_const"])
        u_t = _dot_nt(st_b, ch["ia_u"]) + ch["u_const_t"]
        st = st * ch["w_last"] + jnp.where(same_blk, _dot(u_t.astype(BF16), ch["bp"]), 0.0) + ch["sv"]
    st_ref[...] = st
    y = jnp.concatenate(ys, axis=0) if nchunk > 1 else ys[0]

    avg_seg = _seg_matrix(lw, RW_HEAD, 1.0 / RW_HEAD)
    mean = _dot_split(y, avg_seg)
    yc = y - mean
    var = _dot_split(yc * yc, avg_seg)
    yn = yc * lax.rsqrt(var + RW_GN_EPS) * lw_ref[...] + lb_ref[...]
    bonus = _dot_split(r * k2 * rk_ref[...], ones_seg) * v
    o_ref[0] = ((yn + bonus) * g).astype(o_ref.dtype)


def _rwkv(u3, mu_p, w2p, a2p, g2p, w0, a0, k_k, k_a, r_k, lnx_w, lnx_b, nchunk=4):
    b, t, _ = u3.shape
    tt = nchunk * RW_C
    lw = RW_LW
    nhg = RW_WIDTH // lw
    col = lambda base: (lambda bi, h, ti: (bi, ti, base + h))
    vec = lambda base: (lambda bi, h, ti: (0, base + h))
    par = pl.BlockSpec((1, lw), vec(0))
    return pl.pallas_call(
        functools.partial(_rwkv_kernel, nchunk=nchunk),
        out_shape=jax.ShapeDtypeStruct((b, t, RW_WIDTH), BF16),
        grid=(b, nhg, t // tt),
        in_specs=[
            pl.BlockSpec((1, tt, lw), col(0)),
            pl.BlockSpec((1, tt, lw), col(nhg)),
            pl.BlockSpec((1, tt, lw), col(2 * nhg)),
            pl.BlockSpec((1, tt, MISC_W), lambda bi, h, ti: (bi, ti, COL_MISC // MISC_W)),
            pl.BlockSpec((1, lw), vec(0)),
            pl.BlockSpec((1, lw), vec(nhg)),
            pl.BlockSpec((1, lw), vec(2 * nhg)),
            pl.BlockSpec((1, MISC_W), lambda bi, h, ti: (0, COL_MISC // MISC_W)),
            pl.BlockSpec((MISC_W, lw), lambda bi, h, ti: (0, h)),
            pl.BlockSpec((MISC_W, lw), lambda bi, h, ti: (0, h)),
            pl.BlockSpec((MISC_W, lw), lambda bi, h, ti: (0, h)),
            par, par, par, par, par, par, par,
        ],
        out_specs=pl.BlockSpec((1, tt, lw), lambda bi, h, ti: (bi, ti, h)),
        scratch_shapes=[
            pltpu.VMEM((lw, lw), F32),
            pltpu.VMEM((8, lw), F32),
            pltpu.VMEM((8, lw), F32),
            pltpu.VMEM((8, lw), F32),
            pltpu.VMEM((8, MISC_W), F32),
        ],
        compiler_params=_cp(("arbitrary", "arbitrary", "arbitrary")),
        name="rwkv7",
    )(u3, u3, u3, u3, mu_p, mu_p, mu_p, mu_p, w2p, a2p, g2p, w0, a0, k_k, k_a, r_k, lnx_w, lnx_b)


def _pad_rows(w, rows, offset):
    return jnp.zeros((rows, w.shape[1]), w.dtype).at[offset:offset + w.shape[0]].set(w)


def _rwkv_params(mu, w0, w2, a0, a2, g2, k_k, k_a, r_k, lnx_w, lnx_b):
    mu_p = jnp.zeros((1, IN_COLS), F32)
    mu_p = mu_p.at[0, COL_RKV:COL_RKV + 3 * RW_WIDTH].set(mu[:3 * RW_WIDTH])
    mu_p = mu_p.at[0, COL_MISC:COL_MISC + sum(RW_LORA)].set(mu[3 * RW_WIDTH:])
    o1, o2 = RW_LORA[0], RW_LORA[0] + RW_LORA[1]
    row = lambda p: p.reshape(1, RW_WIDTH).astype(F32)
    return (mu_p, _pad_rows(w2, MISC_W, 0).astype(BF16), _pad_rows(a2, MISC_W, o1).astype(BF16),
            _pad_rows(g2, MISC_W, o2).astype(BF16), row(w0), row(a0), row(k_k), row(k_a), row(r_k),
            row(lnx_w), row(lnx_b))


def _nsa_prep_kernel(q_ref, ks_ref, kw_ref, gt_ref, qg_ref, ksg_ref, kwg_ref, qo_ref, kso_ref, kwo_ref, go_ref):
    avg = _seg_matrix(NSA_KV, NSA_HEAD, 1.0 / NSA_HEAD)

    def head_norm(x, g):
        ms = _dot_split(x * x, avg)
        return x * lax.rsqrt(ms + RMS_EPS) * g

    scale = NSA_HEAD ** -0.5 * LOG2E
    for s in range(NSA_WIDTH // NSA_KV):
        sl = slice(s * NSA_KV, (s + 1) * NSA_KV)
        qo_ref[:, sl] = (head_norm(q_ref[:, sl], qg_ref[...]) * scale).astype(qo_ref.dtype)
    kso_ref[...] = head_norm(ks_ref[...], ksg_ref[...]).astype(kso_ref.dtype)
    kwo_ref[...] = head_norm(kw_ref[...], kwg_ref[...]).astype(kwo_ref.dtype)
    go_ref[...] = jax.nn.sigmoid(gt_ref[...])


def _nsa_prep(u2, q_gain, ks_gain, kw_gain, tm=512):
    n = u2.shape[0]
    kvb = COL_KV // NSA_KV
    gain = pl.BlockSpec((1, NSA_KV), lambda i: (0, 0))
    return pl.pallas_call(
        _nsa_prep_kernel,
        out_shape=(jax.ShapeDtypeStruct((n, NSA_WIDTH), BF16), jax.ShapeDtypeStruct((n, NSA_KV), BF16),
                   jax.ShapeDtypeStruct((n, NSA_KV), BF16), jax.ShapeDtypeStruct((n, LANES), F32)),
        grid=(n // tm,),
        in_specs=[
            pl.BlockSpec((tm, NSA_WIDTH), lambda i: (i, COL_Q // NSA_WIDTH)),
            pl.BlockSpec((tm, NSA_KV), lambda i: (i, kvb + 2)),
            pl.BlockSpec((tm, NSA_KV), lambda i: (i, kvb + 4)),
            pl.BlockSpec((tm, LANES), lambda i: (i, (COL_MISC + MISC_GATE) // LANES)),
            gain, gain, gain,
        ],
        out_specs=(pl.BlockSpec((tm, NSA_WIDTH), lambda i: (i, 0)), pl.BlockSpec((tm, NSA_KV), lambda i: (i, 0)),
                   pl.BlockSpec((tm, NSA_KV), lambda i: (i, 0)), pl.BlockSpec((tm, LANES), lambda i: (i, 0))),
        compiler_params=_cp(("arbitrary",)),
        name="nsa_prep",
    )(u2, u2, u2, u2, q_gain, ks_gain, kw_gain)


def _nsa_cmp_kernel(zk_ref, zv_ref, pk_ref, pv_ref, kw1_ref, kb1_ref, kw2_ref, kb2_ref, vw1_ref, vb1_ref,
                    vw2_ref, vb2_ref, kn_ref, ko_ref, vo_ref):
    nz = zk_ref.shape[2]
    half = kw1_ref.shape[0] // 2

    def compress(z, pos, w1_ref, b1, w2, b2):
        top = _dot(z, w1_ref[0:half, :])
        bot = _dot(z, w1_ref[half:, :])
        c1 = _dot(jnp.broadcast_to(pos, (8, pos.shape[1])).astype(BF16), w1_ref[...])[0:1] + b1
        hid = jax.nn.gelu(top + pltpu.roll(bot, nz - 1, 0) + c1)
        return _dot(hid.astype(BF16), w2) + b2

    kc = compress(zk_ref[0, 0], pk_ref[...], kw1_ref, kb1_ref[...], kw2_ref[...], kb2_ref[...])
    ms = jnp.mean(kc * kc, axis=-1, keepdims=True)
    ko_ref[0, 0] = (kc * lax.rsqrt(ms + RMS_EPS) * kn_ref[...]).astype(ko_ref.dtype)
    vc = compress(zv_ref[0, 0], pv_ref[...], vw1_ref, vb1_ref[...], vw2_ref[...], vb2_ref[...])
    vo_ref[0, 0] = vc.astype(vo_ref.dtype)


def _nsa_cmp(zk, zv, pos_k, pos_v, kw1, kb1, kw2, kb2, vw1, vb1, vw2, vb2, kn0):
    b, g, nz, zw = zk.shape
    full = lambda a: pl.BlockSpec(a.shape, lambda bi, gi: (0,) * a.ndim)
    zspec = pl.BlockSpec((1, 1, nz, zw), lambda bi, gi: (bi, gi, 0, 0))
    ospec = pl.BlockSpec((1, 1, nz, NSA_HEAD), lambda bi, gi: (bi, gi, 0, 0))
    params = (pos_k, pos_v, kw1, kb1, kw2, kb2, vw1, vb1, vw2, vb2, kn0)
    return pl.pallas_call(
        _nsa_cmp_kernel,
        out_shape=(jax.ShapeDtypeStruct((b, g, nz, NSA_HEAD), BF16),) * 2,
        grid=(b, g),
        in_specs=[zspec, zspec] + [full(a) for a in params],
        out_specs=(ospec, ospec),
        compiler_params=_cp(("arbitrary", "arbitrary")),
        name="nsa_compress",
    )(zk, zv, *params)


def _nsa_select_kernel(q_ref, kc_ref, vc_ref, oc_ref, sel_ref, *, n_sel):
    tq = q_ref.shape[4]
    nz = kc_ref.shape[2]
    ns = sel_ref.shape[2]
    t0 = pl.program_id(2) * tq
    n_idx = _iota((nz, tq), 0)
    t_idx = t0 + _iota((nz, tq), 1)
    cmask = (n_idx * CMP_STRIDE + (CMP_LEN - 1) <= t_idx) & (n_idx < nz - 1)
    kc = kc_ref[0, 0]
    vc_t = vc_ref[0, 0]
    psum = jnp.zeros((nz, tq), F32)
    for r in range(NSA_R):
        s = jnp.where(cmask, _dot(kc, q_ref[0, 0, r]), -jnp.inf)
        mx = jnp.max(s, axis=0, keepdims=True)
        e = jnp.exp2(s - jnp.where(mx == -jnp.inf, 0.0, mx))
        p = e / jnp.maximum(jnp.sum(e, axis=0, keepdims=True), 1e-30)
        psum = psum + p
        oc_ref[0, 0, r] = _dot(vc_t, p.astype(BF16))
    jb = _iota((ns, nz), 0) * SLC_BLOCK
    cb = _iota((ns, nz), 1) * CMP_STRIDE
    overlap = ((cb < jb + SLC_BLOCK) & (cb + CMP_LEN > jb) & (_iota((ns, nz), 1) < nz - 1)).astype(BF16)
    imp = _dot_split(psum, overlap, pieces=3, left=True)
    j_idx = _iota((ns, tq), 0)
    t_q = t0 + _iota((ns, tq), 1)
    qblk = t_q // SLC_BLOCK
    forced = (j_idx == 0) | (j_idx == qblk) | (j_idx == qblk - 1)
    imp = jnp.where(forced, jnp.inf, imp)
    imp = jnp.where(j_idx * SLC_BLOCK <= t_q, imp, -jnp.inf)
    rank = jnp.zeros((ns, tq), F32)
    for i in range(ns):
        row = imp[i:i + 1, :]
        before = (row > imp) | ((row == imp) & (j_idx > i))
        rank = rank + before.astype(F32)
    sel_ref[0, 0] = ((rank < n_sel) & (imp > -jnp.inf)).astype(F32)


def _nsa_select(q_t, kc, vc_t, tq=NSA_TQ):
    b, g, r, dk, t = q_t.shape
    nz = kc.shape[2]
    ns = t // SLC_BLOCK
    qspec = pl.BlockSpec((1, 1, r, dk, tq), lambda bi, gi, i: (bi, gi, 0, 0, i))
    return pl.pallas_call(
        functools.partial(_nsa_select_kernel, n_sel=min(N_SELECT, ns)),
        out_shape=(jax.ShapeDtypeStruct((b, g, r, dk, t), F32), jax.ShapeDtypeStruct((b, g, ns, t), F32)),
        grid=(b, g, t // tq),
        in_specs=[
            qspec,
            pl.BlockSpec((1, 1, nz, dk), lambda bi, gi, i: (bi, gi, 0, 0)),
            pl.BlockSpec((1, 1, dk, nz), lambda bi, gi, i: (bi, gi, 0, 0)),
        ],
        out_specs=(qspec, pl.BlockSpec((1, 1, ns, tq), lambda bi, gi, i: (bi, gi, 0, i))),
        compiler_params=_cp(("arbitrary", "arbitrary", "arbitrary")),
        name="nsa_select",
    )(q_t, kc, vc_t)


def _nsa_attn_kernel(q_ref, ks_ref, vs_ref, kw_ref, vw_ref, sel_ref, oc_ref, gt_ref, o_ref, m_ref, acc_ref):
    r, dk, tq = q_ref.shape[2:]
    kt = vs_ref.shape[4]
    kwt = vw_ref.shape[4]
    t0 = pl.program_id(2) * tq
    q_t = jnp.concatenate([q_ref[0, 0, h] for h in range(r)], axis=1)
    tpos = t0 + _iota((1, tq), 1)

    def reset():
        m_ref[...] = jnp.full_like(m_ref, NEG)
        acc_ref[...] = jnp.zeros_like(acc_ref)

    def online_update(s, bias, v_aug):
        ps, alphas = [], []
        for h in range(r):
            s_h = s[:, h * tq:(h + 1) * tq] + bias
            m_old = m_ref[h]
            m_new = jnp.maximum(m_old, jnp.max(s_h, axis=0, keepdims=True))
            alphas.append(jnp.exp2(m_old - m_new))
            ps.append(jnp.exp2((s_h - m_new).astype(BF16)))
            m_ref[h] = m_new
        acc_ref[...] = jnp.concatenate(alphas, axis=1) * acc_ref[...] + _dot(v_aug, jnp.concatenate(ps, axis=1))

    def result():
        seen = jnp.concatenate([m_ref[h] for h in range(r)], axis=1) > 0.5 * NEG
        acc = acc_ref[...]
        return jnp.where(seen, acc[0:dk] / jnp.maximum(acc[dk:dk + 1], 1e-30), 0.0)

    reset()
    nb = kt // SLC_BLOCK

    def sel_body(j, carry):
        k = ks_ref[0, 0, pl.ds(pl.multiple_of(j * kt, kt), kt), :]
        s = _dot(k, q_t)
        bm = jnp.concatenate([jnp.broadcast_to(sel_ref[0, 0, pl.ds(j * nb + c, 1), :], (SLC_BLOCK, tq))
                              for c in range(nb)], axis=0)
        kpos = j * kt + _iota((kt, tq), 0)
        online_update(s, jnp.where((bm > 0.5) & (kpos <= tpos), 0.0, NEG), vs_ref[0, 0, j])
        return carry

    lax.fori_loop(0, (t0 + tq + kt - 1) // kt, sel_body, 0)
    o_s = result()

    reset()

    def win_body(j, carry):
        k = kw_ref[0, 0, pl.ds(pl.multiple_of(j * kwt, kwt), kwt), :]
        s = _dot(k, q_t)
        kpos = j * kwt + _iota((kwt, tq), 0)
        online_update(s, jnp.where((kpos <= tpos) & (kpos > tpos - WINDOW), 0.0, NEG), vw_ref[0, 0, j])
        return carry

    lo = jnp.maximum(t0 - (WINDOW - 1), 0) // kwt
    lax.fori_loop(lo, (t0 + tq - 1) // kwt + 1, win_body, 0)
    o_w = result()

    for h in range(r):
        sl = slice(h * tq, (h + 1) * tq)
        gt = gt_ref[0, 0, h]
        out = gt[0:1] * oc_ref[0, 0, h] + gt[1:2] * o_s[:, sl] + gt[2:3] * o_w[:, sl]
        o_ref[0, 0, h] = out.astype(o_ref.dtype)


def _nsa_attn(q_t, ks, vs_t, kw, vw_t, sel_t, oc_t, gates_t, tq=NSA_TQ):
    b, g, r, dk, t = q_t.shape
    ns = sel_t.shape[2]
    qspec = pl.BlockSpec((1, 1, r, dk, tq), lambda bi, gi, i: (bi, gi, 0, 0, i))
    kspec = pl.BlockSpec((1, 1, t, dk), lambda bi, gi, i: (bi, gi, 0, 0))
    vspec = lambda v: pl.BlockSpec((1, 1) + v.shape[2:], lambda bi, gi, i: (bi, gi, 0, 0, 0))
    return pl.pallas_call(
        _nsa_attn_kernel,
        out_shape=jax.ShapeDtypeStruct((b, g, r, dk, t), BF16),
        grid=(b, g, t // tq),
        in_specs=[
            qspec, kspec, vspec(vs_t), kspec, vspec(vw_t),
            pl.BlockSpec((1, 1, ns, tq), lambda bi, gi, i: (bi, gi, 0, i)),
            qspec,
            pl.BlockSpec((1, 1, r, 3, tq), lambda bi, gi, i: (bi, gi, 0, 0, i)),
        ],
        out_specs=qspec,
        scratch_shapes=[pltpu.VMEM((r, 1, tq), F32), pltpu.VMEM((vs_t.shape[3], r * tq), F32)],
        compiler_params=_cp(("arbitrary", "arbitrary", "arbitrary")),
        name="nsa_attention",
    )(q_t, ks, vs_t, kw, vw_t, sel_t, oc_t, gates_t)


def _nsa(u3, q_norm, k_norm, pos_k, pos_v, ck_w1, ck_b1, ck_w2, ck_b2, cv_w1, cv_b1, cv_w2, cv_b2):
    b, t, _ = u3.shape
    n = b * t
    g, r, dk = NSA_G, NSA_R, NSA_HEAD
    tile_g = lambda p: jnp.tile(p.astype(F32), g).reshape(1, g * dk)
    qn, ksn, kwn, gsig = _nsa_prep(u3.reshape(n, IN_COLS), tile_g(q_norm), tile_g(k_norm[1]), tile_g(k_norm[2]))
    per_group = lambda a: a.reshape(b, t, g, dk).transpose(0, 2, 1, 3)
    kv = lambda idx: u3[..., COL_KV + idx * NSA_KV:COL_KV + (idx + 1) * NSA_KV]
    q_t = qn.reshape(b, t, g, r, dk).transpose(0, 2, 3, 4, 1)
    nz = t // CMP_STRIDE
    zk = per_group(kv(0).astype(BF16)).reshape(b, g, nz, CMP_STRIDE * dk)
    zv = per_group(kv(1).astype(BF16)).reshape(b, g, nz, CMP_STRIDE * dk)
    row = lambda p: p.reshape(1, -1).astype(F32)
    kc, vc = _nsa_cmp(zk, zv, row(pos_k), row(pos_v), ck_w1.astype(BF16), row(ck_b1), ck_w2.astype(BF16), row(ck_b2),
                      cv_w1.astype(BF16), row(cv_b1), cv_w2.astype(BF16), row(cv_b2), row(k_norm[0]))
    oc_t, sel_t = _nsa_select(q_t, kc, vc.transpose(0, 1, 3, 2))
    gates_t = gsig[:, :g * r * 3].reshape(b, t, g, r, 3).transpose(0, 2, 3, 4, 1)

    def tiled_t(a, tile):
        v_t = a.astype(BF16).reshape(b, t // tile, tile, g, dk).transpose(0, 3, 1, 4, 2)
        extra = jnp.zeros(v_t.shape[:3] + (16, tile), BF16).at[:, :, :, 0, :].set(1.0)
        return jnp.concatenate([v_t, extra], axis=3)

    o_t = _nsa_attn(q_t, per_group(ksn), tiled_t(kv(3), min(NSA_KT, t)), per_group(kwn), tiled_t(kv(5), min(NSA_KWT, t)),
                    sel_t, oc_t, gates_t)
    return o_t.transpose(0, 4, 1, 2, 3).reshape(b, t, NSA_WIDTH)


def _outproj_kernel(x_ref, yr_ref, yn_ref, w_ref, g_ref, rw_ref, rb_ref,
                    h_ref, xn_ref, e_ref, gate_ref, rank_ref, cnt_ref, cnt_acc):
    tm = x_ref.shape[0]

    @pl.when(pl.program_id(0) == 0)
    def _():
        cnt_acc[...] = jnp.zeros_like(cnt_acc)

    h = x_ref[...] + _dot(yr_ref[...], w_ref[0:RW_WIDTH, :]) + _dot(yn_ref[...], w_ref[RW_WIDTH:, :])
    h_ref[...] = h
    ms = jnp.mean(h * h, axis=-1, keepdims=True)
    xn = h * lax.rsqrt(ms + RMS_EPS) * g_ref[...]
    xn_ref[...] = xn
    logits = _dot(xn, rw_ref[...], HI) + rb_ref[...]
    lane = _iota((tm, LANES), 1).astype(F32)
    work = logits
    top_e, top_v = [], []
    for _ in range(TOP_K):
        mx = jnp.max(work, axis=1, keepdims=True)
        idx = jnp.min(jnp.where(work == mx, lane, float(LANES)), axis=1, keepdims=True)
        top_e.append(idx)
        top_v.append(mx)
        work = jnp.where(lane == idx, -jnp.inf, work)
    ex = [jnp.exp(v - top_v[0]) for v in top_v]
    den = ex[0] + ex[1] + ex[2] + ex[3]
    multihot = jnp.zeros((tm, LANES), F32)
    for e in top_e:
        multihot = multihot + (lane == e).astype(F32)
    strict = (_iota((tm, tm), 1) < _iota((tm, tm), 0)).astype(BF16)
    before = cnt_acc[...] + _dot(strict, multihot.astype(BF16))
    e_out = jnp.zeros((tm, LANES), F32)
    g_out = jnp.zeros((tm, LANES), F32)
    r_out = jnp.zeros((tm, LANES), F32)
    for k in range(TOP_K):
        slot = lane == float(k)
        rank_k = jnp.sum(jnp.where(lane == top_e[k], before, 0.0), axis=1, keepdims=True)
        e_out = jnp.where(slot, top_e[k], e_out)
        g_out = jnp.where(slot, ex[k] / den, g_out)
        r_out = jnp.where(slot, rank_k, r_out)
    e_ref[...] = e_out.astype(I32)
    gate_ref[...] = g_out
    rank_ref[...] = r_out.astype(I32)
    cnt_acc[...] = cnt_acc[...] + jnp.sum(multihot, axis=0, keepdims=True)
    cnt_ref[...] = cnt_acc[...].astype(I32)


def _outproj(x2, y_rw, y_nsa, w_out_bf, g, rw_p, rb_p, tm=256):
    n, d = x2.shape
    rowblk = lambda w: pl.BlockSpec((tm, w), lambda i: (i, 0))
    full = lambda a: pl.BlockSpec(a.shape, lambda i: (0,) * a.ndim)
    return pl.pallas_call(
        _outproj_kernel,
        out_shape=(jax.ShapeDtypeStruct((n, d), F32), jax.ShapeDtypeStruct((n, d), F32),
                   jax.ShapeDtypeStruct((n, LANES), I32), jax.ShapeDtypeStruct((n, LANES), F32),
                   jax.ShapeDtypeStruct((n, LANES), I32), jax.ShapeDtypeStruct((1, LANES), I32)),
        grid=(n // tm,),
        in_specs=[rowblk(d), rowblk(RW_WIDTH), rowblk(NSA_WIDTH), full(w_out_bf), full(g), full(rw_p), full(rb_p)],
        out_specs=(rowblk(d), rowblk(d), rowblk(LANES), rowblk(LANES), rowblk(LANES),
                   pl.BlockSpec((1, LANES), lambda i: (0, 0))),
        scratch_shapes=[pltpu.VMEM((1, LANES), F32)],
        compiler_params=_cp(("arbitrary",)),
        name="outproj_router",
    )(x2, y_rw, y_nsa, w_out_bf, g, rw_p, rb_p)


GATHER_UNROLL = 8
MOE_ROW_SPLIT = 2


def _gather_kernel(row_tok, n_act, x_hbm, o_ref, sem):
    tg = o_ref.shape[0]
    base = pl.program_id(0) * tg

    @pl.when(base < n_act[0])
    def _():
        def issue(r8, c):
            for u in range(GATHER_UNROLL):
                r = r8 * GATHER_UNROLL + u
                pltpu.make_async_copy(x_hbm.at[pl.ds(row_tok[base + r], 1), :], o_ref.at[pl.ds(r, 1), :], sem).start()
            return c

        lax.fori_loop(0, tg // GATHER_UNROLL, issue, 0)
        pltpu.make_async_copy(x_hbm.at[pl.ds(0, tg), :], o_ref, sem).wait()

    @pl.when(base >= n_act[0])
    def _():
        o_ref[...] = jnp.zeros_like(o_ref)


def _gather_rows(row_tok, n_act, xn, n_rows, tg=512):
    d = xn.shape[1]
    return pl.pallas_call(
        _gather_kernel,
        out_shape=jax.ShapeDtypeStruct((n_rows, d), xn.dtype),
        grid_spec=pltpu.PrefetchScalarGridSpec(
            num_scalar_prefetch=2,
            grid=(n_rows // tg,),
            in_specs=[pl.BlockSpec(memory_space=pl.ANY)],
            out_specs=pl.BlockSpec((tg, d), lambda i, rt, na: (i, 0)),
            scratch_shapes=[pltpu.SemaphoreType.DMA],
        ),
        compiler_params=_cp(("arbitrary",)),
        name="moe_gather",
    )(row_tok, n_act, xn)


def _gm1_kernel(e_s, j_s, rt_s, valid_s, first_s, rto_s, jo_s, x_ref, wg_ref, wl_ref, bg_ref, bl_ref, o_ref,
                wgb, wlb):
    s = pl.program_id(0)

    @pl.when(valid_s[s] == 1)
    def _():
        @pl.when(first_s[s] == 1)
        def _():
            wgb[...] = wg_ref[0].astype(BF16)
            wlb[...] = wl_ref[0].astype(BF16)

        half = x_ref.shape[0] // MOE_ROW_SPLIT
        for part in range(MOE_ROW_SPLIT):
            rows = slice(part * half, (part + 1) * half)
            x = x_ref[rows, :].astype(BF16)
            gt = jnp.minimum(_dot(x, wgb[...]) + bg_ref[0], SWIGLU_LIMIT)
            lin = jnp.clip(_dot(x, wlb[...]) + bl_ref[0], -SWIGLU_LIMIT, SWIGLU_LIMIT)
            o_ref[rows, :] = (gt * jax.nn.sigmoid(SWIGLU_ALPHA * gt) * (lin + 1.0)).astype(o_ref.dtype)

    @pl.when(valid_s[s] == 0)
    def _():
        o_ref[...] = jnp.zeros_like(o_ref)


def _gm2_kernel(e_s, j_s, rt_s, valid_s, first_s, rto_s, jo_s, x_ref, w_ref, b_ref, o_ref, wb):
    s = pl.program_id(0)

    @pl.when(valid_s[s] == 1)
    def _():
        @pl.when(first_s[s] == 1)
        def _():
            wb[...] = w_ref[0].astype(BF16)

        half = x_ref.shape[0] // MOE_ROW_SPLIT
        for part in range(MOE_ROW_SPLIT):
            rows = slice(part * half, (part + 1) * half)
            o_ref[rows, :] = _dot(x_ref[rows, :], wb[...]) + b_ref[0]

    @pl.when(valid_s[s] == 0)
    def _():
        o_ref[...] = jnp.zeros_like(o_ref)


def _schedule(counts, tr, n_col_tiles, max_tiles):
    tiles_e = (counts + tr - 1) // tr
    tile_end = jnp.cumsum(tiles_e)
    tile_start = tile_end - tiles_e
    steps_e = tiles_e * n_col_tiles
    step_end = jnp.cumsum(steps_e)
    total = step_end[-1]
    n_steps = max_tiles * n_col_tiles
    step = jnp.arange(n_steps, dtype=I32)
    s = jnp.minimum(step, total - 1)
    e = jnp.sum((step_end[None, :] <= s[:, None]).astype(I32), axis=1)
    local = s - (step_end[e] - steps_e[e])
    te = jnp.maximum(tiles_e[e], 1)
    j = (local // te).astype(I32)
    rt = (tile_start[e] + local % te).astype(I32)
    valid = step < total
    first = ((local % te) == 0).astype(I32)
    spare = step - total
    rt_out = jnp.where(valid, rt, tile_end[-1] + spare // n_col_tiles).astype(I32)
    j_out = jnp.where(valid, j, spare % n_col_tiles).astype(I32)
    return (e, j, rt, valid.astype(I32), first, rt_out, j_out), tile_start * tr, tile_end[-1] * tr


def _grouped_mlp(x_rows, sched1, sched2, w1, b1, w2, b2, tr, tn1, tn2):
    n_rows, d = x_rows.shape
    de = w2.shape[1]
    nt1 = de // tn1
    b1r = b1.reshape(N_EXPERTS, 1, 2 * de)
    b2r = b2.reshape(N_EXPERTS, 1, d)
    act = pl.pallas_call(
        _gm1_kernel,
        out_shape=jax.ShapeDtypeStruct((n_rows, de), BF16),
        grid_spec=pltpu.PrefetchScalarGridSpec(
            num_scalar_prefetch=7,
            grid=(sched1[0].shape[0],),
            in_specs=[
                pl.BlockSpec((tr, d), lambda s, e, j, rt, v, f, ro, jo: (rt[s], 0)),
                pl.BlockSpec((1, d, tn1), lambda s, e, j, rt, v, f, ro, jo: (e[s], 0, j[s])),
                pl.BlockSpec((1, d, tn1), lambda s, e, j, rt, v, f, ro, jo: (e[s], 0, nt1 + j[s])),
                pl.BlockSpec((1, 1, tn1), lambda s, e, j, rt, v, f, ro, jo: (e[s], 0, j[s])),
                pl.BlockSpec((1, 1, tn1), lambda s, e, j, rt, v, f, ro, jo: (e[s], 0, nt1 + j[s])),
            ],
            out_specs=pl.BlockSpec((tr, tn1), lambda s, e, j, rt, v, f, ro, jo: (ro[s], jo[s])),
            scratch_shapes=[pltpu.VMEM((d, tn1), BF16), pltpu.VMEM((d, tn1), BF16)],
        ),
        compiler_params=_cp(("arbitrary",)),
        name="moe_up",
    )(*sched1, x_rows, w1, w1, b1r, b1r)
    return pl.pallas_call(
        _gm2_kernel,
        out_shape=jax.ShapeDtypeStruct((n_rows, d), F32),
        grid_spec=pltpu.PrefetchScalarGridSpec(
            num_scalar_prefetch=7,
            grid=(sched2[0].shape[0],),
            in_specs=[
                pl.BlockSpec((tr, de), lambda s, e, j, rt, v, f, ro, jo: (rt[s], 0)),
                pl.BlockSpec((1, de, tn2), lambda s, e, j, rt, v, f, ro, jo: (e[s], 0, j[s])),
                pl.BlockSpec((1, 1, tn2), lambda s, e, j, rt, v, f, ro, jo: (e[s], 0, j[s])),
            ],
            out_specs=pl.BlockSpec((tr, tn2), lambda s, e, j, rt, v, f, ro, jo: (ro[s], jo[s])),
            scratch_shapes=[pltpu.VMEM((de, tn2), BF16)],
        ),
        compiler_params=_cp(("arbitrary",)),
        name="moe_down",
    )(*sched2, act, w2, b2r)


def _ple_kernel(dest, h_ref, gate_ref, p_ref, y_hbm, g_ref, gw_ref, pw_ref, o_ref, buf, sems):
    tm = h_ref.shape[0]
    i = pl.program_id(0)
    slot = i % 2

    def fetch(step, s):
        def issue(t2, c):
            for u in range(2):
                t = t2 * 2 + u
                for k in range(TOP_K):
                    pltpu.make_async_copy(y_hbm.at[pl.ds(dest[(step * tm + t) * TOP_K + k], 1), :],
                                          buf.at[s, k, pl.ds(t, 1), :], sems.at[s]).start()
            return c

        lax.fori_loop(0, tm // 2, issue, 0)

    @pl.when(i == 0)
    def _():
        fetch(0, 0)

    @pl.when(i + 1 < pl.num_programs(0))
    def _():
        fetch(i + 1, 1 - slot)

    for k in range(TOP_K):
        pltpu.make_async_copy(y_hbm.at[pl.ds(0, tm), :], buf.at[slot, k], sems.at[slot]).wait()
    gate = gate_ref[...]
    h = h_ref[...]
    for k in range(TOP_K):
        h = h + gate[:, k:k + 1] * buf[slot, k]
    ms = jnp.mean(h * h, axis=-1, keepdims=True)
    hn = (h * lax.rsqrt(ms + RMS_EPS) * g_ref[...]).astype(BF16)
    pgate = jax.nn.sigmoid(_dot(hn, gw_ref[...]))
    o_ref[...] = h + pgate * _dot(p_ref[...].astype(BF16), pw_ref[...])


def _combine_ple(dest, h1, gate, p2, y_rows, g, gw_bf, pw_bf, tm=128):
    n, d = h1.shape
    rowblk = lambda w: pl.BlockSpec((tm, w), lambda i, ds: (i, 0))
    full = lambda a: pl.BlockSpec(a.shape, lambda i, ds: (0,) * a.ndim)
    return pl.pallas_call(
        _ple_kernel,
        out_shape=jax.ShapeDtypeStruct((n, d), F32),
        grid_spec=pltpu.PrefetchScalarGridSpec(
            num_scalar_prefetch=1,
            grid=(n // tm,),
            in_specs=[rowblk(d), rowblk(LANES), rowblk(p2.shape[1]), pl.BlockSpec(memory_space=pl.ANY),
                      full(g), full(gw_bf), full(pw_bf)],
            out_specs=rowblk(d),
            scratch_shapes=[pltpu.VMEM((2, TOP_K, tm, d), F32), pltpu.SemaphoreType.DMA((2,))],
        ),
        compiler_params=_cp(("arbitrary",)),
        name="combine_ple",
    )(dest, h1, gate, p2, y_rows, g, gw_bf, pw_bf)


MOE_TR = 512
MOE_TN1 = 512
MOE_TN2 = 1024


def _layer(x, p, mix_norm_g, w_in, rw, nsa, w_out, moe_norm_g, router_w, router_b, moe_w1, moe_b1, moe_w2, moe_b2,
           ple_norm_g, ple_w, ple_gate_w):
    b, t, d = x.shape
    n = b * t
    x2 = x.reshape(n, d)
    row = lambda v: v.reshape(1, -1).astype(F32)
    n_lora = sum(RW_LORA)
    rw_cols = 3 * RW_WIDTH + n_lora
    kv0 = rw_cols + NSA_WIDTH
    n_gate = 3 * NSA_G * NSA_R
    zeros = lambda w: jnp.zeros((d, w), w_in.dtype)
    w_p = jnp.concatenate([
        w_in[:, :3 * RW_WIDTH], w_in[:, rw_cols:kv0], w_in[:, kv0:kv0 + 6 * NSA_KV],
        w_in[:, 3 * RW_WIDTH:rw_cols], zeros(MISC_GATE - n_lora), w_in[:, kv0 + 6 * NSA_KV:],
        zeros(MISC_W - MISC_GATE - n_gate)], axis=1).astype(BF16)
    u = _inproj(x2, row(mix_norm_g), w_p)
    u3 = u.reshape(b, t, IN_COLS)
    y_rw = _rwkv(u3, *_rwkv_params(*rw))
    y_nsa = _nsa(u3, *nsa)
    out = _tail(x2, y_rw.reshape(n, RW_WIDTH), y_nsa.reshape(n, NSA_WIDTH), p.reshape(n, PLE_DIM), w_out, moe_norm_g,
                router_w, router_b, moe_w1, moe_b1, moe_w2, moe_b2, ple_norm_g, ple_w, ple_gate_w)
    return out.reshape(b, t, d)


def _tail(x2, y_rw, y_nsa, p2, w_out, moe_norm_g, router_w, router_b, moe_w1, moe_b1, moe_w2, moe_b2,
          ple_norm_g, ple_w, ple_gate_w):
    n, d = x2.shape
    row = lambda v: v.reshape(1, -1).astype(F32)
    rw_p = jnp.zeros((d, LANES), F32).at[:, :N_EXPERTS].set(router_w)
    rb_p = jnp.full((1, LANES), NEG, F32).at[0, :N_EXPERTS].set(router_b)
    h1, xn, top_e, gate, rank, counts = _outproj(x2, y_rw, y_nsa, w_out.astype(BF16), row(moe_norm_g), rw_p, rb_p)
    counts = counts[0, :N_EXPERTS]
    max_tiles = (n * TOP_K) // MOE_TR + N_EXPERTS
    n_rows = max_tiles * MOE_TR
    sched1, pstart, n_act = _schedule(counts, MOE_TR, D_EXPERT // MOE_TN1, max_tiles)
    sched2, _, _ = _schedule(counts, MOE_TR, d // MOE_TN2, max_tiles)
    top_e = top_e[:, :TOP_K]
    dest = (pstart[top_e] + rank[:, :TOP_K]).astype(I32)
    tok = jnp.broadcast_to(jnp.arange(n, dtype=I32)[:, None], (n, TOP_K))
    row_tok = jnp.zeros((n_rows,), I32).at[dest.reshape(-1)].set(tok.reshape(-1))
    x_rows = _gather_rows(row_tok, n_act.reshape(1).astype(I32), xn, n_rows)
    y_rows = _grouped_mlp(x_rows, sched1, sched2, moe_w1, moe_b1, moe_w2, moe_b2, MOE_TR, MOE_TN1, MOE_TN2)
    return _combine_ple(dest.reshape(-1), h1, gate, p2, y_rows, row(ple_norm_g),
                        ple_gate_w.astype(BF16), ple_w.astype(BF16))


def kernel(x, p, mix_norm_g, w_in, rw_mu, rw_w0, rw_w2, rw_a0, rw_a2, rw_g2, rw_k_k, rw_k_a, rw_r_k, rw_lnx_w,
           rw_lnx_b, nsa_q_norm, nsa_k_norm, cmp_pos_k, cmp_pos_v, cmp_k_w1, cmp_k_b1, cmp_k_w2, cmp_k_b2, cmp_v_w1,
           cmp_v_b1, cmp_v_w2, cmp_v_b2, w_out, moe_norm_g, router_w, router_b, moe_w1, moe_b1, moe_w2, moe_b2,
           ple_norm_g, ple_w, ple_gate_w):
    h = x
    for i in range(mix_norm_g.shape[0]):
        rw = (rw_mu[i], rw_w0[i], rw_w2[i], rw_a0[i], rw_a2[i], rw_g2[i], rw_k_k[i], rw_k_a[i], rw_r_k[i],
              rw_lnx_w[i], rw_lnx_b[i])
        nsa = (nsa_q_norm[i], nsa_k_norm[i], cmp_pos_k[i], cmp_pos_v[i], cmp_k_w1[i], cmp_k_b1[i], cmp_k_w2[i],
               cmp_k_b2[i], cmp_v_w1[i], cmp_v_b1[i], cmp_v_w2[i], cmp_v_b2[i])
        h = _layer(h, p[i], mix_norm_g[i], w_in[i], rw, nsa, w_out[i], moe_norm_g[i], router_w[i], router_b[i],
                   moe_w1[i], moe_b1[i], moe_w2[i], moe_b2[i], ple_norm_g[i], ple_w[i], ple_gate_w[i])
    return h
```

```python
import functools

import jax
import jax.numpy as jnp
from jax import lax
from jax.experimental import pallas as pl
from jax.experimental.pallas import tpu as pltpu

F32 = jnp.float32
BF16 = jnp.bfloat16
I32 = jnp.int32
HI = lax.Precision.HIGHEST

LANES = 128
VMEM_LIMIT = 56 * 1024 * 1024

D_MODEL = 2048
RMS_EPS = 1e-6
RW_WIDTH = 1024
RW_HEAD = 64
RW_LORA = (64, 64, 160)
RW_GN_EPS = 64e-5
NSA_WIDTH = 1024
NSA_HEAD = 64
NSA_G = 4
NSA_R = 4
NSA_KV = NSA_G * NSA_HEAD
CMP_LEN = 32
CMP_STRIDE = 16
CMP_HIDDEN = 256
SLC_BLOCK = 64
N_SELECT = 16
WINDOW = 512
N_EXPERTS = 32
TOP_K = 4
D_EXPERT = 2048
SWIGLU_LIMIT = 7.0
SWIGLU_ALPHA = 1.702
PLE_DIM = 256

COL_RKV = 0
COL_Q = 3072
COL_KV = 4096
COL_MISC = 5632
MISC_W = 512
MISC_GATE = 384
IN_COLS = 6144

RW_HG = 4
RW_LW = RW_HG * RW_HEAD
RW_C = 64
NSA_TQ = 256
NSA_KT = 512
NSA_KWT = 256
NEG = -1e30
LOG2E = 1.4426950408889634


def _cp(sem, vmem=VMEM_LIMIT):
    return pltpu.CompilerParams(dimension_semantics=sem, vmem_limit_bytes=vmem)


def _dot(a, b, prec=None):
    return jnp.dot(a, b, preferred_element_type=F32, precision=prec)


def _dot_nt(a, b, prec=None):
    return lax.dot_general(a, b, (((1,), (1,)), ((), ())), preferred_element_type=F32, precision=prec)


def _dot_split(x, m, pieces=2, left=False):
    out, rem = None, x
    for _ in range(pieces):
        part = rem.astype(BF16)
        rem = rem - part.astype(F32)
        term = _dot(m, part) if left else _dot(part, m)
        out = term if out is None else out + term
    return out


def _iota(shape, dim):
    return lax.broadcasted_iota(I32, shape, dim)


def _seg_matrix(n, seg, scale):
    same = (_iota((n, n), 0) // seg) == (_iota((n, n), 1) // seg)
    return jnp.where(same, scale, 0.0).astype(BF16)


def _inproj_kernel(x_ref, g_ref, w_ref, o_ref, kv_ref, xn_ref, *, kv_lo, kv_hi):
    j = pl.program_id(1)

    @pl.when(j == 0)
    def _():
        x = x_ref[...]
        ms = jnp.mean(x * x, axis=-1, keepdims=True)
        xn_ref[...] = (x * lax.rsqrt(ms + RMS_EPS) * g_ref[...]).astype(BF16)

    u = _dot(xn_ref[...], w_ref[...])
    o_ref[...] = u

    @pl.when((j >= kv_lo) & (j < kv_hi))
    def _():
        kv_ref[...] = u.astype(kv_ref.dtype)


def _inproj(x2, g, w_bf, tm=1024, tn=512):
    n, d = x2.shape
    nc = w_bf.shape[1]
    kv_lo, kv_hi = COL_KV // tn, (COL_KV + 6 * NSA_KV) // tn
    kv_col = lambda i, j: (i, jnp.clip(j, kv_lo, kv_hi - 1) - kv_lo)
    return pl.pallas_call(
        functools.partial(_inproj_kernel, kv_lo=kv_lo, kv_hi=kv_hi),
        out_shape=(jax.ShapeDtypeStruct((n, nc), F32), jax.ShapeDtypeStruct((n, 6 * NSA_KV), BF16)),
        grid=(n // tm, nc // tn),
        in_specs=[
            pl.BlockSpec((tm, d), lambda i, j: (i, 0)),
            pl.BlockSpec((1, d), lambda i, j: (0, 0)),
            pl.BlockSpec((d, tn), lambda i, j: (0, j)),
        ],
        out_specs=(pl.BlockSpec((tm, tn), lambda i, j: (i, j)), pl.BlockSpec((tm, tn), kv_col)),
        scratch_shapes=[pltpu.VMEM((tm, d), BF16)],
        compiler_params=_cp(("arbitrary", "arbitrary")),
        name="inproj",
    )(x2, g, w_bf)


def _softplus(z):
    return jnp.maximum(z, 0.0) + jnp.log(1.0 + jnp.exp(-jnp.abs(z)))


def _rwkv_kernel(r_ref, k_ref, v_ref, m_ref, mur_ref, muk_ref, muv_ref, mum_ref, w2_ref, a2_ref, g2_ref,
                 w0_ref, a0_ref, kk_ref, ka_ref, rk_ref, lw_ref, lb_ref, o_ref,
                 st_ref, pr_ref, pk_ref, pv_ref, pm_ref, *, nchunk):
    tt = nchunk * RW_C
    lw = RW_LW

    @pl.when(pl.program_id(2) == 0)
    def _():
        st_ref[...] = jnp.zeros_like(st_ref)
        pr_ref[...] = jnp.zeros_like(pr_ref)
        pk_ref[...] = jnp.zeros_like(pk_ref)
        pv_ref[...] = jnp.zeros_like(pv_ref)
        pm_ref[...] = jnp.zeros_like(pm_ref)

    def shift_lerp(x, prev_ref, mu):
        rolled = pltpu.roll(x, 1, 0)
        first = _iota(x.shape, 0) == 0
        sh = jnp.where(first, prev_ref[0:1, :], rolled)
        prev_ref[0:1, :] = x[tt - 1:tt, :]
        return x + (sh - x) * mu

    r = shift_lerp(r_ref[0], pr_ref, mur_ref[...])
    k = shift_lerp(k_ref[0], pk_ref, muk_ref[...])
    v = shift_lerp(v_ref[0], pv_ref, muv_ref[...])
    m = shift_lerp(m_ref[0], pm_ref, mum_ref[...])

    lw_dec = _dot(jnp.tanh(m).astype(BF16), w2_ref[...])
    lw_a = _dot(m.astype(BF16), a2_ref[...])
    g = _dot(jax.nn.sigmoid(m).astype(BF16), g2_ref[...])
    w_log = -_softplus(-(w0_ref[...] + lw_dec)) - 0.5
    ld = -jnp.exp(w_log)
    a = jax.nn.sigmoid(a0_ref[...] + lw_a)

    ones_seg = _seg_matrix(lw, RW_HEAD, 1.0)
    kk = k * kk_ref[...]
    kk = kk / jnp.maximum(jnp.sqrt(_dot_split(kk * kk, ones_seg)), 1e-12)
    k2 = k * (1.0 + (a - 1.0) * ka_ref[...])
    av = -kk
    bv = kk * a

    ri = _iota((lw, lw), 0)
    ci = _iota((lw, lw), 1)
    same_blk = (ri // RW_C) == (ci // RW_C)
    m_strict = same_blk & (ci < ri)
    m_lower = same_blk & (ci <= ri)
    eye = (ri == ci).astype(F32)
    tri = (_iota((RW_C, RW_C), 1) <= _iota((RW_C, RW_C), 0)).astype(BF16)
    lane_head = _iota((1, lw), 1) // RW_HEAD
    hmask = [(lane_head == h).astype(F32) for h in range(RW_HG)]

    def stack_heads(x):
        return jnp.concatenate([x * hmask[h] for h in range(RW_HG)], axis=0)

    def unstack_heads(xs):
        out = xs[0:RW_C] * hmask[0]
        for h in range(1, RW_HG):
            out = out + xs[h * RW_C:(h + 1) * RW_C] * hmask[h]
        return out

    def fold_heads(xs):
        out = xs[0:RW_C]
        for h in range(1, RW_HG):
            out = out + xs[h * RW_C:(h + 1) * RW_C]
        return out

    chunks = []
    for c in range(nchunk):
        sl = slice(c * RW_C, (c + 1) * RW_C)
        ld_c = ld[sl]
        cs = _dot_split(ld_c, tri, pieces=3, left=True)
        w_in = jnp.exp(cs)
        w_inv = jnp.exp(-cs)
        w_prev = jnp.exp(cs - ld_c)
        w_last = w_in[RW_C - 1:RW_C, :]
        bt = bv[sl] * w_inv
        kt = k2[sl] * w_inv
        at_s = stack_heads(av[sl] * w_prev)
        rt_s = stack_heads(r[sl] * w_in)
        v_c = v[sl]
        chunks.append(dict(
            at_s=at_s, rt_s=rt_s, at_b=at_s.astype(BF16), rt_b=rt_s.astype(BF16), w_last=w_last,
            bt4=jnp.concatenate([bt] * RW_HG, axis=0).astype(BF16),
            kt4=jnp.concatenate([kt] * RW_HG, axis=0).astype(BF16),
            v4=jnp.concatenate([v_c] * RW_HG, axis=0).astype(BF16),
            bp=(bt * w_last).astype(BF16),
            sv=jnp.where(same_blk, _dot(v_c.T.astype(BF16), (kt * w_last).astype(BF16)), 0.0)))
    for ch in chunks:
        ch["a_ab"] = jnp.where(m_strict, _dot_nt(ch["at_b"], ch["bt4"]), 0.0)
        ch["a_ak"] = jnp.where(m_strict, _dot_nt(ch["at_b"], ch["kt4"]), 0.0).astype(BF16)
        ch["a_rb"] = jnp.where(m_lower, _dot_nt(ch["rt_b"], ch["bt4"]), 0.0).astype(BF16)
        ch["a_rk"] = jnp.where(m_lower, _dot_nt(ch["rt_b"], ch["kt4"]), 0.0).astype(BF16)
        ch["pw"] = ch["a_ab"]
        ch["inv"] = eye + ch["a_ab"]
    for _ in range(RW_C.bit_length() - 2):
        for ch in chunks:
            pw_b = ch["pw"].astype(BF16)
            ch["pw"] = _dot(pw_b, pw_b)
        for ch in chunks:
            ch["inv"] = ch["inv"] + _dot(ch["inv"].astype(BF16), ch["pw"].astype(BF16))
    for ch in chunks:
        inv_b = ch["inv"].astype(BF16)
        ia = _dot(inv_b, ch["at_b"])
        u_const = _dot(inv_b, _dot(ch["a_ak"], ch["v4"]).astype(BF16))
        g_mat = ch["rt_s"] + _dot(ch["a_rb"], ia.astype(BF16))
        y_const = _dot(ch["a_rb"], u_const.astype(BF16)) + _dot(ch["a_rk"], ch["v4"])
        ch["ia_u"] = fold_heads(ia).astype(BF16)
        ch["g_u"] = fold_heads(g_mat).astype(BF16)
        ch["u_const_t"] = unstack_heads(u_const).T
        ch["y_const"] = unstack_heads(y_const)

    ys = []
    st = st_ref[...]
    for ch in chunks:
        st_b = st.astype(BF16)
        ys.append(_dot_nt(ch["g_u"], st_b) + ch["y_const"])
        u_t = _dot_nt(st_b, ch["ia_u"]) + ch["u_const_t"]
        st = st * ch["w_last"] + jnp.where(same_blk, _dot(u_t.astype(BF16), ch["bp"]), 0.0) + ch["sv"]
    st_ref[...] = st
    y = jnp.concatenate(ys, axis=0) if nchunk > 1 else ys[0]

    avg_seg = _seg_matrix(lw, RW_HEAD, 1.0 / RW_HEAD)
    mean = _dot_split(y, avg_seg)
    yc = y - mean
    var = _dot_split(yc * yc, avg_seg)
    yn = yc * lax.rsqrt(var + RW_GN_EPS) * lw_ref[...] + lb_ref[...]
    bonus = _dot_split(r * k2 * rk_ref[...], ones_seg) * v
    o_ref[0] = ((yn + bonus) * g).astype(o_ref.dtype)


def _rwkv(u3, mu_p, w2p, a2p, g2p, w0, a0, k_k, k_a, r_k, lnx_w, lnx_b, nchunk=4):
    b, t, _ = u3.shape
    tt = nchunk * RW_C
    lw = RW_LW
    nhg = RW_WIDTH // lw
    col = lambda base: (lambda bi, h, ti: (bi, ti, base + h))
    vec = lambda base: (lambda bi, h, ti: (0, base + h))
    par = pl.BlockSpec((1, lw), vec(0))
    return pl.pallas_call(
        functools.partial(_rwkv_kernel, nchunk=nchunk),
        out_shape=jax.ShapeDtypeStruct((b, t, RW_WIDTH), BF16),
        grid=(b, nhg, t // tt),
        in_specs=[
            pl.BlockSpec((1, tt, lw), col(0)),
            pl.BlockSpec((1, tt, lw), col(nhg)),
            pl.BlockSpec((1, tt, lw), col(2 * nhg)),
            pl.BlockSpec((1, tt, MISC_W), lambda bi, h, ti: (bi, ti, COL_MISC // MISC_W)),
            pl.BlockSpec((1, lw), vec(0)),
            pl.BlockSpec((1, lw), vec(nhg)),
            pl.BlockSpec((1, lw), vec(2 * nhg)),
            pl.BlockSpec((1, MISC_W), lambda bi, h, ti: (0, COL_MISC // MISC_W)),
            pl.BlockSpec((MISC_W, lw), lambda bi, h, ti: (0, h)),
            pl.BlockSpec((MISC_W, lw), lambda bi, h, ti: (0, h)),
            pl.BlockSpec((MISC_W, lw), lambda bi, h, ti: (0, h)),
            par, par, par, par, par, par, par,
        ],
        out_specs=pl.BlockSpec((1, tt, lw), lambda bi, h, ti: (bi, ti, h)),
        scratch_shapes=[
            pltpu.VMEM((lw, lw), F32),
            pltpu.VMEM((8, lw), F32),
            pltpu.VMEM((8, lw), F32),
            pltpu.VMEM((8, lw), F32),
            pltpu.VMEM((8, MISC_W), F32),
        ],
        compiler_params=_cp(("arbitrary", "arbitrary", "arbitrary")),
        name="rwkv7",
    )(u3, u3, u3, u3, mu_p, mu_p, mu_p, mu_p, w2p, a2p, g2p, w0, a0, k_k, k_a, r_k, lnx_w, lnx_b)


def _pad_rows(w, rows, offset):
    return jnp.zeros((rows, w.shape[1]), w.dtype).at[offset:offset + w.shape[0]].set(w)


def _rwkv_params(mu, w0, w2, a0, a2, g2, k_k, k_a, r_k, lnx_w, lnx_b):
    mu_p = jnp.zeros((1, IN_COLS), F32)
    mu_p = mu_p.at[0, COL_RKV:COL_RKV + 3 * RW_WIDTH].set(mu[:3 * RW_WIDTH])
    mu_p = mu_p.at[0, COL_MISC:COL_MISC + sum(RW_LORA)].set(mu[3 * RW_WIDTH:])
    o1, o2 = RW_LORA[0], RW_LORA[0] + RW_LORA[1]
    row = lambda p: p.reshape(1, RW_WIDTH).astype(F32)
    return (mu_p, _pad_rows(w2, MISC_W, 0).astype(BF16), _pad_rows(a2, MISC_W, o1).astype(BF16),
            _pad_rows(g2, MISC_W, o2).astype(BF16), row(w0), row(a0), row(k_k), row(k_a), row(r_k),
            row(lnx_w), row(lnx_b))


def _nsa_prep_kernel(q_ref, ks_ref, kw_ref, gt_ref, qg_ref, ksg_ref, kwg_ref, qo_ref, kso_ref, kwo_ref, go_ref):
    avg = _seg_matrix(NSA_KV, NSA_HEAD, 1.0 / NSA_HEAD)

    def head_norm(x, g):
        ms = _dot_split(x * x, avg)
        return x * lax.rsqrt(ms + RMS_EPS) * g

    scale = NSA_HEAD ** -0.5 * LOG2E
    for s in range(NSA_WIDTH // NSA_KV):
        sl = slice(s * NSA_KV, (s + 1) * NSA_KV)
        qo_ref[:, sl] = (head_norm(q_ref[:, sl], qg_ref[...]) * scale).astype(qo_ref.dtype)
    kso_ref[...] = head_norm(ks_ref[...], ksg_ref[...]).astype(kso_ref.dtype)
    kwo_ref[...] = head_norm(kw_ref[...], kwg_ref[...]).astype(kwo_ref.dtype)
    go_ref[...] = jax.nn.sigmoid(gt_ref[...])


def _nsa_prep(u2, q_gain, ks_gain, kw_gain, tm=512):
    n = u2.shape[0]
    kvb = COL_KV // NSA_KV
    gain = pl.BlockSpec((1, NSA_KV), lambda i: (0, 0))
    return pl.pallas_call(
        _nsa_prep_kernel,
        out_shape=(jax.ShapeDtypeStruct((n, NSA_WIDTH), BF16), jax.ShapeDtypeStruct((n, NSA_KV), BF16),
                   jax.ShapeDtypeStruct((n, NSA_KV), BF16), jax.ShapeDtypeStruct((n, LANES), F32)),
        grid=(n // tm,),
        in_specs=[
            pl.BlockSpec((tm, NSA_WIDTH), lambda i: (i, COL_Q // NSA_WIDTH)),
            pl.BlockSpec((tm, NSA_KV), lambda i: (i, kvb + 2)),
            pl.BlockSpec((tm, NSA_KV), lambda i: (i, kvb + 4)),
            pl.BlockSpec((tm, LANES), lambda i: (i, (COL_MISC + MISC_GATE) // LANES)),
            gain, gain, gain,
        ],
        out_specs=(pl.BlockSpec((tm, NSA_WIDTH), lambda i: (i, 0)), pl.BlockSpec((tm, NSA_KV), lambda i: (i, 0)),
                   pl.BlockSpec((tm, NSA_KV), lambda i: (i, 0)), pl.BlockSpec((tm, LANES), lambda i: (i, 0))),
        compiler_params=_cp(("arbitrary",)),
        name="nsa_prep",
    )(u2, u2, u2, u2, q_gain, ks_gain, kw_gain)


def _nsa_cmp_kernel(zk_ref, zv_ref, pk_ref, pv_ref, kw1_ref, kb1_ref, kw2_ref, kb2_ref, vw1_ref, vb1_ref,
                    vw2_ref, vb2_ref, kn_ref, ko_ref, vo_ref):
    nz = zk_ref.shape[2]
    half = kw1_ref.shape[0] // 2

    def compress(z, pos, w1_ref, b1, w2, b2):
        top = _dot(z, w1_ref[0:half, :])
        bot = _dot(z, w1_ref[half:, :])
        c1 = _dot(jnp.broadcast_to(pos, (8, pos.shape[1])).astype(BF16), w1_ref[...])[0:1] + b1
        hid = jax.nn.gelu(top + pltpu.roll(bot, nz - 1, 0) + c1)
        return _dot(hid.astype(BF16), w2) + b2

    kc = compress(zk_ref[0, 0], pk_ref[...], kw1_ref, kb1_ref[...], kw2_ref[...], kb2_ref[...])
    ms = jnp.mean(kc * kc, axis=-1, keepdims=True)
    ko_ref[0, 0] = (kc * lax.rsqrt(ms + RMS_EPS) * kn_ref[...]).astype(ko_ref.dtype)
    vc = compress(zv_ref[0, 0], pv_ref[...], vw1_ref, vb1_ref[...], vw2_ref[...], vb2_ref[...])
    vo_ref[0, 0] = vc.astype(vo_ref.dtype)


def _nsa_cmp(zk, zv, pos_k, pos_v, kw1, kb1, kw2, kb2, vw1, vb1, vw2, vb2, kn0):
    b, g, nz, zw = zk.shape
    full = lambda a: pl.BlockSpec(a.shape, lambda bi, gi: (0,) * a.ndim)
    zspec = pl.BlockSpec((1, 1, nz, zw), lambda bi, gi: (bi, gi, 0, 0))
    ospec = pl.BlockSpec((1, 1, nz, NSA_HEAD), lambda bi, gi: (bi, gi, 0, 0))
    params = (pos_k, pos_v, kw1, kb1, kw2, kb2, vw1, vb1, vw2, vb2, kn0)
    return pl.pallas_call(
        _nsa_cmp_kernel,
        out_shape=(jax.ShapeDtypeStruct((b, g, nz, NSA_HEAD), BF16),) * 2,
        grid=(b, g),
        in_specs=[zspec, zspec] + [full(a) for a in params],
        out_specs=(ospec, ospec),
        compiler_params=_cp(("arbitrary", "arbitrary")),
        name="nsa_compress",
    )(zk, zv, *params)


def _nsa_select_kernel(q_ref, kc_ref, vc_ref, oc_ref, sel_ref, *, n_sel):
    tq = q_ref.shape[4]
    nz = kc_ref.shape[2]
    ns = sel_ref.shape[2]
    t0 = pl.program_id(2) * tq
    n_idx = _iota((nz, tq), 0)
    t_idx = t0 + _iota((nz, tq), 1)
    cmask = (n_idx * CMP_STRIDE + (CMP_LEN - 1) <= t_idx) & (n_idx < nz - 1)
    kc = kc_ref[0, 0]
    vc_t = vc_ref[0, 0]
    psum = jnp.zeros((nz, tq), F32)
    for r in range(NSA_R):
        s = jnp.where(cmask, _dot(kc, q_ref[0, 0, r]), -jnp.inf)
        mx = jnp.max(s, axis=0, keepdims=True)
        e = jnp.exp2(s - jnp.where(mx == -jnp.inf, 0.0, mx))
        p = e / jnp.maximum(jnp.sum(e, axis=0, keepdims=True), 1e-30)
        psum = psum + p
        oc_ref[0, 0, r] = _dot(vc_t, p.astype(BF16))
    jb = _iota((ns, nz), 0) * SLC_BLOCK
    cb = _iota((ns, nz), 1) * CMP_STRIDE
    overlap = ((cb < jb + SLC_BLOCK) & (cb + CMP_LEN > jb) & (_iota((ns, nz), 1) < nz - 1)).astype(BF16)
    imp = _dot_split(psum, overlap, pieces=3, left=True)
    j_idx = _iota((ns, tq), 0)
    t_q = t0 + _iota((ns, tq), 1)
    qblk = t_q // SLC_BLOCK
    forced = (j_idx == 0) | (j_idx == qblk) | (j_idx == qblk - 1)
    imp = jnp.where(forced, jnp.inf, imp)
    imp = jnp.where(j_idx * SLC_BLOCK <= t_q, imp, -jnp.inf)
    rank = jnp.zeros((ns, tq), F32)
    for i in range(ns):
        row = imp[i:i + 1, :]
        before = (row > imp) | ((row == imp) & (j_idx > i))
        rank = rank + before.astype(F32)
    sel_ref[0, 0] = ((rank < n_sel) & (imp > -jnp.inf)).astype(F32)


def _nsa_select(q_t, kc, vc_t, tq=NSA_TQ):
    b, g, r, dk, t = q_t.shape
    nz = kc.shape[2]
    ns = t // SLC_BLOCK
    qspec = pl.BlockSpec((1, 1, r, dk, tq), lambda bi, gi, i: (bi, gi, 0, 0, i))
    return pl.pallas_call(
        functools.partial(_nsa_select_kernel, n_sel=min(N_SELECT, ns)),
        out_shape=(jax.ShapeDtypeStruct((b, g, r, dk, t), F32), jax.ShapeDtypeStruct((b, g, ns, t), F32)),
        grid=(b, g, t // tq),
        in_specs=[
            qspec,
            pl.BlockSpec((1, 1, nz, dk), lambda bi, gi, i: (bi, gi, 0, 0)),
            pl.BlockSpec((1, 1, dk, nz), lambda bi, gi, i: (bi, gi, 0, 0)),
        ],
        out_specs=(qspec, pl.BlockSpec((1, 1, ns, tq), lambda bi, gi, i: (bi, gi, 0, i))),
        compiler_params=_cp(("arbitrary", "arbitrary", "arbitrary")),
        name="nsa_select",
    )(q_t, kc, vc_t)


def _nsa_attn_kernel(q_ref, ks_ref, vs_ref, kw_ref, vw_ref, sel_ref, oc_ref, gt_ref, o_ref, m_ref, acc_ref):
    r, dk, tq = q_ref.shape[2:]
    kt = vs_ref.shape[4]
    kwt = vw_ref.shape[4]
    t0 = pl.program_id(2) * tq
    q_t = jnp.concatenate([q_ref[0, 0, h] for h in range(r)], axis=1)
    tpos = t0 + _iota((1, tq), 1)

    def reset():
        m_ref[...] = jnp.full_like(m_ref, NEG)
        acc_ref[...] = jnp.zeros_like(acc_ref)

    def online_update(s, bias, v_aug):
        ps, alphas = [], []
        for h in range(r):
            s_h = s[:, h * tq:(h + 1) * tq] + bias
            m_old = m_ref[h]
            m_new = jnp.maximum(m_old, jnp.max(s_h, axis=0, keepdims=True))
            alphas.append(jnp.exp2(m_old - m_new))
            ps.append(jnp.exp2((s_h - m_new).astype(BF16)))
            m_ref[h] = m_new
        acc_ref[...] = jnp.concatenate(alphas, axis=1) * acc_ref[...] + _dot(v_aug, jnp.concatenate(ps, axis=1))

    def result():
        seen = jnp.concatenate([m_ref[h] for h in range(r)], axis=1) > 0.5 * NEG
        acc = acc_ref[...]
        return jnp.where(seen, acc[0:dk] / jnp.maximum(acc[dk:dk + 1], 1e-30), 0.0)

    reset()
    nb = kt // SLC_BLOCK

    def sel_body(j, carry):
        k = ks_ref[0, 0, pl.ds(pl.multiple_of(j * kt, kt), kt), :]
        s = _dot(k, q_t)
        bm = jnp.concatenate([jnp.broadcast_to(sel_ref[0, 0, pl.ds(j * nb + c, 1), :], (SLC_BLOCK, tq))
                              for c in range(nb)], axis=0)
        kpos = j * kt + _iota((kt, tq), 0)
        online_update(s, jnp.where((bm > 0.5) & (kpos <= tpos), 0.0, NEG), vs_ref[0, 0, j])
        return carry

    lax.fori_loop(0, (t0 + tq + kt - 1) // kt, sel_body, 0)
    o_s = result()

    reset()

    def win_body(j, carry):
        k = kw_ref[0, 0, pl.ds(pl.multiple_of(j * kwt, kwt), kwt), :]
        s = _dot(k, q_t)
        kpos = j * kwt + _iota((kwt, tq), 0)
        online_update(s, jnp.where((kpos <= tpos) & (kpos > tpos - WINDOW), 0.0, NEG), vw_ref[0, 0, j])
        return carry

    lo = jnp.maximum(t0 - (WINDOW - 1), 0) // kwt
    lax.fori_loop(lo, (t0 + tq - 1) // kwt + 1, win_body, 0)
    o_w = result()

    for h in range(r):
        sl = slice(h * tq, (h + 1) * tq)
        gt = gt_ref[0, 0, h]
        out = gt[0:1] * oc_ref[0, 0, h] + gt[1:2] * o_s[:, sl] + gt[2:3] * o_w[:, sl]
        o_ref[0, 0, h] = out.astype(o_ref.dtype)


def _nsa_attn(q_t, ks, vs_t, kw, vw_t, sel_t, oc_t, gates_t, tq=NSA_TQ):
    b, g, r, dk, t = q_t.shape
    ns = sel_t.shape[2]
    qspec = pl.BlockSpec((1, 1, r, dk, tq), lambda bi, gi, i: (bi, gi, 0, 0, i))
    kspec = pl.BlockSpec((1, 1, t, dk), lambda bi, gi, i: (bi, gi, 0, 0))
    vspec = lambda v: pl.BlockSpec((1, 1) + v.shape[2:], lambda bi, gi, i: (bi, gi, 0, 0, 0))
    return pl.pallas_call(
        _nsa_attn_kernel,
        out_shape=jax.ShapeDtypeStruct((b, g, r, dk, t), BF16),
        grid=(b, g, t // tq),
        in_specs=[
            qspec, kspec, vspec(vs_t), kspec, vspec(vw_t),
            pl.BlockSpec((1, 1, ns, tq), lambda bi, gi, i: (bi, gi, 0, i)),
            qspec,
            pl.BlockSpec((1, 1, r, 3, tq), lambda bi, gi, i: (bi, gi, 0, 0, i)),
        ],
        out_specs=qspec,
        scratch_shapes=[pltpu.VMEM((r, 1, tq), F32), pltpu.VMEM((vs_t.shape[3], r * tq), F32)],
        compiler_params=_cp(("arbitrary", "arbitrary", "arbitrary")),
        name="nsa_attention",
    )(q_t, ks, vs_t, kw, vw_t, sel_t, oc_t, gates_t)


def _nsa(u3, kv_bf, q_norm, k_norm, pos_k, pos_v, ck_w1, ck_b1, ck_w2, ck_b2, cv_w1, cv_b1, cv_w2, cv_b2):
    b, t, _ = u3.shape
    n = b * t
    g, r, dk = NSA_G, NSA_R, NSA_HEAD
    tile_g = lambda p: jnp.tile(p.astype(F32), g).reshape(1, g * dk)
    qn, ksn, kwn, gsig = _nsa_prep(u3.reshape(n, IN_COLS), tile_g(q_norm), tile_g(k_norm[1]), tile_g(k_norm[2]))
    per_group = lambda a: a.reshape(b, t, g, dk).transpose(0, 2, 1, 3)
    kv = lambda idx: kv_bf[..., idx * NSA_KV:(idx + 1) * NSA_KV]
    q_t = qn.reshape(b, t, g, r, dk).transpose(0, 2, 3, 4, 1)
    nz = t // CMP_STRIDE
    zk = per_group(kv(0)).reshape(b, g, nz, CMP_STRIDE * dk)
    zv = per_group(kv(1)).reshape(b, g, nz, CMP_STRIDE * dk)
    row = lambda p: p.reshape(1, -1).astype(F32)
    kc, vc = _nsa_cmp(zk, zv, row(pos_k), row(pos_v), ck_w1.astype(BF16), row(ck_b1), ck_w2.astype(BF16), row(ck_b2),
                      cv_w1.astype(BF16), row(cv_b1), cv_w2.astype(BF16), row(cv_b2), row(k_norm[0]))
    oc_t, sel_t = _nsa_select(q_t, kc, vc.transpose(0, 1, 3, 2))
    gates_t = gsig[:, :g * r * 3].reshape(b, t, g, r, 3).transpose(0, 2, 3, 4, 1)

    def tiled_t(a, tile):
        v_t = a.reshape(b, t // tile, tile, g, dk).transpose(0, 3, 1, 4, 2)
        extra = jnp.zeros(v_t.shape[:3] + (16, tile), BF16).at[:, :, :, 0, :].set(1.0)
        return jnp.concatenate([v_t, extra], axis=3)

    o_t = _nsa_attn(q_t, per_group(ksn), tiled_t(kv(3), min(NSA_KT, t)), per_group(kwn), tiled_t(kv(5), min(NSA_KWT, t)),
                    sel_t, oc_t, gates_t)
    return o_t.transpose(0, 4, 1, 2, 3).reshape(b, t, NSA_WIDTH)


def _outproj_kernel(x_ref, yr_ref, yn_ref, w_ref, g_ref, rw_ref, rb_ref,
                    h_ref, xn_ref, e_ref, gate_ref, rank_ref, cnt_ref, cnt_acc):
    tm = x_ref.shape[0]

    @pl.when(pl.program_id(0) == 0)
    def _():
        cnt_acc[...] = jnp.zeros_like(cnt_acc)

    h = x_ref[...] + _dot(yr_ref[...], w_ref[0:RW_WIDTH, :]) + _dot(yn_ref[...], w_ref[RW_WIDTH:, :])
    h_ref[...] = h
    ms = jnp.mean(h * h, axis=-1, keepdims=True)
    xn = h * lax.rsqrt(ms + RMS_EPS) * g_ref[...]
    xn_ref[...] = xn
    xn_hi = xn.astype(BF16)
    xn_lo = (xn - xn_hi.astype(F32)).astype(BF16)
    logits = (_dot(xn_hi, rw_ref[0]) + _dot(xn_lo, rw_ref[0]) + _dot(xn_hi, rw_ref[1])
              + rb_ref[...])
    lane = _iota((tm, LANES), 1).astype(F32)
    work = logits
    top_e, top_v = [], []
    for _ in range(TOP_K):
        mx = jnp.max(work, axis=1, keepdims=True)
        idx = jnp.min(jnp.where(work == mx, lane, float(LANES)), axis=1, keepdims=True)
        top_e.append(idx)
        top_v.append(mx)
        work = jnp.where(lane == idx, -jnp.inf, work)
    ex = [jnp.exp(v - top_v[0]) for v in top_v]
    den = ex[0] + ex[1] + ex[2] + ex[3]
    multihot = jnp.zeros((tm, LANES), F32)
    for e in top_e:
        multihot = multihot + (lane == e).astype(F32)
    strict = (_iota((tm, tm), 1) < _iota((tm, tm), 0)).astype(BF16)
    before = cnt_acc[...] + _dot(strict, multihot.astype(BF16))
    e_out = jnp.zeros((tm, LANES), F32)
    g_out = jnp.zeros((tm, LANES), F32)
    r_out = jnp.zeros((tm, LANES), F32)
    for k in range(TOP_K):
        slot = lane == float(k)
        rank_k = jnp.sum(jnp.where(lane == top_e[k], before, 0.0), axis=1, keepdims=True)
        e_out = jnp.where(slot, top_e[k], e_out)
        g_out = jnp.where(slot, ex[k] / den, g_out)
        r_out = jnp.where(slot, rank_k, r_out)
    e_ref[...] = e_out.astype(I32)
    gate_ref[...] = g_out
    rank_ref[...] = r_out.astype(I32)
    cnt_acc[...] = cnt_acc[...] + jnp.sum(multihot, axis=0, keepdims=True)
    cnt_ref[...] = cnt_acc[...].astype(I32)


def _outproj(x2, y_rw, y_nsa, w_out_bf, g, rw_p, rb_p, tm=256):
    n, d = x2.shape
    rowblk = lambda w: pl.BlockSpec((tm, w), lambda i: (i, 0))
    full = lambda a: pl.BlockSpec(a.shape, lambda i: (0,) * a.ndim)
    return pl.pallas_call(
        _outproj_kernel,
        out_shape=(jax.ShapeDtypeStruct((n, d), F32), jax.ShapeDtypeStruct((n, d), F32),
                   jax.ShapeDtypeStruct((n, LANES), I32), jax.ShapeDtypeStruct((n, LANES), F32),
                   jax.ShapeDtypeStruct((n, LANES), I32), jax.ShapeDtypeStruct((1, LANES), I32)),
        grid=(n // tm,),
        in_specs=[rowblk(d), rowblk(RW_WIDTH), rowblk(NSA_WIDTH), full(w_out_bf), full(g), full(rw_p), full(rb_p)],
        out_specs=(rowblk(d), rowblk(d), rowblk(LANES), rowblk(LANES), rowblk(LANES),
                   pl.BlockSpec((1, LANES), lambda i: (0, 0))),
        scratch_shapes=[pltpu.VMEM((1, LANES), F32)],
        compiler_params=_cp(("arbitrary",)),
        name="outproj_router",
    )(x2, y_rw, y_nsa, w_out_bf, g, rw_p, rb_p)


GATHER_UNROLL = 8
MOE_ROW_SPLIT = 2


def _gather_kernel(row_tok, n_act, x_hbm, o_ref, buf, sem):
    tg = o_ref.shape[0]
    base = pl.program_id(0) * tg

    @pl.when(base < n_act[0])
    def _():
        def issue(r8, c):
            for u in range(GATHER_UNROLL):
                r = r8 * GATHER_UNROLL + u
                pltpu.make_async_copy(x_hbm.at[pl.ds(row_tok[base + r], 1), :], buf.at[pl.ds(r, 1), :], sem).start()
            return c

        lax.fori_loop(0, tg // GATHER_UNROLL, issue, 0)
        pltpu.make_async_copy(x_hbm.at[pl.ds(0, tg), :], buf, sem).wait()
        o_ref[...] = buf[...].astype(o_ref.dtype)

    @pl.when(base >= n_act[0])
    def _():
        o_ref[...] = jnp.zeros_like(o_ref)


def _gather_rows(row_tok, n_act, xn, n_rows, tg=512):
    d = xn.shape[1]
    return pl.pallas_call(
        _gather_kernel,
        out_shape=jax.ShapeDtypeStruct((n_rows, d), BF16),
        grid_spec=pltpu.PrefetchScalarGridSpec(
            num_scalar_prefetch=2,
            grid=(n_rows // tg,),
            in_specs=[pl.BlockSpec(memory_space=pl.ANY)],
            out_specs=pl.BlockSpec((tg, d), lambda i, rt, na: (i, 0)),
            scratch_shapes=[pltpu.VMEM((tg, d), xn.dtype), pltpu.SemaphoreType.DMA],
        ),
        compiler_params=_cp(("arbitrary",)),
        name="moe_gather",
    )(row_tok, n_act, xn)


def _gm1_kernel(e_s, j_s, rt_s, valid_s, first_s, rto_s, jo_s, x_ref, wg_ref, wl_ref, bg_ref, bl_ref, o_ref,
                wgb, wlb):
    s = pl.program_id(0)

    @pl.when(valid_s[s] == 1)
    def _():
        @pl.when(first_s[s] == 1)
        def _():
            wgb[...] = wg_ref[0].astype(BF16)
            wlb[...] = wl_ref[0].astype(BF16)

        half = x_ref.shape[0] // MOE_ROW_SPLIT
        for part in range(MOE_ROW_SPLIT):
            rows = slice(part * half, (part + 1) * half)
            x = x_ref[rows, :]
            gt = jnp.minimum(_dot(x, wgb[...]) + bg_ref[0], SWIGLU_LIMIT)
            lin = jnp.clip(_dot(x, wlb[...]) + bl_ref[0], -SWIGLU_LIMIT, SWIGLU_LIMIT)
            o_ref[rows, :] = (gt * jax.nn.sigmoid(SWIGLU_ALPHA * gt) * (lin + 1.0)).astype(o_ref.dtype)

    @pl.when(valid_s[s] == 0)
    def _():
        o_ref[...] = jnp.zeros_like(o_ref)


def _gm2_kernel(e_s, j_s, rt_s, valid_s, first_s, rto_s, jo_s, x_ref, w_ref, b_ref, o_ref, wb):
    s = pl.program_id(0)

    @pl.when(valid_s[s] == 1)
    def _():
        @pl.when(first_s[s] == 1)
        def _():
            wb[...] = w_ref[0].astype(BF16)

        half = x_ref.shape[0] // MOE_ROW_SPLIT
        for part in range(MOE_ROW_SPLIT):
            rows = slice(part * half, (part + 1) * half)
            o_ref[rows, :] = _dot(x_ref[rows, :], wb[...]) + b_ref[0]

    @pl.when(valid_s[s] == 0)
    def _():
        o_ref[...] = jnp.zeros_like(o_ref)


def _schedule(counts, tr, n_col_tiles, max_tiles):
    tiles_e = (counts + tr - 1) // tr
    tile_end = jnp.cumsum(tiles_e)
    tile_start = tile_end - tiles_e
    steps_e = tiles_e * n_col_tiles
    step_end = jnp.cumsum(steps_e)
    total = step_end[-1]
    n_steps = max_tiles * n_col_tiles
    step = jnp.arange(n_steps, dtype=I32)
    s = jnp.minimum(step, total - 1)
    e = jnp.sum((step_end[None, :] <= s[:, None]).astype(I32), axis=1)
    own = (jnp.arange(N_EXPERTS, dtype=I32)[None, :] == e[:, None]).astype(I32)
    of_e = lambda per_expert: jnp.sum(own * per_expert[None, :], axis=1)
    local = s - of_e(step_end - steps_e)
    te = jnp.maximum(of_e(tiles_e), 1)
    j = (local // te).astype(I32)
    rt = (of_e(tile_start) + local % te).astype(I32)
    valid = step < total
    first = ((local % te) == 0).astype(I32)
    spare = step - total
    rt_out = jnp.where(valid, rt, tile_end[-1] + spare // n_col_tiles).astype(I32)
    j_out = jnp.where(valid, j, spare % n_col_tiles).astype(I32)
    return (e, j, rt, valid.astype(I32), first, rt_out, j_out), tile_start * tr, tile_end[-1] * tr


def _grouped_mlp(x_rows, sched1, sched2, w1, b1, w2, b2, tr, tn1, tn2):
    n_rows, d = x_rows.shape
    de = w2.shape[1]
    nt1 = de // tn1
    b1r = b1.reshape(N_EXPERTS, 1, 2 * de)
    b2r = b2.reshape(N_EXPERTS, 1, d)
    act = pl.pallas_call(
        _gm1_kernel,
        out_shape=jax.ShapeDtypeStruct((n_rows, de), BF16),
        grid_spec=pltpu.PrefetchScalarGridSpec(
            num_scalar_prefetch=7,
            grid=(sched1[0].shape[0],),
            in_specs=[
                pl.BlockSpec((tr, d), lambda s, e, j, rt, v, f, ro, jo: (rt[s], 0)),
                pl.BlockSpec((1, d, tn1), lambda s, e, j, rt, v, f, ro, jo: (e[s], 0, j[s])),
                pl.BlockSpec((1, d, tn1), lambda s, e, j, rt, v, f, ro, jo: (e[s], 0, nt1 + j[s])),
                pl.BlockSpec((1, 1, tn1), lambda s, e, j, rt, v, f, ro, jo: (e[s], 0, j[s])),
                pl.BlockSpec((1, 1, tn1), lambda s, e, j, rt, v, f, ro, jo: (e[s], 0, nt1 + j[s])),
            ],
            out_specs=pl.BlockSpec((tr, tn1), lambda s, e, j, rt, v, f, ro, jo: (ro[s], jo[s])),
            scratch_shapes=[pltpu.VMEM((d, tn1), BF16), pltpu.VMEM((d, tn1), BF16)],
        ),
        compiler_params=_cp(("arbitrary",)),
        name="moe_up",
    )(*sched1, x_rows, w1, w1, b1r, b1r)
    return pl.pallas_call(
        _gm2_kernel,
        out_shape=jax.ShapeDtypeStruct((n_rows, d), F32),
        grid_spec=pltpu.PrefetchScalarGridSpec(
            num_scalar_prefetch=7,
            grid=(sched2[0].shape[0],),
            in_specs=[
                pl.BlockSpec((tr, de), lambda s, e, j, rt, v, f, ro, jo: (rt[s], 0)),
                pl.BlockSpec((1, de, tn2), lambda s, e, j, rt, v, f, ro, jo: (e[s], 0, j[s])),
                pl.BlockSpec((1, 1, tn2), lambda s, e, j, rt, v, f, ro, jo: (e[s], 0, j[s])),
            ],
            out_specs=pl.BlockSpec((tr, tn2), lambda s, e, j, rt, v, f, ro, jo: (ro[s], jo[s])),
            scratch_shapes=[pltpu.VMEM((de, tn2), BF16)],
        ),
        compiler_params=_cp(("arbitrary",)),
        name="moe_down",
    )(*sched2, act, w2, b2r)


def _ple_kernel(dest, h_ref, gate_ref, p_ref, y_hbm, g_ref, gw_ref, pw_ref, o_ref, buf, sems):
    tm = h_ref.shape[0]
    i = pl.program_id(0)
    slot = i % 2

    def fetch(step, s):
        def issue(t2, c):
            for u in range(2):
                t = t2 * 2 + u
                for k in range(TOP_K):
                    pltpu.make_async_copy(y_hbm.at[pl.ds(dest[(step * tm + t) * TOP_K + k], 1), :],
                                          buf.at[s, k, pl.ds(t, 1), :], sems.at[s]).start()
            return c

        lax.fori_loop(0, tm // 2, issue, 0)

    @pl.when(i == 0)
    def _():
        fetch(0, 0)

    @pl.when(i + 1 < pl.num_programs(0))
    def _():
        fetch(i + 1, 1 - slot)

    for k in range(TOP_K):
        pltpu.make_async_copy(y_hbm.at[pl.ds(0, tm), :], buf.at[slot, k], sems.at[slot]).wait()
    gate = gate_ref[...]
    h = h_ref[...]
    for k in range(TOP_K):
        h = h + gate[:, k:k + 1] * buf[slot, k]
    ms = jnp.mean(h * h, axis=-1, keepdims=True)
    hn = (h * lax.rsqrt(ms + RMS_EPS) * g_ref[...]).astype(BF16)
    pgate = jax.nn.sigmoid(_dot(hn, gw_ref[...]))
    o_ref[...] = h + pgate * _dot(p_ref[...].astype(BF16), pw_ref[...])


def _combine_ple(dest, h1, gate, p2, y_rows, g, gw_bf, pw_bf, tm=128):
    n, d = h1.shape
    rowblk = lambda w: pl.BlockSpec((tm, w), lambda i, ds: (i, 0))
    full = lambda a: pl.BlockSpec(a.shape, lambda i, ds: (0,) * a.ndim)
    return pl.pallas_call(
        _ple_kernel,
        out_shape=jax.ShapeDtypeStruct((n, d), F32),
        grid_spec=pltpu.PrefetchScalarGridSpec(
            num_scalar_prefetch=1,
            grid=(n // tm,),
            in_specs=[rowblk(d), rowblk(LANES), rowblk(p2.shape[1]), pl.BlockSpec(memory_space=pl.ANY),
                      full(g), full(gw_bf), full(pw_bf)],
            out_specs=rowblk(d),
            scratch_shapes=[pltpu.VMEM((2, TOP_K, tm, d), F32), pltpu.SemaphoreType.DMA((2,))],
        ),
        compiler_params=_cp(("arbitrary",)),
        name="combine_ple",
    )(dest, h1, gate, p2, y_rows, g, gw_bf, pw_bf)


MOE_TR = 512
MOE_TN1 = 512
MOE_TN2 = 1024


def _layer(x, p, mix_norm_g, w_in, rw, nsa, w_out, moe_norm_g, router_w, router_b, moe_w1, moe_b1, moe_w2, moe_b2,
           ple_norm_g, ple_w, ple_gate_w):
    b, t, d = x.shape
    n = b * t
    x2 = x.reshape(n, d)
    row = lambda v: v.reshape(1, -1).astype(F32)
    n_lora = sum(RW_LORA)
    rw_cols = 3 * RW_WIDTH + n_lora
    kv0 = rw_cols + NSA_WIDTH
    n_gate = 3 * NSA_G * NSA_R
    zeros = lambda w: jnp.zeros((d, w), w_in.dtype)
    w_p = jnp.concatenate([
        w_in[:, :3 * RW_WIDTH], w_in[:, rw_cols:kv0], w_in[:, kv0:kv0 + 6 * NSA_KV],
        w_in[:, 3 * RW_WIDTH:rw_cols], zeros(MISC_GATE - n_lora), w_in[:, kv0 + 6 * NSA_KV:],
        zeros(MISC_W - MISC_GATE - n_gate)], axis=1).astype(BF16)
    u, kv_bf = _inproj(x2, row(mix_norm_g), w_p)
    u3 = u.reshape(b, t, IN_COLS)
    y_rw = _rwkv(u3, *_rwkv_params(*rw))
    y_nsa = _nsa(u3, kv_bf.reshape(b, t, 6 * NSA_KV), *nsa)
    out = _tail(x2, y_rw.reshape(n, RW_WIDTH), y_nsa.reshape(n, NSA_WIDTH), p.reshape(n, PLE_DIM), w_out, moe_norm_g,
                router_w, router_b, moe_w1, moe_b1, moe_w2, moe_b2, ple_norm_g, ple_w, ple_gate_w)
    return out.reshape(b, t, d)


def _tail(x2, y_rw, y_nsa, p2, w_out, moe_norm_g, router_w, router_b, moe_w1, moe_b1, moe_w2, moe_b2,
          ple_norm_g, ple_w, ple_gate_w):
    n, d = x2.shape
    row = lambda v: v.reshape(1, -1).astype(F32)
    rw_f = jnp.zeros((d, LANES), F32).at[:, :N_EXPERTS].set(router_w)
    rw_hi = rw_f.astype(BF16)
    rw_p = jnp.stack([rw_hi, (rw_f - rw_hi.astype(F32)).astype(BF16)])
    rb_p = jnp.full((1, LANES), NEG, F32).at[0, :N_EXPERTS].set(router_b)
    h1, xn, top_e, gate, rank, counts = _outproj(x2, y_rw, y_nsa, w_out.astype(BF16), row(moe_norm_g), rw_p, rb_p)
    counts = counts[0, :N_EXPERTS]
    max_tiles = (n * TOP_K) // MOE_TR + N_EXPERTS
    n_rows = max_tiles * MOE_TR
    sched1, pstart, n_act = _schedule(counts, MOE_TR, D_EXPERT // MOE_TN1, max_tiles)
    sched2, _, _ = _schedule(counts, MOE_TR, d // MOE_TN2, max_tiles)
    top_e = top_e[:, :TOP_K]
    own = top_e[:, :, None] == jnp.arange(N_EXPERTS, dtype=I32)[None, None, :]
    dest = (jnp.sum(jnp.where(own, pstart[None, None, :], 0), axis=2) + rank[:, :TOP_K]).astype(I32)
    tok = jnp.broadcast_to(jnp.arange(n, dtype=I32)[:, None], (n, TOP_K))
    row_tok = jnp.zeros((n_rows,), I32).at[dest.reshape(-1)].set(tok.reshape(-1), unique_indices=True,
                                                                mode="promise_in_bounds")
    x_rows = _gather_rows(row_tok, n_act.reshape(1).astype(I32), xn, n_rows)
    y_rows = _grouped_mlp(x_rows, sched1, sched2, moe_w1, moe_b1, moe_w2, moe_b2, MOE_TR, MOE_TN1, MOE_TN2)
    return _combine_ple(dest.reshape(-1), h1, gate, p2, y_rows, row(ple_norm_g),
                        ple_gate_w.astype(BF16), ple_w.astype(BF16))


def kernel(x, p, mix_norm_g, w_in, rw_mu, rw_w0, rw_w2, rw_a0, rw_a2, rw_g2, rw_k_k, rw_k_a, rw_r_k, rw_lnx_w,
           rw_lnx_b, nsa_q_norm, nsa_k_norm, cmp_pos_k, cmp_pos_v, cmp_k_w1, cmp_k_b1, cmp_k_w2, cmp_k_b2, cmp_v_w1,
           cmp_v_b1, cmp_v_w2, cmp_v_b2, w_out, moe_norm_g, router_w, router_b, moe_w1, moe_b1, moe_w2, moe_b2,
           ple_norm_g, ple_w, ple_gate_w):
    h = x
    for i in range(mix_norm_g.shape[0]):
        rw = (rw_mu[i], rw_w0[i], rw_w2[i], rw_a0[i], rw_a2[i], rw_g2[i], rw_k_k[i], rw_k_a[i], rw_r_k[i],
              rw_lnx_w[i], rw_lnx_b[i])
        nsa = (nsa_q_norm[i], nsa_k_norm[i], cmp_pos_k[i], cmp_pos_v[i], cmp_k_w1[i], cmp_k_b1[i], cmp_k_w2[i],
               cmp_k_b2[i], cmp_v_w1[i], cmp_v_b1[i], cmp_v_w2[i], cmp_v_b2[i])
        h = _layer(h, p[i], mix_norm_g[i], w_in[i], rw, nsa, w_out[i], moe_norm_g[i], router_w[i], router_b[i],
                   moe_w1[i], moe_b1[i], moe_w2[i], moe_b2[i], ple_norm_g[i], ple_w[i], ple_gate_w[i])
    return h
```

```python
import functools

import jax
import jax.numpy as jnp
from jax import lax
from jax.experimental import pallas as pl
from jax.experimental.pallas import tpu as pltpu

F32 = jnp.float32
BF16 = jnp.bfloat16
I32 = jnp.int32
HI = lax.Precision.HIGHEST

LANES = 128
VMEM_LIMIT = 56 * 1024 * 1024

D_MODEL = 2048
RMS_EPS = 1e-6
RW_WIDTH = 1024
RW_HEAD = 64
RW_LORA = (64, 64, 160)
RW_GN_EPS = 64e-5
NSA_WIDTH = 1024
NSA_HEAD = 64
NSA_G = 4
NSA_R = 4
NSA_KV = NSA_G * NSA_HEAD
CMP_LEN = 32
CMP_STRIDE = 16
CMP_HIDDEN = 256
SLC_BLOCK = 64
N_SELECT = 16
WINDOW = 512
N_EXPERTS = 32
TOP_K = 4
D_EXPERT = 2048
SWIGLU_LIMIT = 7.0
SWIGLU_ALPHA = 1.702
PLE_DIM = 256

COL_RKV = 0
COL_Q = 3072
COL_KV = 4096
COL_MISC = 5632
MISC_W = 512
MISC_GATE = 384
IN_COLS = 6144

RW_HG = 4
RW_LW = RW_HG * RW_HEAD
RW_C = 64
NSA_TQ = 256
NSA_KT = 512
NSA_KWT = 256
NEG = -1e30
LOG2E = 1.4426950408889634


def _cp(sem, vmem=VMEM_LIMIT):
    return pltpu.CompilerParams(dimension_semantics=sem, vmem_limit_bytes=vmem)


def _dot(a, b, prec=None):
    return jnp.dot(a, b, preferred_element_type=F32, precision=prec)


def _dot_nt(a, b, prec=None):
    return lax.dot_general(a, b, (((1,), (1,)), ((), ())), preferred_element_type=F32, precision=prec)


def _dot_split(x, m, pieces=2, left=False):
    out, rem = None, x
    for _ in range(pieces):
        part = rem.astype(BF16)
        rem = rem - part.astype(F32)
        term = _dot(m, part) if left else _dot(part, m)
        out = term if out is None else out + term
    return out


def _iota(shape, dim):
    return lax.broadcasted_iota(I32, shape, dim)


def _seg_matrix(n, seg, scale):
    same = (_iota((n, n), 0) // seg) == (_iota((n, n), 1) // seg)
    return jnp.where(same, scale, 0.0).astype(BF16)


def _inproj_kernel(x_ref, g_ref, w_ref, o_ref, kv_ref, xn_ref, *, kv_lo, kv_hi):
    j = pl.program_id(1)

    @pl.when(j == 0)
    def _():
        x = x_ref[...]
        ms = jnp.mean(x * x, axis=-1, keepdims=True)
        xn_ref[...] = (x * lax.rsqrt(ms + RMS_EPS) * g_ref[...]).astype(BF16)

    u = _dot(xn_ref[...], w_ref[...])
    o_ref[...] = u

    @pl.when((j >= kv_lo) & (j < kv_hi))
    def _():
        kv_ref[...] = u.astype(kv_ref.dtype)


def _inproj(x2, g, w_bf, tm=1024, tn=512):
    n, d = x2.shape
    nc = w_bf.shape[1]
    kv_lo, kv_hi = COL_KV // tn, (COL_KV + 6 * NSA_KV) // tn
    kv_col = lambda i, j: (i, jnp.clip(j, kv_lo, kv_hi - 1) - kv_lo)
    return pl.pallas_call(
        functools.partial(_inproj_kernel, kv_lo=kv_lo, kv_hi=kv_hi),
        out_shape=(jax.ShapeDtypeStruct((n, nc), F32), jax.ShapeDtypeStruct((n, 6 * NSA_KV), BF16)),
        grid=(n // tm, nc // tn),
        in_specs=[
            pl.BlockSpec((tm, d), lambda i, j: (i, 0)),
            pl.BlockSpec((1, d), lambda i, j: (0, 0)),
            pl.BlockSpec((d, tn), lambda i, j: (0, j)),
        ],
        out_specs=(pl.BlockSpec((tm, tn), lambda i, j: (i, j)), pl.BlockSpec((tm, tn), kv_col)),
        scratch_shapes=[pltpu.VMEM((tm, d), BF16)],
        compiler_params=_cp(("arbitrary", "arbitrary")),
        name="inproj",
    )(x2, g, w_bf)


def _softplus(z):
    return jnp.maximum(z, 0.0) + jnp.log(1.0 + jnp.exp(-jnp.abs(z)))


def _rwkv_kernel(r_ref, k_ref, v_ref, m_ref, mur_ref, muk_ref, muv_ref, mum_ref, w2_ref, a2_ref, g2_ref,
                 w0_ref, a0_ref, kk_ref, ka_ref, rk_ref, lw_ref, lb_ref, o_ref,
                 st_ref, pr_ref, pk_ref, pv_ref, pm_ref, *, nchunk):
    tt = nchunk * RW_C
    lw = RW_LW

    @pl.when(pl.program_id(2) == 0)
    def _():
        st_ref[...] = jnp.zeros_like(st_ref)
        pr_ref[...] = jnp.zeros_like(pr_ref)
        pk_ref[...] = jnp.zeros_like(pk_ref)
        pv_ref[...] = jnp.zeros_like(pv_ref)
        pm_ref[...] = jnp.zeros_like(pm_ref)

    def shift_lerp(x, prev_ref, mu):
        rolled = pltpu.roll(x, 1, 0)
        first = _iota(x.shape, 0) == 0
        sh = jnp.where(first, prev_ref[0:1, :], rolled)
        prev_ref[0:1, :] = x[tt - 1:tt, :]
        return x + (sh - x) * mu

    r = shift_lerp(r_ref[0], pr_ref, mur_ref[...])
    k = shift_lerp(k_ref[0], pk_ref, muk_ref[...])
    v = shift_lerp(v_ref[0], pv_ref, muv_ref[...])
    m = shift_lerp(m_ref[0], pm_ref, mum_ref[...])

    lw_dec = _dot(jnp.tanh(m).astype(BF16), w2_ref[...])
    lw_a = _dot(m.astype(BF16), a2_ref[...])
    g = _dot(jax.nn.sigmoid(m).astype(BF16), g2_ref[...])
    w_log = -_softplus(-(w0_ref[...] + lw_dec)) - 0.5
    ld = -jnp.exp(w_log)
    a = jax.nn.sigmoid(a0_ref[...] + lw_a)

    ones_seg = _seg_matrix(lw, RW_HEAD, 1.0)
    kk = k * kk_ref[...]
    kk = kk / jnp.maximum(jnp.sqrt(_dot_split(kk * kk, ones_seg)), 1e-12)
    k2 = k * (1.0 + (a - 1.0) * ka_ref[...])
    av = -kk
    bv = kk * a

    ri = _iota((lw, lw), 0)
    ci = _iota((lw, lw), 1)
    same_blk = (ri // RW_C) == (ci // RW_C)
    m_strict = same_blk & (ci < ri)
    m_lower = same_blk & (ci <= ri)
    eye = (ri == ci).astype(F32)
    tri = (_iota((RW_C, RW_C), 1) <= _iota((RW_C, RW_C), 0)).astype(BF16)
    lane_head = _iota((1, lw), 1) // RW_HEAD
    hmask = [(lane_head == h).astype(F32) for h in range(RW_HG)]

    def stack_heads(x):
        return jnp.concatenate([x * hmask[h] for h in range(RW_HG)], axis=0)

    def unstack_heads(xs):
        out = xs[0:RW_C] * hmask[0]
        for h in range(1, RW_HG):
            out = out + xs[h * RW_C:(h + 1) * RW_C] * hmask[h]
        return out

    def fold_heads(xs):
        out = xs[0:RW_C]
        for h in range(1, RW_HG):
            out = out + xs[h * RW_C:(h + 1) * RW_C]
        return out

    chunks = []
    for c in range(nchunk):
        sl = slice(c * RW_C, (c + 1) * RW_C)
        ld_c = ld[sl]
        cs = _dot_split(ld_c, tri, pieces=3, left=True)
        w_in = jnp.exp(cs)
        w_inv = jnp.exp(-cs)
        w_prev = jnp.exp(cs - ld_c)
        w_last = w_in[RW_C - 1:RW_C, :]
        bt = bv[sl] * w_inv
        kt = k2[sl] * w_inv
        at_s = stack_heads(av[sl] * w_prev)
        rt_s = stack_heads(r[sl] * w_in)
        v_c = v[sl]
        chunks.append(dict(
            at_s=at_s, rt_s=rt_s, at_b=at_s.astype(BF16), rt_b=rt_s.astype(BF16), w_last=w_last,
            bt4=jnp.concatenate([bt] * RW_HG, axis=0).astype(BF16),
            kt4=jnp.concatenate([kt] * RW_HG, axis=0).astype(BF16),
            v4=jnp.concatenate([v_c] * RW_HG, axis=0).astype(BF16),
            bp=(bt * w_last).astype(BF16),
            sv=jnp.where(same_blk, _dot(v_c.T.astype(BF16), (kt * w_last).astype(BF16)), 0.0)))
    for ch in chunks:
        ch["a_ab"] = jnp.where(m_strict, _dot_nt(ch["at_b"], ch["bt4"]), 0.0)
        ch["a_ak"] = jnp.where(m_strict, _dot_nt(ch["at_b"], ch["kt4"]), 0.0).astype(BF16)
        ch["a_rb"] = jnp.where(m_lower, _dot_nt(ch["rt_b"], ch["bt4"]), 0.0).astype(BF16)
        ch["a_rk"] = jnp.where(m_lower, _dot_nt(ch["rt_b"], ch["kt4"]), 0.0).astype(BF16)
        ch["pw"] = ch["a_ab"]
        ch["inv"] = eye + ch["a_ab"]
    for _ in range(RW_C.bit_length() - 2):
        for ch in chunks:
            pw_b = ch["pw"].astype(BF16)
            ch["pw"] = _dot(pw_b, pw_b)
        for ch in chunks:
            ch["inv"] = ch["inv"] + _dot(ch["inv"].astype(BF16), ch["pw"].astype(BF16))
    for ch in chunks:
        inv_b = ch["inv"].astype(BF16)
        ia = _dot(inv_b, ch["at_b"])
        u_const = _dot(inv_b, _dot(ch["a_ak"], ch["v4"]).astype(BF16))
        g_mat = ch["rt_s"] + _dot(ch["a_rb"], ia.astype(BF16))
        y_const = _dot(ch["a_rb"], u_const.astype(BF16)) + _dot(ch["a_rk"], ch["v4"])
        ch["ia_u"] = fold_heads(ia).astype(BF16)
        ch["g_u"] = fold_heads(g_mat).astype(BF16)
        ch["u_const_t"] = unstack_heads(u_const).T
        ch["y_const"] = unstack_heads(y_const)

    ys = []
    st = st_ref[...]
    for ch in chunks:
        st_b = st.astype(BF16)
        ys.append(_dot_nt(ch["g_u"], st_b) + ch["y_const"])
        u_t = _dot_nt(st_b, ch["ia_u"]) + ch["u_const_t"]
        st = st * ch["w_last"] + jnp.where(same_blk, _dot(u_t.astype(BF16), ch["bp"]), 0.0) + ch["sv"]
    st_ref[...] = st
    y = jnp.concatenate(ys, axis=0) if nchunk > 1 else ys[0]

    avg_seg = _seg_matrix(lw, RW_HEAD, 1.0 / RW_HEAD)
    mean = _dot_split(y, avg_seg)
    yc = y - mean
    var = _dot_split(yc * yc, avg_seg)
    yn = yc * lax.rsqrt(var + RW_GN_EPS) * lw_ref[...] + lb_ref[...]
    bonus = _dot_split(r * k2 * rk_ref[...], ones_seg) * v
    o_ref[0] = ((yn + bonus) * g).astype(o_ref.dtype)


def _rwkv(u3, mu_p, w2p, a2p, g2p, w0, a0, k_k, k_a, r_k, lnx_w, lnx_b, nchunk=4):
    b, t, _ = u3.shape
    tt = nchunk * RW_C
    lw = RW_LW
    nhg = RW_WIDTH // lw
    col = lambda base: (lambda bi, h, ti: (bi, ti, base + h))
    vec = lambda base: (lambda bi, h, ti: (0, base + h))
    par = pl.BlockSpec((1, lw), vec(0))
    return pl.pallas_call(
        functools.partial(_rwkv_kernel, nchunk=nchunk),
        out_shape=jax.ShapeDtypeStruct((b, t, RW_WIDTH), BF16),
        grid=(b, nhg, t // tt),
        in_specs=[
            pl.BlockSpec((1, tt, lw), col(0)),
            pl.BlockSpec((1, tt, lw), col(nhg)),
            pl.BlockSpec((1, tt, lw), col(2 * nhg)),
            pl.BlockSpec((1, tt, MISC_W), lambda bi, h, ti: (bi, ti, COL_MISC // MISC_W)),
            pl.BlockSpec((1, lw), vec(0)),
            pl.BlockSpec((1, lw), vec(nhg)),
            pl.BlockSpec((1, lw), vec(2 * nhg)),
            pl.BlockSpec((1, MISC_W), lambda bi, h, ti: (0, COL_MISC // MISC_W)),
            pl.BlockSpec((MISC_W, lw), lambda bi, h, ti: (0, h)),
            pl.BlockSpec((MISC_W, lw), lambda bi, h, ti: (0, h)),
            pl.BlockSpec((MISC_W, lw), lambda bi, h, ti: (0, h)),
            par, par, par, par, par, par, par,
        ],
        out_specs=pl.BlockSpec((1, tt, lw), lambda bi, h, ti: (bi, ti, h)),
        scratch_shapes=[
            pltpu.VMEM((lw, lw), F32),
            pltpu.VMEM((8, lw), F32),
            pltpu.VMEM((8, lw), F32),
            pltpu.VMEM((8, lw), F32),
            pltpu.VMEM((8, MISC_W), F32),
        ],
        compiler_params=_cp(("arbitrary", "arbitrary", "arbitrary")),
        name="rwkv7",
    )(u3, u3, u3, u3, mu_p, mu_p, mu_p, mu_p, w2p, a2p, g2p, w0, a0, k_k, k_a, r_k, lnx_w, lnx_b)


def _pad_rows(w, rows, offset):
    return jnp.zeros((rows, w.shape[1]), w.dtype).at[offset:offset + w.shape[0]].set(w)


def _rwkv_params(mu, w0, w2, a0, a2, g2, k_k, k_a, r_k, lnx_w, lnx_b):
    mu_p = jnp.zeros((1, IN_COLS), F32)
    mu_p = mu_p.at[0, COL_RKV:COL_RKV + 3 * RW_WIDTH].set(mu[:3 * RW_WIDTH])
    mu_p = mu_p.at[0, COL_MISC:COL_MISC + sum(RW_LORA)].set(mu[3 * RW_WIDTH:])
    o1, o2 = RW_LORA[0], RW_LORA[0] + RW_LORA[1]
    row = lambda p: p.reshape(1, RW_WIDTH).astype(F32)
    return (mu_p, _pad_rows(w2, MISC_W, 0).astype(BF16), _pad_rows(a2, MISC_W, o1).astype(BF16),
            _pad_rows(g2, MISC_W, o2).astype(BF16), row(w0), row(a0), row(k_k), row(k_a), row(r_k),
            row(lnx_w), row(lnx_b))


def _nsa_prep_kernel(q_ref, ks_ref, kw_ref, gt_ref, qg_ref, ksg_ref, kwg_ref, qo_ref, kso_ref, kwo_ref, go_ref):
    avg = _seg_matrix(NSA_KV, NSA_HEAD, 1.0 / NSA_HEAD)

    def head_norm(x, g):
        ms = _dot_split(x * x, avg)
        return x * lax.rsqrt(ms + RMS_EPS) * g

    scale = NSA_HEAD ** -0.5 * LOG2E
    for s in range(NSA_G):
        sl = slice(s * NSA_KV, (s + 1) * NSA_KV)
        q_t = (head_norm(q_ref[:, sl], qg_ref[...]) * scale).T
        for h in range(NSA_R):
            qo_ref[0, s, h] = q_t[h * NSA_HEAD:(h + 1) * NSA_HEAD, :].astype(qo_ref.dtype)
    kso_ref[...] = head_norm(ks_ref[...], ksg_ref[...]).astype(kso_ref.dtype)
    kwo_ref[...] = head_norm(kw_ref[...], kwg_ref[...]).astype(kwo_ref.dtype)
    go_ref[...] = jax.nn.sigmoid(gt_ref[...])


def _nsa_prep(u2, b, q_gain, ks_gain, kw_gain, tm=512):
    n = u2.shape[0]
    tiles = n // b // tm
    kvb = COL_KV // NSA_KV
    gain = pl.BlockSpec((1, NSA_KV), lambda i: (0, 0))
    return pl.pallas_call(
        _nsa_prep_kernel,
        out_shape=(jax.ShapeDtypeStruct((b, NSA_G, NSA_R, NSA_HEAD, n // b), BF16),
                   jax.ShapeDtypeStruct((n, NSA_KV), BF16),
                   jax.ShapeDtypeStruct((n, NSA_KV), BF16), jax.ShapeDtypeStruct((n, LANES), F32)),
        grid=(n // tm,),
        in_specs=[
            pl.BlockSpec((tm, NSA_WIDTH), lambda i: (i, COL_Q // NSA_WIDTH)),
            pl.BlockSpec((tm, NSA_KV), lambda i: (i, kvb + 2)),
            pl.BlockSpec((tm, NSA_KV), lambda i: (i, kvb + 4)),
            pl.BlockSpec((tm, LANES), lambda i: (i, (COL_MISC + MISC_GATE) // LANES)),
            gain, gain, gain,
        ],
        out_specs=(pl.BlockSpec((1, NSA_G, NSA_R, NSA_HEAD, tm), lambda i: (i // tiles, 0, 0, 0, i % tiles)),
                   pl.BlockSpec((tm, NSA_KV), lambda i: (i, 0)),
                   pl.BlockSpec((tm, NSA_KV), lambda i: (i, 0)), pl.BlockSpec((tm, LANES), lambda i: (i, 0))),
        compiler_params=_cp(("arbitrary",)),
        name="nsa_prep",
    )(u2, u2, u2, u2, q_gain, ks_gain, kw_gain)


def _nsa_cmp_kernel(zk_ref, zv_ref, pk_ref, pv_ref, kw1_ref, kb1_ref, kw2_ref, kb2_ref, vw1_ref, vb1_ref,
                    vw2_ref, vb2_ref, kn_ref, ko_ref, vo_ref):
    nz = zk_ref.shape[2]
    half = kw1_ref.shape[0] // 2

    def compress(z, pos, w1_ref, b1, w2, b2):
        top = _dot(z, w1_ref[0:half, :])
        bot = _dot(z, w1_ref[half:, :])
        c1 = _dot(jnp.broadcast_to(pos, (8, pos.shape[1])).astype(BF16), w1_ref[...])[0:1] + b1
        hid = jax.nn.gelu(top + pltpu.roll(bot, nz - 1, 0) + c1)
        return _dot(hid.astype(BF16), w2) + b2

    kc = compress(zk_ref[0, 0], pk_ref[...], kw1_ref, kb1_ref[...], kw2_ref[...], kb2_ref[...])
    ms = jnp.mean(kc * kc, axis=-1, keepdims=True)
    ko_ref[0, 0] = (kc * lax.rsqrt(ms + RMS_EPS) * kn_ref[...]).astype(ko_ref.dtype)
    vc = compress(zv_ref[0, 0], pv_ref[...], vw1_ref, vb1_ref[...], vw2_ref[...], vb2_ref[...])
    vo_ref[0, 0] = vc.astype(vo_ref.dtype)


def _nsa_cmp(zk, zv, pos_k, pos_v, kw1, kb1, kw2, kb2, vw1, vb1, vw2, vb2, kn0):
    b, g, nz, zw = zk.shape
    full = lambda a: pl.BlockSpec(a.shape, lambda bi, gi: (0,) * a.ndim)
    zspec = pl.BlockSpec((1, 1, nz, zw), lambda bi, gi: (bi, gi, 0, 0))
    ospec = pl.BlockSpec((1, 1, nz, NSA_HEAD), lambda bi, gi: (bi, gi, 0, 0))
    params = (pos_k, pos_v, kw1, kb1, kw2, kb2, vw1, vb1, vw2, vb2, kn0)
    return pl.pallas_call(
        _nsa_cmp_kernel,
        out_shape=(jax.ShapeDtypeStruct((b, g, nz, NSA_HEAD), BF16),) * 2,
        grid=(b, g),
        in_specs=[zspec, zspec] + [full(a) for a in params],
        out_specs=(ospec, ospec),
        compiler_params=_cp(("arbitrary", "arbitrary")),
        name="nsa_compress",
    )(zk, zv, *params)


def _nsa_select_kernel(q_ref, kc_ref, vc_ref, oc_ref, sel_ref, *, n_sel):
    tq = q_ref.shape[4]
    nz = kc_ref.shape[2]
    ns = sel_ref.shape[2]
    t0 = pl.program_id(2) * tq
    n_idx = _iota((nz, tq), 0)
    t_idx = t0 + _iota((nz, tq), 1)
    cmask = (n_idx * CMP_STRIDE + (CMP_LEN - 1) <= t_idx) & (n_idx < nz - 1)
    kc = kc_ref[0, 0]
    vc_t = vc_ref[0, 0]
    psum = jnp.zeros((nz, tq), F32)
    for r in range(NSA_R):
        s = jnp.where(cmask, _dot(kc, q_ref[0, 0, r]), -jnp.inf)
        mx = jnp.max(s, axis=0, keepdims=True)
        e = jnp.exp2(s - jnp.where(mx == -jnp.inf, 0.0, mx))
        p = e / jnp.maximum(jnp.sum(e, axis=0, keepdims=True), 1e-30)
        psum = psum + p
        oc_ref[0, 0, r] = _dot(vc_t, p.astype(BF16))
    jb = _iota((ns, nz), 0) * SLC_BLOCK
    cb = _iota((ns, nz), 1) * CMP_STRIDE
    overlap = ((cb < jb + SLC_BLOCK) & (cb + CMP_LEN > jb) & (_iota((ns, nz), 1) < nz - 1)).astype(BF16)
    imp = _dot_split(psum, overlap, pieces=3, left=True)
    j_idx = _iota((ns, tq), 0)
    t_q = t0 + _iota((ns, tq), 1)
    qblk = t_q // SLC_BLOCK
    forced = (j_idx == 0) | (j_idx == qblk) | (j_idx == qblk - 1)
    imp = jnp.where(forced, jnp.inf, imp)
    imp = jnp.where(j_idx * SLC_BLOCK <= t_q, imp, -jnp.inf)
    rank = jnp.zeros((ns, tq), F32)
    for i in range(ns):
        row = imp[i:i + 1, :]
        before = (row > imp) | ((row == imp) & (j_idx > i))
        rank = rank + before.astype(F32)
    sel_ref[0, 0] = ((rank < n_sel) & (imp > -jnp.inf)).astype(F32)


def _nsa_select(q_t, kc, vc_t, tq=NSA_TQ):
    b, g, r, dk, t = q_t.shape
    nz = kc.shape[2]
    ns = t // SLC_BLOCK
    qspec = pl.BlockSpec((1, 1, r, dk, tq), lambda bi, gi, i: (bi, gi, 0, 0, i))
    return pl.pallas_call(
        functools.partial(_nsa_select_kernel, n_sel=min(N_SELECT, ns)),
        out_shape=(jax.ShapeDtypeStruct((b, g, r, dk, t), F32), jax.ShapeDtypeStruct((b, g, ns, t), F32)),
        grid=(b, g, t // tq),
        in_specs=[
            qspec,
            pl.BlockSpec((1, 1, nz, dk), lambda bi, gi, i: (bi, gi, 0, 0)),
            pl.BlockSpec((1, 1, dk, nz), lambda bi, gi, i: (bi, gi, 0, 0)),
        ],
        out_specs=(qspec, pl.BlockSpec((1, 1, ns, tq), lambda bi, gi, i: (bi, gi, 0, i))),
        compiler_params=_cp(("arbitrary", "arbitrary", "arbitrary")),
        name="nsa_select",
    )(q_t, kc, vc_t)


def _nsa_attn_kernel(q_ref, ks_ref, vs_ref, kw_ref, vw_ref, sel_ref, oc_ref, gt_ref, o_ref, m_ref, acc_ref):
    r, dk, tq = q_ref.shape[2:]
    kt = vs_ref.shape[4]
    kwt = vw_ref.shape[4]
    t0 = pl.program_id(2) * tq
    q_t = jnp.concatenate([q_ref[0, 0, h] for h in range(r)], axis=1)
    tpos = t0 + _iota((1, tq), 1)

    def reset():
        m_ref[...] = jnp.full_like(m_ref, NEG)
        acc_ref[...] = jnp.zeros_like(acc_ref)

    def online_update(s, bias, v_aug):
        ps, alphas = [], []
        for h in range(r):
            s_h = s[:, h * tq:(h + 1) * tq] + bias
            m_old = m_ref[h]
            m_new = jnp.maximum(m_old, jnp.max(s_h, axis=0, keepdims=True))
            alphas.append(jnp.exp2(m_old - m_new))
            ps.append(jnp.exp2((s_h - m_new).astype(BF16)))
            m_ref[h] = m_new
        acc_ref[...] = jnp.concatenate(alphas, axis=1) * acc_ref[...] + _dot(v_aug, jnp.concatenate(ps, axis=1))

    def result():
        seen = jnp.concatenate([m_ref[h] for h in range(r)], axis=1) > 0.5 * NEG
        acc = acc_ref[...]
        return jnp.where(seen, acc[0:dk] / jnp.maximum(acc[dk:dk + 1], 1e-30), 0.0)

    reset()
    nb = kt // SLC_BLOCK

    def sel_body(j, carry):
        k = ks_ref[0, 0, pl.ds(pl.multiple_of(j * kt, kt), kt), :]
        s = _dot(k, q_t)
        bm = jnp.concatenate([jnp.broadcast_to(sel_ref[0, 0, pl.ds(j * nb + c, 1), :], (SLC_BLOCK, tq))
                              for c in range(nb)], axis=0)
        kpos = j * kt + _iota((kt, tq), 0)
        online_update(s, jnp.where((bm > 0.5) & (kpos <= tpos), 0.0, NEG), vs_ref[0, 0, j])
        return carry

    lax.fori_loop(0, (t0 + tq + kt - 1) // kt, sel_body, 0)
    o_s = result()

    reset()

    def win_body(j, carry):
        k = kw_ref[0, 0, pl.ds(pl.multiple_of(j * kwt, kwt), kwt), :]
        s = _dot(k, q_t)
        kpos = j * kwt + _iota((kwt, tq), 0)
        online_update(s, jnp.where((kpos <= tpos) & (kpos > tpos - WINDOW), 0.0, NEG), vw_ref[0, 0, j])
        return carry

    lo = jnp.maximum(t0 - (WINDOW - 1), 0) // kwt
    lax.fori_loop(lo, (t0 + tq - 1) // kwt + 1, win_body, 0)
    o_w = result()

    outs = []
    for h in range(r):
        sl = slice(h * tq, (h + 1) * tq)
        gt = gt_ref[0, 0, h]
        outs.append(gt[0:1] * oc_ref[0, 0, h] + gt[1:2] * o_s[:, sl] + gt[2:3] * o_w[:, sl])
    o_ref[0] = jnp.concatenate(outs, axis=0).T.astype(o_ref.dtype)


def _nsa_attn(q_t, ks, vs_t, kw, vw_t, sel_t, oc_t, gates_t, tq=NSA_TQ):
    b, g, r, dk, t = q_t.shape
    ns = sel_t.shape[2]
    qspec = pl.BlockSpec((1, 1, r, dk, tq), lambda bi, gi, i: (bi, gi, 0, 0, i))
    kspec = pl.BlockSpec((1, 1, t, dk), lambda bi, gi, i: (bi, gi, 0, 0))
    vspec = lambda v: pl.BlockSpec((1, 1) + v.shape[2:], lambda bi, gi, i: (bi, gi, 0, 0, 0))
    return pl.pallas_call(
        _nsa_attn_kernel,
        out_shape=jax.ShapeDtypeStruct((b, t, g * r * dk), BF16),
        grid=(b, g, t // tq),
        in_specs=[
            qspec, kspec, vspec(vs_t), kspec, vspec(vw_t),
            pl.BlockSpec((1, 1, ns, tq), lambda bi, gi, i: (bi, gi, 0, i)),
            qspec,
            pl.BlockSpec((1, 1, r, 3, tq), lambda bi, gi, i: (bi, gi, 0, 0, i)),
        ],
        out_specs=pl.BlockSpec((1, tq, r * dk), lambda bi, gi, i: (bi, i, gi)),
        scratch_shapes=[pltpu.VMEM((r, 1, tq), F32), pltpu.VMEM((vs_t.shape[3], r * tq), F32)],
        compiler_params=_cp(("arbitrary", "arbitrary", "arbitrary")),
        name="nsa_attention",
    )(q_t, ks, vs_t, kw, vw_t, sel_t, oc_t, gates_t)


def _nsa(u3, kv_bf, q_norm, k_norm, pos_k, pos_v, ck_w1, ck_b1, ck_w2, ck_b2, cv_w1, cv_b1, cv_w2, cv_b2):
    b, t, _ = u3.shape
    n = b * t
    g, r, dk = NSA_G, NSA_R, NSA_HEAD
    tile_g = lambda p: jnp.tile(p.astype(F32), g).reshape(1, g * dk)
    q_t, ksn, kwn, gsig = _nsa_prep(u3.reshape(n, IN_COLS), b, tile_g(q_norm), tile_g(k_norm[1]), tile_g(k_norm[2]))
    per_group = lambda a: a.reshape(b, t, g, dk).transpose(0, 2, 1, 3)
    kv = lambda idx: kv_bf[..., idx * NSA_KV:(idx + 1) * NSA_KV]
    nz = t // CMP_STRIDE
    zk = per_group(kv(0)).reshape(b, g, nz, CMP_STRIDE * dk)
    zv = per_group(kv(1)).reshape(b, g, nz, CMP_STRIDE * dk)
    row = lambda p: p.reshape(1, -1).astype(F32)
    kc, vc = _nsa_cmp(zk, zv, row(pos_k), row(pos_v), ck_w1.astype(BF16), row(ck_b1), ck_w2.astype(BF16), row(ck_b2),
                      cv_w1.astype(BF16), row(cv_b1), cv_w2.astype(BF16), row(cv_b2), row(k_norm[0]))
    oc_t, sel_t = _nsa_select(q_t, kc, vc.transpose(0, 1, 3, 2))
    gates_t = gsig[:, :g * r * 3].reshape(b, t, g, r, 3).transpose(0, 2, 3, 4, 1)

    def tiled_t(a, tile):
        v_t = a.reshape(b, t // tile, tile, g, dk).transpose(0, 3, 1, 4, 2)
        extra = jnp.zeros(v_t.shape[:3] + (16, tile), BF16).at[:, :, :, 0, :].set(1.0)
        return jnp.concatenate([v_t, extra], axis=3)

    return _nsa_attn(q_t, per_group(ksn), tiled_t(kv(3), min(NSA_KT, t)), per_group(kwn),
                     tiled_t(kv(5), min(NSA_KWT, t)), sel_t, oc_t, gates_t)


def _outproj_kernel(x_ref, yr_ref, yn_ref, w_ref, g_ref, rw_ref, rb_ref,
                    h_ref, xn_ref, e_ref, gate_ref, rank_ref, cnt_ref, cnt_acc):
    tm = x_ref.shape[0]

    @pl.when(pl.program_id(0) == 0)
    def _():
        cnt_acc[...] = jnp.zeros_like(cnt_acc)

    h = x_ref[...] + _dot(yr_ref[...], w_ref[0:RW_WIDTH, :]) + _dot(yn_ref[...], w_ref[RW_WIDTH:, :])
    h_ref[...] = h
    ms = jnp.mean(h * h, axis=-1, keepdims=True)
    xn = h * lax.rsqrt(ms + RMS_EPS) * g_ref[...]
    xn_ref[...] = xn
    xn_hi = xn.astype(BF16)
    xn_lo = (xn - xn_hi.astype(F32)).astype(BF16)
    logits = (_dot(xn_hi, rw_ref[0]) + _dot(xn_lo, rw_ref[0]) + _dot(xn_hi, rw_ref[1])
              + rb_ref[...])
    lane = _iota((tm, LANES), 1).astype(F32)
    work = logits
    top_e, top_v = [], []
    for _ in range(TOP_K):
        mx = jnp.max(work, axis=1, keepdims=True)
        idx = jnp.min(jnp.where(work == mx, lane, float(LANES)), axis=1, keepdims=True)
        top_e.append(idx)
        top_v.append(mx)
        work = jnp.where(lane == idx, -jnp.inf, work)
    ex = [jnp.exp(v - top_v[0]) for v in top_v]
    den = ex[0] + ex[1] + ex[2] + ex[3]
    multihot = jnp.zeros((tm, LANES), F32)
    for e in top_e:
        multihot = multihot + (lane == e).astype(F32)
    strict = (_iota((tm, tm), 1) < _iota((tm, tm), 0)).astype(BF16)
    before = cnt_acc[...] + _dot(strict, multihot.astype(BF16))
    e_out = jnp.zeros((tm, LANES), F32)
    g_out = jnp.zeros((tm, LANES), F32)
    r_out = jnp.zeros((tm, LANES), F32)
    for k in range(TOP_K):
        slot = lane == float(k)
        rank_k = jnp.sum(jnp.where(lane == top_e[k], before, 0.0), axis=1, keepdims=True)
        e_out = jnp.where(slot, top_e[k], e_out)
        g_out = jnp.where(slot, ex[k] / den, g_out)
        r_out = jnp.where(slot, rank_k, r_out)
    e_ref[...] = e_out.astype(I32)
    gate_ref[...] = g_out
    rank_ref[...] = r_out.astype(I32)
    cnt_acc[...] = cnt_acc[...] + jnp.sum(multihot, axis=0, keepdims=True)
    cnt_ref[...] = cnt_acc[...].astype(I32)


def _outproj(x2, y_rw, y_nsa, w_out_bf, g, rw_p, rb_p, tm=256):
    n, d = x2.shape
    rowblk = lambda w: pl.BlockSpec((tm, w), lambda i: (i, 0))
    full = lambda a: pl.BlockSpec(a.shape, lambda i: (0,) * a.ndim)
    return pl.pallas_call(
        _outproj_kernel,
        out_shape=(jax.ShapeDtypeStruct((n, d), F32), jax.ShapeDtypeStruct((n, d), F32),
                   jax.ShapeDtypeStruct((n, LANES), I32), jax.ShapeDtypeStruct((n, LANES), F32),
                   jax.ShapeDtypeStruct((n, LANES), I32), jax.ShapeDtypeStruct((1, LANES), I32)),
        grid=(n // tm,),
        in_specs=[rowblk(d), rowblk(RW_WIDTH), rowblk(NSA_WIDTH), full(w_out_bf), full(g), full(rw_p), full(rb_p)],
        out_specs=(rowblk(d), rowblk(d), rowblk(LANES), rowblk(LANES), rowblk(LANES),
                   pl.BlockSpec((1, LANES), lambda i: (0, 0))),
        scratch_shapes=[pltpu.VMEM((1, LANES), F32)],
        compiler_params=_cp(("arbitrary",)),
        name="outproj_router",
    )(x2, y_rw, y_nsa, w_out_bf, g, rw_p, rb_p)


GATHER_UNROLL = 8
MOE_ROW_SPLIT = 2


def _gather_kernel(row_tok, n_act, x_hbm, o_ref, buf, sem):
    tg = o_ref.shape[0]
    base = pl.program_id(0) * tg

    @pl.when(base < n_act[0])
    def _():
        def issue(r8, c):
            for u in range(GATHER_UNROLL):
                r = r8 * GATHER_UNROLL + u
                pltpu.make_async_copy(x_hbm.at[pl.ds(row_tok[base + r], 1), :], buf.at[pl.ds(r, 1), :], sem).start()
            return c

        lax.fori_loop(0, tg // GATHER_UNROLL, issue, 0)
        pltpu.make_async_copy(x_hbm.at[pl.ds(0, tg), :], buf, sem).wait()
        o_ref[...] = buf[...].astype(o_ref.dtype)

    @pl.when(base >= n_act[0])
    def _():
        o_ref[...] = jnp.zeros_like(o_ref)


def _gather_rows(row_tok, n_act, xn, n_rows, tg=512):
    d = xn.shape[1]
    return pl.pallas_call(
        _gather_kernel,
        out_shape=jax.ShapeDtypeStruct((n_rows, d), BF16),
        grid_spec=pltpu.PrefetchScalarGridSpec(
            num_scalar_prefetch=2,
            grid=(n_rows // tg,),
            in_specs=[pl.BlockSpec(memory_space=pl.ANY)],
            out_specs=pl.BlockSpec((tg, d), lambda i, rt, na: (i, 0)),
            scratch_shapes=[pltpu.VMEM((tg, d), xn.dtype), pltpu.SemaphoreType.DMA],
        ),
        compiler_params=_cp(("arbitrary",)),
        name="moe_gather",
    )(row_tok, n_act, xn)


def _stream_weights(s, sched, copies, use):
    e_s, j_s, first_s, grp_s, ne_s, nj_s, hn_s = sched

    @pl.when(first_s[s] == 1)
    def _():
        slot = grp_s[s] % 2

        @pl.when(s == 0)
        def _():
            for c in copies(e_s[0], j_s[0], 0):
                c.start()

        for c in copies(e_s[s], j_s[s], slot):
            c.wait()

        @pl.when(hn_s[s] == 1)
        def _():
            for c in copies(ne_s[s], nj_s[s], 1 - slot):
                c.start()

        use(slot)


def _gm1_kernel(e_s, j_s, rt_s, valid_s, first_s, rto_s, jo_s, grp_s, ne_s, nj_s, hn_s,
                x_ref, w_hbm, bg_ref, bl_ref, o_ref, wf, wgb, wlb, sems):
    s = pl.program_id(0)
    tn = wgb.shape[1]
    nt = w_hbm.shape[2] // (2 * tn)

    def copies(e, j, slot):
        return [pltpu.make_async_copy(w_hbm.at[e, :, pl.ds(pl.multiple_of((half * nt + j) * tn, tn), tn)],
                                      wf.at[slot, half], sems.at[slot]) for half in range(2)]

    def use(slot):
        wgb[...] = wf[slot, 0].astype(BF16)
        wlb[...] = wf[slot, 1].astype(BF16)

    @pl.when(valid_s[s] == 1)
    def _():
        _stream_weights(s, (e_s, j_s, first_s, grp_s, ne_s, nj_s, hn_s), copies, use)

        half = x_ref.shape[0] // MOE_ROW_SPLIT
        for part in range(MOE_ROW_SPLIT):
            rows = slice(part * half, (part + 1) * half)
            x = x_ref[rows, :]
            gt = jnp.minimum(_dot(x, wgb[...]) + bg_ref[0], SWIGLU_LIMIT)
            lin = jnp.clip(_dot(x, wlb[...]) + bl_ref[0], -SWIGLU_LIMIT, SWIGLU_LIMIT)
            o_ref[rows, :] = (gt * jax.nn.sigmoid(SWIGLU_ALPHA * gt) * (lin + 1.0)).astype(o_ref.dtype)

    @pl.when(valid_s[s] == 0)
    def _():
        o_ref[...] = jnp.zeros_like(o_ref)


def _gm2_kernel(e_s, j_s, rt_s, valid_s, first_s, rto_s, jo_s, grp_s, ne_s, nj_s, hn_s,
                x_ref, w_hbm, b_ref, o_ref, wf, wb, sems):
    s = pl.program_id(0)
    tn = wb.shape[1]

    def copies(e, j, slot):
        return [pltpu.make_async_copy(w_hbm.at[e, :, pl.ds(pl.multiple_of(j * tn, tn), tn)], wf.at[slot],
                                      sems.at[slot])]

    def use(slot):
        wb[...] = wf[slot].astype(BF16)

    @pl.when(valid_s[s] == 1)
    def _():
        _stream_weights(s, (e_s, j_s, first_s, grp_s, ne_s, nj_s, hn_s), copies, use)

        half = x_ref.shape[0] // MOE_ROW_SPLIT
        for part in range(MOE_ROW_SPLIT):
            rows = slice(part * half, (part + 1) * half)
            o_ref[rows, :] = _dot(x_ref[rows, :], wb[...]) + b_ref[0]

    @pl.when(valid_s[s] == 0)
    def _():
        o_ref[...] = jnp.zeros_like(o_ref)


def _schedule(counts, tr, n_col_tiles, max_tiles):
    tiles_e = (counts + tr - 1) // tr
    tile_end = jnp.cumsum(tiles_e)
    tile_start = tile_end - tiles_e
    steps_e = tiles_e * n_col_tiles
    step_end = jnp.cumsum(steps_e)
    total = step_end[-1]
    n_steps = max_tiles * n_col_tiles
    step = jnp.arange(n_steps, dtype=I32)
    s = jnp.minimum(step, total - 1)
    e = jnp.sum((step_end[None, :] <= s[:, None]).astype(I32), axis=1)
    own = (jnp.arange(N_EXPERTS, dtype=I32)[None, :] == e[:, None]).astype(I32)
    of_e = lambda per_expert: jnp.sum(own * per_expert[None, :], axis=1)
    local = s - of_e(step_end - steps_e)
    te = jnp.maximum(of_e(tiles_e), 1)
    j = (local // te).astype(I32)
    rt = (of_e(tile_start) + local % te).astype(I32)
    valid = step < total
    first = ((local % te) == 0).astype(I32)
    spare = step - total
    rt_out = jnp.where(valid, rt, tile_end[-1] + spare // n_col_tiles).astype(I32)
    j_out = jnp.where(valid, j, spare % n_col_tiles).astype(I32)
    ar = jnp.arange(N_EXPERTS, dtype=I32)
    has = tiles_e > 0
    later = jnp.min(jnp.where((ar[None, :] > ar[:, None]) & has[None, :], ar[None, :], N_EXPERTS), axis=1)
    grp = (of_e(jnp.cumsum(has.astype(I32)) - has.astype(I32)) * n_col_tiles + j).astype(I32)
    last_col = j == n_col_tiles - 1
    next_e = jnp.where(last_col, jnp.minimum(of_e(later), N_EXPERTS - 1), e).astype(I32)
    next_j = jnp.where(last_col, 0, j + 1).astype(I32)
    has_next = (jnp.where(last_col, of_e(later) < N_EXPERTS, True) & valid).astype(I32)
    sched = (e, j, rt, valid.astype(I32), first, rt_out, j_out, grp, next_e, next_j, has_next)
    return sched, tile_start * tr, tile_end[-1] * tr


def _grouped_mlp(x_rows, sched1, sched2, w1, b1, w2, b2, tr, tn1, tn2):
    n_rows, d = x_rows.shape
    de = w2.shape[1]
    nt1 = de // tn1
    b1r = b1.reshape(N_EXPERTS, 1, 2 * de)
    b2r = b2.reshape(N_EXPERTS, 1, d)
    x_map = lambda s, e, j, rt, *_: (rt[s], 0)
    out_map = lambda s, e, j, rt, v, f, ro, jo, *_: (ro[s], jo[s])
    hbm = pl.BlockSpec(memory_space=pl.ANY)
    act = pl.pallas_call(
        _gm1_kernel,
        out_shape=jax.ShapeDtypeStruct((n_rows, de), BF16),
        grid_spec=pltpu.PrefetchScalarGridSpec(
            num_scalar_prefetch=len(sched1),
            grid=(sched1[0].shape[0],),
            in_specs=[
                pl.BlockSpec((tr, d), x_map),
                hbm,
                pl.BlockSpec((1, 1, tn1), lambda s, e, j, *_: (e[s], 0, j[s])),
                pl.BlockSpec((1, 1, tn1), lambda s, e, j, *_: (e[s], 0, nt1 + j[s])),
            ],
            out_specs=pl.BlockSpec((tr, tn1), out_map),
            scratch_shapes=[pltpu.VMEM((2, 2, d, tn1), w1.dtype), pltpu.VMEM((d, tn1), BF16),
                            pltpu.VMEM((d, tn1), BF16), pltpu.SemaphoreType.DMA((2,))],
        ),
        compiler_params=_cp(("arbitrary",)),
        name="moe_up",
    )(*sched1, x_rows, w1, b1r, b1r)
    return pl.pallas_call(
        _gm2_kernel,
        out_shape=jax.ShapeDtypeStruct((n_rows, d), F32),
        grid_spec=pltpu.PrefetchScalarGridSpec(
            num_scalar_prefetch=len(sched2),
            grid=(sched2[0].shape[0],),
            in_specs=[
                pl.BlockSpec((tr, de), x_map),
                hbm,
                pl.BlockSpec((1, 1, tn2), lambda s, e, j, *_: (e[s], 0, j[s])),
            ],
            out_specs=pl.BlockSpec((tr, tn2), out_map),
            scratch_shapes=[pltpu.VMEM((2, de, tn2), w2.dtype), pltpu.VMEM((de, tn2), BF16),
                            pltpu.SemaphoreType.DMA((2,))],
        ),
        compiler_params=_cp(("arbitrary",)),
        name="moe_down",
    )(*sched2, act, w2, b2r)


def _ple_kernel(dest, h_ref, gate_ref, p_ref, y_hbm, g_ref, gw_ref, pw_ref, o_ref, buf, sems):
    tm = h_ref.shape[0]
    i = pl.program_id(0)
    slot = i % 2

    def fetch(step, s):
        def issue(t2, c):
            for u in range(2):
                t = t2 * 2 + u
                for k in range(TOP_K):
                    pltpu.make_async_copy(y_hbm.at[pl.ds(dest[(step * tm + t) * TOP_K + k], 1), :],
                                          buf.at[s, k, pl.ds(t, 1), :], sems.at[s]).start()
            return c

        lax.fori_loop(0, tm // 2, issue, 0)

    @pl.when(i == 0)
    def _():
        fetch(0, 0)

    @pl.when(i + 1 < pl.num_programs(0))
    def _():
        fetch(i + 1, 1 - slot)

    for k in range(TOP_K):
        pltpu.make_async_copy(y_hbm.at[pl.ds(0, tm), :], buf.at[slot, k], sems.at[slot]).wait()
    gate = gate_ref[...]
    h = h_ref[...]
    for k in range(TOP_K):
        h = h + gate[:, k:k + 1] * buf[slot, k]
    ms = jnp.mean(h * h, axis=-1, keepdims=True)
    hn = (h * lax.rsqrt(ms + RMS_EPS) * g_ref[...]).astype(BF16)
    pgate = jax.nn.sigmoid(_dot(hn, gw_ref[...]))
    o_ref[...] = h + pgate * _dot(p_ref[...].astype(BF16), pw_ref[...])


def _combine_ple(dest, h1, gate, p2, y_rows, g, gw_bf, pw_bf, tm=128):
    n, d = h1.shape
    rowblk = lambda w: pl.BlockSpec((tm, w), lambda i, ds: (i, 0))
    full = lambda a: pl.BlockSpec(a.shape, lambda i, ds: (0,) * a.ndim)
    return pl.pallas_call(
        _ple_kernel,
        out_shape=jax.ShapeDtypeStruct((n, d), F32),
        grid_spec=pltpu.PrefetchScalarGridSpec(
            num_scalar_prefetch=1,
            grid=(n // tm,),
            in_specs=[rowblk(d), rowblk(LANES), rowblk(p2.shape[1]), pl.BlockSpec(memory_space=pl.ANY),
                      full(g), full(gw_bf), full(pw_bf)],
            out_specs=rowblk(d),
            scratch_shapes=[pltpu.VMEM((2, TOP_K, tm, d), F32), pltpu.SemaphoreType.DMA((2,))],
        ),
        compiler_params=_cp(("arbitrary",)),
        name="combine_ple",
    )(dest, h1, gate, p2, y_rows, g, gw_bf, pw_bf)


MOE_TR = 512
MOE_TN1 = 512
MOE_TN2 = 1024


def _layer(x, p, mix_norm_g, w_in, rw, nsa, w_out, moe_norm_g, router_w, router_b, moe_w1, moe_b1, moe_w2, moe_b2,
           ple_norm_g, ple_w, ple_gate_w):
    b, t, d = x.shape
    n = b * t
    x2 = x.reshape(n, d)
    row = lambda v: v.reshape(1, -1).astype(F32)
    n_lora = sum(RW_LORA)
    rw_cols = 3 * RW_WIDTH + n_lora
    kv0 = rw_cols + NSA_WIDTH
    n_gate = 3 * NSA_G * NSA_R
    zeros = lambda w: jnp.zeros((d, w), w_in.dtype)
    w_p = jnp.concatenate([
        w_in[:, :3 * RW_WIDTH], w_in[:, rw_cols:kv0], w_in[:, kv0:kv0 + 6 * NSA_KV],
        w_in[:, 3 * RW_WIDTH:rw_cols], zeros(MISC_GATE - n_lora), w_in[:, kv0 + 6 * NSA_KV:],
        zeros(MISC_W - MISC_GATE - n_gate)], axis=1).astype(BF16)
    u, kv_bf = _inproj(x2, row(mix_norm_g), w_p)
    u3 = u.reshape(b, t, IN_COLS)
    y_rw = _rwkv(u3, *_rwkv_params(*rw))
    y_nsa = _nsa(u3, kv_bf.reshape(b, t, 6 * NSA_KV), *nsa)
    out = _tail(x2, y_rw.reshape(n, RW_WIDTH), y_nsa.reshape(n, NSA_WIDTH), p.reshape(n, PLE_DIM), w_out, moe_norm_g,
                router_w, router_b, moe_w1, moe_b1, moe_w2, moe_b2, ple_norm_g, ple_w, ple_gate_w)
    return out.reshape(b, t, d)


def _tail(x2, y_rw, y_nsa, p2, w_out, moe_norm_g, router_w, router_b, moe_w1, moe_b1, moe_w2, moe_b2,
          ple_norm_g, ple_w, ple_gate_w):
    n, d = x2.shape
    row = lambda v: v.reshape(1, -1).astype(F32)
    rw_f = jnp.zeros((d, LANES), F32).at[:, :N_EXPERTS].set(router_w)
    rw_hi = rw_f.astype(BF16)
    rw_p = jnp.stack([rw_hi, (rw_f - rw_hi.astype(F32)).astype(BF16)])
    rb_p = jnp.full((1, LANES), NEG, F32).at[0, :N_EXPERTS].set(router_b)
    h1, xn, top_e, gate, rank, counts = _outproj(x2, y_rw, y_nsa, w_out.astype(BF16), row(moe_norm_g), rw_p, rb_p)
    counts = counts[0, :N_EXPERTS]
    max_tiles = (n * TOP_K) // MOE_TR + N_EXPERTS
    n_rows = max_tiles * MOE_TR
    sched1, pstart, n_act = _schedule(counts, MOE_TR, D_EXPERT // MOE_TN1, max_tiles)
    sched2, _, _ = _schedule(counts, MOE_TR, d // MOE_TN2, max_tiles)
    top_e = top_e[:, :TOP_K]
    own = top_e[:, :, None] == jnp.arange(N_EXPERTS, dtype=I32)[None, None, :]
    dest = (jnp.sum(jnp.where(own, pstart[None, None, :], 0), axis=2) + rank[:, :TOP_K]).astype(I32)
    tok = jnp.broadcast_to(jnp.arange(n, dtype=I32)[:, None], (n, TOP_K))
    row_tok = jnp.zeros((n_rows,), I32).at[dest.reshape(-1)].set(tok.reshape(-1), unique_indices=True,
                                                                mode="promise_in_bounds")
    x_rows = _gather_rows(row_tok, n_act.reshape(1).astype(I32), xn, n_rows)
    y_rows = _grouped_mlp(x_rows, sched1, sched2, moe_w1, moe_b1, moe_w2, moe_b2, MOE_TR, MOE_TN1, MOE_TN2)
    return _combine_ple(dest.reshape(-1), h1, gate, p2, y_rows, row(ple_norm_g),
                        ple_gate_w.astype(BF16), ple_w.astype(BF16))


def kernel(x, p, mix_norm_g, w_in, rw_mu, rw_w0, rw_w2, rw_a0, rw_a2, rw_g2, rw_k_k, rw_k_a, rw_r_k, rw_lnx_w,
           rw_lnx_b, nsa_q_norm, nsa_k_norm, cmp_pos_k, cmp_pos_v, cmp_k_w1, cmp_k_b1, cmp_k_w2, cmp_k_b2, cmp_v_w1,
           cmp_v_b1, cmp_v_w2, cmp_v_b2, w_out, moe_norm_g, router_w, router_b, moe_w1, moe_b1, moe_w2, moe_b2,
           ple_norm_g, ple_w, ple_gate_w):
    h = x
    for i in range(mix_norm_g.shape[0]):
        rw = (rw_mu[i], rw_w0[i], rw_w2[i], rw_a0[i], rw_a2[i], rw_g2[i], rw_k_k[i], rw_k_a[i], rw_r_k[i],
              rw_lnx_w[i], rw_lnx_b[i])
        nsa = (nsa_q_norm[i], nsa_k_norm[i], cmp_pos_k[i], cmp_pos_v[i], cmp_k_w1[i], cmp_k_b1[i], cmp_k_w2[i],
               cmp_k_b2[i], cmp_v_w1[i], cmp_v_b1[i], cmp_v_w2[i], cmp_v_b2[i])
        h = _layer(h, p[i], mix_norm_g[i], w_in[i], rw, nsa, w_out[i], moe_norm_g[i], router_w[i], router_b[i],
                   moe_w1[i], moe_b1[i], moe_w2[i], moe_b2[i], ple_norm_g[i], ple_w[i], ple_gate_w[i])
    return h
```

```python
import functools

import jax
import jax.numpy as jnp
from jax import lax
from jax.experimental import pallas as pl
from jax.experimental.pallas import tpu as pltpu

F32 = jnp.float32
BF16 = jnp.bfloat16
I32 = jnp.int32
HI = lax.Precision.HIGHEST

LANES = 128
VMEM_LIMIT = 56 * 1024 * 1024

D_MODEL = 2048
RMS_EPS = 1e-6
RW_WIDTH = 1024
RW_HEAD = 64
RW_LORA = (64, 64, 160)
RW_GN_EPS = 64e-5
NSA_WIDTH = 1024
NSA_HEAD = 64
NSA_G = 4
NSA_R = 4
NSA_KV = NSA_G * NSA_HEAD
CMP_LEN = 32
CMP_STRIDE = 16
CMP_HIDDEN = 256
SLC_BLOCK = 64
N_SELECT = 16
WINDOW = 512
N_EXPERTS = 32
TOP_K = 4
D_EXPERT = 2048
SWIGLU_LIMIT = 7.0
SWIGLU_ALPHA = 1.702
PLE_DIM = 256

COL_RKV = 0
COL_Q = 3072
COL_KV = 4096
COL_MISC = 5632
MISC_W = 512
MISC_GATE = 384
IN_COLS = 6144

RW_HG = 4
RW_LW = RW_HG * RW_HEAD
RW_C = 64
NSA_TQ = 256
NSA_KT = 512
NSA_KWT = 256
NEG = -1e30
LOG2E = 1.4426950408889634


def _cp(sem, vmem=VMEM_LIMIT):
    return pltpu.CompilerParams(dimension_semantics=sem, vmem_limit_bytes=vmem)


def _dot(a, b, prec=None):
    return jnp.dot(a, b, preferred_element_type=F32, precision=prec)


def _dot_nt(a, b, prec=None):
    return lax.dot_general(a, b, (((1,), (1,)), ((), ())), preferred_element_type=F32, precision=prec)


def _dot_split(x, m, pieces=2, left=False):
    out, rem = None, x
    for _ in range(pieces):
        part = rem.astype(BF16)
        rem = rem - part.astype(F32)
        term = _dot(m, part) if left else _dot(part, m)
        out = term if out is None else out + term
    return out


def _iota(shape, dim):
    return lax.broadcasted_iota(I32, shape, dim)


def _seg_matrix(n, seg, scale):
    same = (_iota((n, n), 0) // seg) == (_iota((n, n), 1) // seg)
    return jnp.where(same, scale, 0.0).astype(BF16)


def _inproj_kernel(x_ref, g_ref, w_ref, o_ref, kv_ref, xn_ref, *, kv_lo, kv_hi):
    j = pl.program_id(1)

    @pl.when(j == 0)
    def _():
        x = x_ref[...]
        ms = jnp.mean(x * x, axis=-1, keepdims=True)
        xn_ref[...] = (x * lax.rsqrt(ms + RMS_EPS) * g_ref[...]).astype(BF16)

    u = _dot(xn_ref[...], w_ref[...])
    o_ref[...] = u

    @pl.when((j >= kv_lo) & (j < kv_hi))
    def _():
        kv_ref[...] = u.astype(kv_ref.dtype)


def _inproj(x2, g, w_bf, tm=1024, tn=512):
    n, d = x2.shape
    nc = w_bf.shape[1]
    kv_lo, kv_hi = COL_KV // tn, (COL_KV + 6 * NSA_KV) // tn
    kv_col = lambda i, j: (i, jnp.clip(j, kv_lo, kv_hi - 1) - kv_lo)
    return pl.pallas_call(
        functools.partial(_inproj_kernel, kv_lo=kv_lo, kv_hi=kv_hi),
        out_shape=(jax.ShapeDtypeStruct((n, nc), F32), jax.ShapeDtypeStruct((n, 6 * NSA_KV), BF16)),
        grid=(n // tm, nc // tn),
        in_specs=[
            pl.BlockSpec((tm, d), lambda i, j: (i, 0)),
            pl.BlockSpec((1, d), lambda i, j: (0, 0)),
            pl.BlockSpec((d, tn), lambda i, j: (0, j)),
        ],
        out_specs=(pl.BlockSpec((tm, tn), lambda i, j: (i, j)), pl.BlockSpec((tm, tn), kv_col)),
        scratch_shapes=[pltpu.VMEM((tm, d), BF16)],
        compiler_params=_cp(("arbitrary", "arbitrary")),
        name="inproj",
    )(x2, g, w_bf)


def _softplus(z):
    return jnp.maximum(z, 0.0) + jnp.log(1.0 + jnp.exp(-jnp.abs(z)))


def _rwkv_kernel(r_ref, k_ref, v_ref, m_ref, mur_ref, muk_ref, muv_ref, mum_ref, w2_ref, a2_ref, g2_ref,
                 w0_ref, a0_ref, kk_ref, ka_ref, rk_ref, lw_ref, lb_ref, o_ref,
                 st_ref, pr_ref, pk_ref, pv_ref, pm_ref, *, nchunk):
    tt = nchunk * RW_C
    lw = RW_LW

    @pl.when(pl.program_id(2) == 0)
    def _():
        st_ref[...] = jnp.zeros_like(st_ref)
        pr_ref[...] = jnp.zeros_like(pr_ref)
        pk_ref[...] = jnp.zeros_like(pk_ref)
        pv_ref[...] = jnp.zeros_like(pv_ref)
        pm_ref[...] = jnp.zeros_like(pm_ref)

    def shift_lerp(x, prev_ref, mu):
        rolled = pltpu.roll(x, 1, 0)
        first = _iota(x.shape, 0) == 0
        sh = jnp.where(first, prev_ref[0:1, :], rolled)
        prev_ref[0:1, :] = x[tt - 1:tt, :]
        return x + (sh - x) * mu

    r = shift_lerp(r_ref[0], pr_ref, mur_ref[...])
    k = shift_lerp(k_ref[0], pk_ref, muk_ref[...])
    v = shift_lerp(v_ref[0], pv_ref, muv_ref[...])
    m = shift_lerp(m_ref[0], pm_ref, mum_ref[...])

    lw_dec = _dot(jnp.tanh(m).astype(BF16), w2_ref[...])
    lw_a = _dot(m.astype(BF16), a2_ref[...])
    g = _dot(jax.nn.sigmoid(m).astype(BF16), g2_ref[...])
    w_log = -_softplus(-(w0_ref[...] + lw_dec)) - 0.5
    ld = -jnp.exp(w_log)
    a = jax.nn.sigmoid(a0_ref[...] + lw_a)

    ones_seg = _seg_matrix(lw, RW_HEAD, 1.0)
    kk = k * kk_ref[...]
    kk = kk / jnp.maximum(jnp.sqrt(_dot_split(kk * kk, ones_seg)), 1e-12)
    k2 = k * (1.0 + (a - 1.0) * ka_ref[...])
    av = -kk
    bv = kk * a

    ri = _iota((lw, lw), 0)
    ci = _iota((lw, lw), 1)
    same_blk = (ri // RW_C) == (ci // RW_C)
    m_strict = same_blk & (ci < ri)
    m_lower = same_blk & (ci <= ri)
    eye = (ri == ci).astype(F32)
    tri = (_iota((RW_C, RW_C), 1) <= _iota((RW_C, RW_C), 0)).astype(BF16)
    lane_head = _iota((1, lw), 1) // RW_HEAD
    hmask = [(lane_head == h).astype(F32) for h in range(RW_HG)]

    def stack_heads(x):
        return jnp.concatenate([x * hmask[h] for h in range(RW_HG)], axis=0)

    def unstack_heads(xs):
        out = xs[0:RW_C] * hmask[0]
        for h in range(1, RW_HG):
            out = out + xs[h * RW_C:(h + 1) * RW_C] * hmask[h]
        return out

    def fold_heads(xs):
        out = xs[0:RW_C]
        for h in range(1, RW_HG):
            out = out + xs[h * RW_C:(h + 1) * RW_C]
        return out

    chunks = []
    for c in range(nchunk):
        sl = slice(c * RW_C, (c + 1) * RW_C)
        ld_c = ld[sl]
        cs = _dot_split(ld_c, tri, pieces=3, left=True)
        w_in = jnp.exp(cs)
        w_inv = jnp.exp(-cs)
        w_prev = jnp.exp(cs - ld_c)
        w_last = w_in[RW_C - 1:RW_C, :]
        bt = bv[sl] * w_inv
        kt = k2[sl] * w_inv
        at_s = stack_heads(av[sl] * w_prev)
        rt_s = stack_heads(r[sl] * w_in)
        v_c = v[sl]
        chunks.append(dict(
            at_s=at_s, rt_s=rt_s, at_b=at_s.astype(BF16), rt_b=rt_s.astype(BF16), w_last=w_last,
            bt4=jnp.concatenate([bt] * RW_HG, axis=0).astype(BF16),
            kt4=jnp.concatenate([kt] * RW_HG, axis=0).astype(BF16),
            v4=jnp.concatenate([v_c] * RW_HG, axis=0).astype(BF16),
            bp=(bt * w_last).astype(BF16),
            sv=jnp.where(same_blk, _dot(v_c.T.astype(BF16), (kt * w_last).astype(BF16)), 0.0)))
    for ch in chunks:
        ch["a_ab"] = jnp.where(m_strict, _dot_nt(ch["at_b"], ch["bt4"]), 0.0)
        ch["a_ak"] = jnp.where(m_strict, _dot_nt(ch["at_b"], ch["kt4"]), 0.0).astype(BF16)
        ch["a_rb"] = jnp.where(m_lower, _dot_nt(ch["rt_b"], ch["bt4"]), 0.0).astype(BF16)
        ch["a_rk"] = jnp.where(m_lower, _dot_nt(ch["rt_b"], ch["kt4"]), 0.0).astype(BF16)
        ch["pw"] = ch["a_ab"]
        ch["inv"] = eye + ch["a_ab"]
    for _ in range(RW_C.bit_length() - 2):
        for ch in chunks:
            pw_b = ch["pw"].astype(BF16)
            ch["pw"] = _dot(pw_b, pw_b)
        for ch in chunks:
            ch["inv"] = ch["inv"] + _dot(ch["inv"].astype(BF16), ch["pw"].astype(BF16))
    for ch in chunks:
        inv_b = ch["inv"].astype(BF16)
        ia = _dot(inv_b, ch["at_b"])
        u_const = _dot(inv_b, _dot(ch["a_ak"], ch["v4"]).astype(BF16))
        g_mat = ch["rt_s"] + _dot(ch["a_rb"], ia.astype(BF16))
        y_const = _dot(ch["a_rb"], u_const.astype(BF16)) + _dot(ch["a_rk"], ch["v4"])
        ch["ia_u"] = fold_heads(ia).astype(BF16)
        ch["g_u"] = fold_heads(g_mat).astype(BF16)
        ch["u_const_t"] = unstack_heads(u_const).T
        ch["y_const"] = unstack_heads(y_const)

    ys = []
    st = st_ref[...]
    for ch in chunks:
        st_b = st.astype(BF16)
        ys.append(_dot_nt(ch["g_u"], st_b) + ch["y_const"])
        u_t = _dot_nt(st_b, ch["ia_u"]) + ch["u_const_t"]
        st = st * ch["w_last"] + jnp.where(same_blk, _dot(u_t.astype(BF16), ch["bp"]), 0.0) + ch["sv"]
    st_ref[...] = st
    y = jnp.concatenate(ys, axis=0) if nchunk > 1 else ys[0]

    avg_seg = _seg_matrix(lw, RW_HEAD, 1.0 / RW_HEAD)
    mean = _dot_split(y, avg_seg)
    yc = y - mean
    var = _dot_split(yc * yc, avg_seg)
    yn = yc * lax.rsqrt(var + RW_GN_EPS) * lw_ref[...] + lb_ref[...]
    bonus = _dot_split(r * k2 * rk_ref[...], ones_seg) * v
    o_ref[0] = ((yn + bonus) * g).astype(o_ref.dtype)


def _rwkv(u3, mu_p, w2p, a2p, g2p, w0, a0, k_k, k_a, r_k, lnx_w, lnx_b, nchunk=8):
    b, t, _ = u3.shape
    tt = nchunk * RW_C
    lw = RW_LW
    nhg = RW_WIDTH // lw
    col = lambda base: (lambda bi, h, ti: (bi, ti, base + h))
    vec = lambda base: (lambda bi, h, ti: (0, base + h))
    par = pl.BlockSpec((1, lw), vec(0))
    return pl.pallas_call(
        functools.partial(_rwkv_kernel, nchunk=nchunk),
        out_shape=jax.ShapeDtypeStruct((b, t, RW_WIDTH), BF16),
        grid=(b, nhg, t // tt),
        in_specs=[
            pl.BlockSpec((1, tt, lw), col(0)),
            pl.BlockSpec((1, tt, lw), col(nhg)),
            pl.BlockSpec((1, tt, lw), col(2 * nhg)),
            pl.BlockSpec((1, tt, MISC_W), lambda bi, h, ti: (bi, ti, COL_MISC // MISC_W)),
            pl.BlockSpec((1, lw), vec(0)),
            pl.BlockSpec((1, lw), vec(nhg)),
            pl.BlockSpec((1, lw), vec(2 * nhg)),
            pl.BlockSpec((1, MISC_W), lambda bi, h, ti: (0, COL_MISC // MISC_W)),
            pl.BlockSpec((MISC_W, lw), lambda bi, h, ti: (0, h)),
            pl.BlockSpec((MISC_W, lw), lambda bi, h, ti: (0, h)),
            pl.BlockSpec((MISC_W, lw), lambda bi, h, ti: (0, h)),
            par, par, par, par, par, par, par,
        ],
        out_specs=pl.BlockSpec((1, tt, lw), lambda bi, h, ti: (bi, ti, h)),
        scratch_shapes=[
            pltpu.VMEM((lw, lw), F32),
            pltpu.VMEM((8, lw), F32),
            pltpu.VMEM((8, lw), F32),
            pltpu.VMEM((8, lw), F32),
            pltpu.VMEM((8, MISC_W), F32),
        ],
        compiler_params=_cp(("arbitrary", "arbitrary", "arbitrary")),
        name="rwkv7",
    )(u3, u3, u3, u3, mu_p, mu_p, mu_p, mu_p, w2p, a2p, g2p, w0, a0, k_k, k_a, r_k, lnx_w, lnx_b)


def _pad_rows(w, rows, offset):
    return jnp.zeros((rows, w.shape[1]), w.dtype).at[offset:offset + w.shape[0]].set(w)


def _rwkv_params(mu, w0, w2, a0, a2, g2, k_k, k_a, r_k, lnx_w, lnx_b):
    mu_p = jnp.zeros((1, IN_COLS), F32)
    mu_p = mu_p.at[0, COL_RKV:COL_RKV + 3 * RW_WIDTH].set(mu[:3 * RW_WIDTH])
    mu_p = mu_p.at[0, COL_MISC:COL_MISC + sum(RW_LORA)].set(mu[3 * RW_WIDTH:])
    o1, o2 = RW_LORA[0], RW_LORA[0] + RW_LORA[1]
    row = lambda p: p.reshape(1, RW_WIDTH).astype(F32)
    return (mu_p, _pad_rows(w2, MISC_W, 0).astype(BF16), _pad_rows(a2, MISC_W, o1).astype(BF16),
            _pad_rows(g2, MISC_W, o2).astype(BF16), row(w0), row(a0), row(k_k), row(k_a), row(r_k),
            row(lnx_w), row(lnx_b))


def _nsa_prep_kernel(q_ref, ks_ref, kw_ref, gt_ref, qg_ref, ksg_ref, kwg_ref, qo_ref, kso_ref, kwo_ref, go_ref):
    avg = _seg_matrix(NSA_KV, NSA_HEAD, 1.0 / NSA_HEAD)

    def head_norm(x, g):
        ms = _dot_split(x * x, avg)
        return x * lax.rsqrt(ms + RMS_EPS) * g

    scale = NSA_HEAD ** -0.5 * LOG2E
    for s in range(NSA_G):
        sl = slice(s * NSA_KV, (s + 1) * NSA_KV)
        q_t = (head_norm(q_ref[:, sl], qg_ref[...]) * scale).T
        for h in range(NSA_R):
            qo_ref[0, s, h] = q_t[h * NSA_HEAD:(h + 1) * NSA_HEAD, :].astype(qo_ref.dtype)
    kso_ref[...] = head_norm(ks_ref[...], ksg_ref[...]).astype(kso_ref.dtype)
    kwo_ref[...] = head_norm(kw_ref[...], kwg_ref[...]).astype(kwo_ref.dtype)
    go_ref[...] = jax.nn.sigmoid(gt_ref[...])


def _nsa_prep(u2, b, q_gain, ks_gain, kw_gain, tm=512):
    n = u2.shape[0]
    tiles = n // b // tm
    kvb = COL_KV // NSA_KV
    gain = pl.BlockSpec((1, NSA_KV), lambda i: (0, 0))
    return pl.pallas_call(
        _nsa_prep_kernel,
        out_shape=(jax.ShapeDtypeStruct((b, NSA_G, NSA_R, NSA_HEAD, n // b), BF16),
                   jax.ShapeDtypeStruct((n, NSA_KV), BF16),
                   jax.ShapeDtypeStruct((n, NSA_KV), BF16), jax.ShapeDtypeStruct((n, LANES), F32)),
        grid=(n // tm,),
        in_specs=[
            pl.BlockSpec((tm, NSA_WIDTH), lambda i: (i, COL_Q // NSA_WIDTH)),
            pl.BlockSpec((tm, NSA_KV), lambda i: (i, kvb + 2)),
            pl.BlockSpec((tm, NSA_KV), lambda i: (i, kvb + 4)),
            pl.BlockSpec((tm, LANES), lambda i: (i, (COL_MISC + MISC_GATE) // LANES)),
            gain, gain, gain,
        ],
        out_specs=(pl.BlockSpec((1, NSA_G, NSA_R, NSA_HEAD, tm), lambda i: (i // tiles, 0, 0, 0, i % tiles)),
                   pl.BlockSpec((tm, NSA_KV), lambda i: (i, 0)),
                   pl.BlockSpec((tm, NSA_KV), lambda i: (i, 0)), pl.BlockSpec((tm, LANES), lambda i: (i, 0))),
        compiler_params=_cp(("arbitrary",)),
        name="nsa_prep",
    )(u2, u2, u2, u2, q_gain, ks_gain, kw_gain)


def _nsa_cmp_kernel(zk_ref, zv_ref, pk_ref, pv_ref, kw1_ref, kb1_ref, kw2_ref, kb2_ref, vw1_ref, vb1_ref,
                    vw2_ref, vb2_ref, kn_ref, ko_ref, vo_ref):
    nz = zk_ref.shape[2]
    half = kw1_ref.shape[0] // 2

    def compress(z, pos, w1_ref, b1, w2, b2):
        top = _dot(z, w1_ref[0:half, :])
        bot = _dot(z, w1_ref[half:, :])
        c1 = _dot(jnp.broadcast_to(pos, (8, pos.shape[1])).astype(BF16), w1_ref[...])[0:1] + b1
        hid = jax.nn.gelu(top + pltpu.roll(bot, nz - 1, 0) + c1)
        return _dot(hid.astype(BF16), w2) + b2

    kc = compress(zk_ref[0, 0], pk_ref[...], kw1_ref, kb1_ref[...], kw2_ref[...], kb2_ref[...])
    ms = jnp.mean(kc * kc, axis=-1, keepdims=True)
    ko_ref[0, 0] = (kc * lax.rsqrt(ms + RMS_EPS) * kn_ref[...]).astype(ko_ref.dtype)
    vc = compress(zv_ref[0, 0], pv_ref[...], vw1_ref, vb1_ref[...], vw2_ref[...], vb2_ref[...])
    vo_ref[0, 0] = vc.astype(vo_ref.dtype)


def _nsa_cmp(zk, zv, pos_k, pos_v, kw1, kb1, kw2, kb2, vw1, vb1, vw2, vb2, kn0):
    b, g, nz, zw = zk.shape
    full = lambda a: pl.BlockSpec(a.shape, lambda bi, gi: (0,) * a.ndim)
    zspec = pl.BlockSpec((1, 1, nz, zw), lambda bi, gi: (bi, gi, 0, 0))
    ospec = pl.BlockSpec((1, 1, nz, NSA_HEAD), lambda bi, gi: (bi, gi, 0, 0))
    params = (pos_k, pos_v, kw1, kb1, kw2, kb2, vw1, vb1, vw2, vb2, kn0)
    return pl.pallas_call(
        _nsa_cmp_kernel,
        out_shape=(jax.ShapeDtypeStruct((b, g, nz, NSA_HEAD), BF16),) * 2,
        grid=(b, g),
        in_specs=[zspec, zspec] + [full(a) for a in params],
        out_specs=(ospec, ospec),
        compiler_params=_cp(("arbitrary", "arbitrary")),
        name="nsa_compress",
    )(zk, zv, *params)


def _nsa_select_kernel(q_ref, kc_ref, vc_ref, oc_ref, sel_ref, *, n_sel):
    tq = q_ref.shape[4]
    nz = kc_ref.shape[2]
    ns = sel_ref.shape[2]
    t0 = pl.program_id(2) * tq
    n_idx = _iota((nz, tq), 0)
    t_idx = t0 + _iota((nz, tq), 1)
    cmask = (n_idx * CMP_STRIDE + (CMP_LEN - 1) <= t_idx) & (n_idx < nz - 1)
    kc = kc_ref[0, 0]
    vc_t = vc_ref[0, 0]
    psum = jnp.zeros((nz, tq), F32)
    for r in range(NSA_R):
        s = jnp.where(cmask, _dot(kc, q_ref[0, 0, r]), -jnp.inf)
        mx = jnp.max(s, axis=0, keepdims=True)
        e = jnp.exp2(s - jnp.where(mx == -jnp.inf, 0.0, mx))
        p = e / jnp.maximum(jnp.sum(e, axis=0, keepdims=True), 1e-30)
        psum = psum + p
        oc_ref[0, 0, r] = _dot(vc_t, p.astype(BF16))
    jb = _iota((ns, nz), 0) * SLC_BLOCK
    cb = _iota((ns, nz), 1) * CMP_STRIDE
    overlap = ((cb < jb + SLC_BLOCK) & (cb + CMP_LEN > jb) & (_iota((ns, nz), 1) < nz - 1)).astype(BF16)
    imp = _dot_split(psum, overlap, pieces=3, left=True)
    j_idx = _iota((ns, tq), 0)
    t_q = t0 + _iota((ns, tq), 1)
    qblk = t_q // SLC_BLOCK
    forced = (j_idx == 0) | (j_idx == qblk) | (j_idx == qblk - 1)
    imp = jnp.where(forced, jnp.inf, imp)
    imp = jnp.where(j_idx * SLC_BLOCK <= t_q, imp, -jnp.inf)
    sub = 8
    groups = [imp[g * sub:(g + 1) * sub, :] for g in range(ns // sub)]
    ranks = [jnp.zeros((sub, tq), F32) for _ in groups]
    for i in range(ns):
        row = imp[i:i + 1, :]
        for g, imp_g in enumerate(groups):
            if (g + 1) * sub - 1 < i:
                before = row > imp_g
            elif g * sub > i:
                before = row >= imp_g
            else:
                j_g = g * sub + _iota((sub, tq), 0)
                before = (row > imp_g) | ((row == imp_g) & (j_g > i))
            ranks[g] = ranks[g] + jnp.where(before, 1.0, 0.0)
    rank = jnp.concatenate(ranks, axis=0)
    sel_ref[0, 0] = ((rank < n_sel) & (imp > -jnp.inf)).astype(F32)


def _nsa_select(q_t, kc, vc_t, tq=NSA_TQ):
    b, g, r, dk, t = q_t.shape
    nz = kc.shape[2]
    ns = t // SLC_BLOCK
    qspec = pl.BlockSpec((1, 1, r, dk, tq), lambda bi, gi, i: (bi, gi, 0, 0, i))
    return pl.pallas_call(
        functools.partial(_nsa_select_kernel, n_sel=min(N_SELECT, ns)),
        out_shape=(jax.ShapeDtypeStruct((b, g, r, dk, t), F32), jax.ShapeDtypeStruct((b, g, ns, t), F32)),
        grid=(b, g, t // tq),
        in_specs=[
            qspec,
            pl.BlockSpec((1, 1, nz, dk), lambda bi, gi, i: (bi, gi, 0, 0)),
            pl.BlockSpec((1, 1, dk, nz), lambda bi, gi, i: (bi, gi, 0, 0)),
        ],
        out_specs=(qspec, pl.BlockSpec((1, 1, ns, tq), lambda bi, gi, i: (bi, gi, 0, i))),
        compiler_params=_cp(("arbitrary", "arbitrary", "arbitrary")),
        name="nsa_select",
    )(q_t, kc, vc_t)


def _nsa_attn_kernel(q_ref, ks_ref, vs_ref, kw_ref, vw_ref, sel_ref, oc_ref, gt_ref, o_ref, m_ref, acc_ref):
    r, dk, tq = q_ref.shape[2:]
    kt = vs_ref.shape[4]
    kwt = vw_ref.shape[4]
    t0 = pl.program_id(2) * tq
    q_t = jnp.concatenate([q_ref[0, 0, h] for h in range(r)], axis=1)
    tpos = t0 + _iota((1, tq), 1)

    def reset():
        m_ref[...] = jnp.full_like(m_ref, NEG)
        acc_ref[...] = jnp.zeros_like(acc_ref)

    def online_update(s, bias, v_aug):
        ps, alphas = [], []
        for h in range(r):
            s_h = s[:, h * tq:(h + 1) * tq] + bias
            m_old = m_ref[h]
            m_new = jnp.maximum(m_old, jnp.max(s_h, axis=0, keepdims=True))
            alphas.append(jnp.exp2(m_old - m_new))
            ps.append(jnp.exp2((s_h - m_new).astype(BF16)))
            m_ref[h] = m_new
        acc_ref[...] = jnp.concatenate(alphas, axis=1) * acc_ref[...] + _dot(v_aug, jnp.concatenate(ps, axis=1))

    def result():
        seen = jnp.concatenate([m_ref[h] for h in range(r)], axis=1) > 0.5 * NEG
        acc = acc_ref[...]
        return jnp.where(seen, acc[0:dk] / jnp.maximum(acc[dk:dk + 1], 1e-30), 0.0)

    reset()
    nb = kt // SLC_BLOCK

    def sel_body(j, carry):
        k = ks_ref[0, 0, pl.ds(pl.multiple_of(j * kt, kt), kt), :]
        s = _dot(k, q_t)
        bm = jnp.concatenate([jnp.broadcast_to(sel_ref[0, 0, pl.ds(j * nb + c, 1), :], (SLC_BLOCK, tq))
                              for c in range(nb)], axis=0)
        kpos = j * kt + _iota((kt, tq), 0)
        online_update(s, jnp.where((bm > 0.5) & (kpos <= tpos), 0.0, NEG), vs_ref[0, 0, j])
        return carry

    lax.fori_loop(0, (t0 + tq + kt - 1) // kt, sel_body, 0)
    o_s = result()

    reset()

    def win_body(j, carry):
        k = kw_ref[0, 0, pl.ds(pl.multiple_of(j * kwt, kwt), kwt), :]
        s = _dot(k, q_t)
        kpos = j * kwt + _iota((kwt, tq), 0)
        online_update(s, jnp.where((kpos <= tpos) & (kpos > tpos - WINDOW), 0.0, NEG), vw_ref[0, 0, j])
        return carry

    lo = jnp.maximum(t0 - (WINDOW - 1), 0) // kwt
    lax.fori_loop(lo, (t0 + tq - 1) // kwt + 1, win_body, 0)
    o_w = result()

    outs = []
    for h in range(r):
        sl = slice(h * tq, (h + 1) * tq)
        gt = gt_ref[0, 0, h]
        outs.append(gt[0:1] * oc_ref[0, 0, h] + gt[1:2] * o_s[:, sl] + gt[2:3] * o_w[:, sl])
    o_ref[0] = jnp.concatenate(outs, axis=0).T.astype(o_ref.dtype)


def _nsa_attn(q_t, ks, vs_t, kw, vw_t, sel_t, oc_t, gates_t, tq=NSA_TQ):
    b, g, r, dk, t = q_t.shape
    ns = sel_t.shape[2]
    qspec = pl.BlockSpec((1, 1, r, dk, tq), lambda bi, gi, i: (bi, gi, 0, 0, i))
    kspec = pl.BlockSpec((1, 1, t, dk), lambda bi, gi, i: (bi, gi, 0, 0))
    vspec = lambda v: pl.BlockSpec((1, 1) + v.shape[2:], lambda bi, gi, i: (bi, gi, 0, 0, 0))
    return pl.pallas_call(
        _nsa_attn_kernel,
        out_shape=jax.ShapeDtypeStruct((b, t, g * r * dk), BF16),
        grid=(b, g, t // tq),
        in_specs=[
            qspec, kspec, vspec(vs_t), kspec, vspec(vw_t),
            pl.BlockSpec((1, 1, ns, tq), lambda bi, gi, i: (bi, gi, 0, i)),
            qspec,
            pl.BlockSpec((1, 1, r, 3, tq), lambda bi, gi, i: (bi, gi, 0, 0, i)),
        ],
        out_specs=pl.BlockSpec((1, tq, r * dk), lambda bi, gi, i: (bi, i, gi)),
        scratch_shapes=[pltpu.VMEM((r, 1, tq), F32), pltpu.VMEM((vs_t.shape[3], r * tq), F32)],
        compiler_params=_cp(("arbitrary", "arbitrary", "arbitrary")),
        name="nsa_attention",
    )(q_t, ks, vs_t, kw, vw_t, sel_t, oc_t, gates_t)


def _nsa(u3, kv_bf, q_norm, k_norm, pos_k, pos_v, ck_w1, ck_b1, ck_w2, ck_b2, cv_w1, cv_b1, cv_w2, cv_b2):
    b, t, _ = u3.shape
    n = b * t
    g, r, dk = NSA_G, NSA_R, NSA_HEAD
    tile_g = lambda p: jnp.tile(p.astype(F32), g).reshape(1, g * dk)
    q_t, ksn, kwn, gsig = _nsa_prep(u3.reshape(n, IN_COLS), b, tile_g(q_norm), tile_g(k_norm[1]), tile_g(k_norm[2]))
    per_group = lambda a: a.reshape(b, t, g, dk).transpose(0, 2, 1, 3)
    kv = lambda idx: kv_bf[..., idx * NSA_KV:(idx + 1) * NSA_KV]
    nz = t // CMP_STRIDE
    zk = per_group(kv(0)).reshape(b, g, nz, CMP_STRIDE * dk)
    zv = per_group(kv(1)).reshape(b, g, nz, CMP_STRIDE * dk)
    row = lambda p: p.reshape(1, -1).astype(F32)
    kc, vc = _nsa_cmp(zk, zv, row(pos_k), row(pos_v), ck_w1.astype(BF16), row(ck_b1), ck_w2.astype(BF16), row(ck_b2),
                      cv_w1.astype(BF16), row(cv_b1), cv_w2.astype(BF16), row(cv_b2), row(k_norm[0]))
    oc_t, sel_t = _nsa_select(q_t, kc, vc.transpose(0, 1, 3, 2))
    gates_t = gsig[:, :g * r * 3].reshape(b, t, g, r, 3).transpose(0, 2, 3, 4, 1)

    def tiled_t(a, tile):
        v_t = a.reshape(b, t // tile, tile, g, dk).transpose(0, 3, 1, 4, 2)
        extra = jnp.zeros(v_t.shape[:3] + (16, tile), BF16).at[:, :, :, 0, :].set(1.0)
        return jnp.concatenate([v_t, extra], axis=3)

    return _nsa_attn(q_t, per_group(ksn), tiled_t(kv(3), min(NSA_KT, t)), per_group(kwn),
                     tiled_t(kv(5), min(NSA_KWT, t)), sel_t, oc_t, gates_t)


def _outproj_kernel(x_ref, yr_ref, yn_ref, w_ref, g_ref, rw_ref, rb_ref,
                    h_ref, xn_ref, e_ref, gate_ref, rank_ref, cnt_ref, cnt_acc):
    tm = x_ref.shape[0]

    @pl.when(pl.program_id(0) == 0)
    def _():
        cnt_acc[...] = jnp.zeros_like(cnt_acc)

    h = x_ref[...] + _dot(yr_ref[...], w_ref[0:RW_WIDTH, :]) + _dot(yn_ref[...], w_ref[RW_WIDTH:, :])
    h_ref[...] = h
    ms = jnp.mean(h * h, axis=-1, keepdims=True)
    xn = h * lax.rsqrt(ms + RMS_EPS) * g_ref[...]
    xn_ref[...] = xn
    xn_hi = xn.astype(BF16)
    xn_lo = (xn - xn_hi.astype(F32)).astype(BF16)
    logits = (_dot(xn_hi, rw_ref[0]) + _dot(xn_lo, rw_ref[0]) + _dot(xn_hi, rw_ref[1])
              + rb_ref[...])
    lane = _iota((tm, LANES), 1).astype(F32)
    work = logits
    top_e, top_v = [], []
    for _ in range(TOP_K):
        mx = jnp.max(work, axis=1, keepdims=True)
        idx = jnp.min(jnp.where(work == mx, lane, float(LANES)), axis=1, keepdims=True)
        top_e.append(idx)
        top_v.append(mx)
        work = jnp.where(lane == idx, -jnp.inf, work)
    ex = [jnp.exp(v - top_v[0]) for v in top_v]
    den = ex[0] + ex[1] + ex[2] + ex[3]
    multihot = jnp.zeros((tm, LANES), F32)
    for e in top_e:
        multihot = multihot + (lane == e).astype(F32)
    strict = (_iota((tm, tm), 1) < _iota((tm, tm), 0)).astype(BF16)
    before = cnt_acc[...] + _dot(strict, multihot.astype(BF16))
    e_out = jnp.zeros((tm, LANES), F32)
    g_out = jnp.zeros((tm, LANES), F32)
    r_out = jnp.zeros((tm, LANES), F32)
    for k in range(TOP_K):
        slot = lane == float(k)
        rank_k = jnp.sum(jnp.where(lane == top_e[k], before, 0.0), axis=1, keepdims=True)
        e_out = jnp.where(slot, top_e[k], e_out)
        g_out = jnp.where(slot, ex[k] / den, g_out)
        r_out = jnp.where(slot, rank_k, r_out)
    e_ref[...] = e_out.astype(I32)
    gate_ref[...] = g_out
    rank_ref[...] = r_out.astype(I32)
    cnt_acc[...] = cnt_acc[...] + jnp.sum(multihot, axis=0, keepdims=True)
    cnt_ref[...] = cnt_acc[...].astype(I32)


def _outproj(x2, y_rw, y_nsa, w_out_bf, g, rw_p, rb_p, tm=256):
    n, d = x2.shape
    rowblk = lambda w: pl.BlockSpec((tm, w), lambda i: (i, 0))
    full = lambda a: pl.BlockSpec(a.shape, lambda i: (0,) * a.ndim)
    return pl.pallas_call(
        _outproj_kernel,
        out_shape=(jax.ShapeDtypeStruct((n, d), F32), jax.ShapeDtypeStruct((n, d), F32),
                   jax.ShapeDtypeStruct((n, LANES), I32), jax.ShapeDtypeStruct((n, LANES), F32),
                   jax.ShapeDtypeStruct((n, LANES), I32), jax.ShapeDtypeStruct((1, LANES), I32)),
        grid=(n // tm,),
        in_specs=[rowblk(d), rowblk(RW_WIDTH), rowblk(NSA_WIDTH), full(w_out_bf), full(g), full(rw_p), full(rb_p)],
        out_specs=(rowblk(d), rowblk(d), rowblk(LANES), rowblk(LANES), rowblk(LANES),
                   pl.BlockSpec((1, LANES), lambda i: (0, 0))),
        scratch_shapes=[pltpu.VMEM((1, LANES), F32)],
        compiler_params=_cp(("arbitrary",)),
        name="outproj_router",
    )(x2, y_rw, y_nsa, w_out_bf, g, rw_p, rb_p)


GATHER_UNROLL = 8
MOE_ROW_SPLIT = 2


def _gather_kernel(row_tok, n_act, x_hbm, o_ref, buf, sem):
    tg = o_ref.shape[0]
    base = pl.program_id(0) * tg

    @pl.when(base < n_act[0])
    def _():
        def issue(r8, c):
            for u in range(GATHER_UNROLL):
                r = r8 * GATHER_UNROLL + u
                pltpu.make_async_copy(x_hbm.at[pl.ds(row_tok[base + r], 1), :], buf.at[pl.ds(r, 1), :],
                                      sem).start(priority=u % 2)
            return c

        lax.fori_loop(0, tg // GATHER_UNROLL, issue, 0)
        pltpu.make_async_copy(x_hbm.at[pl.ds(0, tg), :], buf, sem).wait()
        o_ref[...] = buf[...].astype(o_ref.dtype)

    @pl.when(base >= n_act[0])
    def _():
        o_ref[...] = jnp.zeros_like(o_ref)


def _gather_rows(row_tok, n_act, xn, n_rows, tg=512):
    d = xn.shape[1]
    return pl.pallas_call(
        _gather_kernel,
        out_shape=jax.ShapeDtypeStruct((n_rows, d), BF16),
        grid_spec=pltpu.PrefetchScalarGridSpec(
            num_scalar_prefetch=2,
            grid=(n_rows // tg,),
            in_specs=[pl.BlockSpec(memory_space=pl.ANY)],
            out_specs=pl.BlockSpec((tg, d), lambda i, rt, na: (i, 0)),
            scratch_shapes=[pltpu.VMEM((tg, d), xn.dtype), pltpu.SemaphoreType.DMA],
        ),
        compiler_params=_cp(("arbitrary",)),
        name="moe_gather",
    )(row_tok, n_act, xn)


def _stream_weights(s, sched, copies, use):
    e_s, j_s, first_s, grp_s, ne_s, nj_s, hn_s = sched

    @pl.when(first_s[s] == 1)
    def _():
        slot = grp_s[s] % 2

        @pl.when(s == 0)
        def _():
            for c in copies(e_s[0], j_s[0], 0):
                c.start()

        for c in copies(e_s[s], j_s[s], slot):
            c.wait()

        @pl.when(hn_s[s] == 1)
        def _():
            for c in copies(ne_s[s], nj_s[s], 1 - slot):
                c.start()

        use(slot)


def _gm1_kernel(e_s, j_s, rt_s, valid_s, first_s, rto_s, jo_s, grp_s, ne_s, nj_s, hn_s,
                x_ref, w_hbm, bg_ref, bl_ref, o_ref, wf, wgb, wlb, sems):
    s = pl.program_id(0)
    tn = wgb.shape[1]
    nt = w_hbm.shape[2] // (2 * tn)

    def copies(e, j, slot):
        return [pltpu.make_async_copy(w_hbm.at[e, :, pl.ds(pl.multiple_of((half * nt + j) * tn, tn), tn)],
                                      wf.at[slot, half], sems.at[slot]) for half in range(2)]

    def use(slot):
        wgb[...] = wf[slot, 0].astype(BF16)
        wlb[...] = wf[slot, 1].astype(BF16)

    @pl.when(valid_s[s] == 1)
    def _():
        _stream_weights(s, (e_s, j_s, first_s, grp_s, ne_s, nj_s, hn_s), copies, use)

        half = x_ref.shape[0] // MOE_ROW_SPLIT
        for part in range(MOE_ROW_SPLIT):
            rows = slice(part * half, (part + 1) * half)
            x = x_ref[rows, :]
            gt = jnp.minimum(_dot(x, wgb[...]) + bg_ref[0], SWIGLU_LIMIT)
            lin = jnp.clip(_dot(x, wlb[...]) + bl_ref[0], -SWIGLU_LIMIT, SWIGLU_LIMIT)
            o_ref[rows, :] = (gt * jax.nn.sigmoid(SWIGLU_ALPHA * gt) * (lin + 1.0)).astype(o_ref.dtype)

    @pl.when(valid_s[s] == 0)
    def _():
        o_ref[...] = jnp.zeros_like(o_ref)


def _gm2_kernel(e_s, j_s, rt_s, valid_s, first_s, rto_s, jo_s, grp_s, ne_s, nj_s, hn_s,
                x_ref, w_hbm, b_ref, o_ref, wf, wb, sems):
    s = pl.program_id(0)
    tn = wb.shape[1]

    def copies(e, j, slot):
        return [pltpu.make_async_copy(w_hbm.at[e, :, pl.ds(pl.multiple_of(j * tn, tn), tn)], wf.at[slot],
                                      sems.at[slot])]

    def use(slot):
        wb[...] = wf[slot].astype(BF16)

    @pl.when(valid_s[s] == 1)
    def _():
        _stream_weights(s, (e_s, j_s, first_s, grp_s, ne_s, nj_s, hn_s), copies, use)

        half = x_ref.shape[0] // MOE_ROW_SPLIT
        for part in range(MOE_ROW_SPLIT):
            rows = slice(part * half, (part + 1) * half)
            o_ref[rows, :] = _dot(x_ref[rows, :], wb[...]) + b_ref[0]

    @pl.when(valid_s[s] == 0)
    def _():
        o_ref[...] = jnp.zeros_like(o_ref)


def _schedule(counts, tr, n_col_tiles, max_tiles):
    tiles_e = (counts + tr - 1) // tr
    tile_end = jnp.cumsum(tiles_e)
    tile_start = tile_end - tiles_e
    steps_e = tiles_e * n_col_tiles
    step_end = jnp.cumsum(steps_e)
    total = step_end[-1]
    n_steps = max_tiles * n_col_tiles
    step = jnp.arange(n_steps, dtype=I32)
    s = jnp.minimum(step, total - 1)
    e = jnp.sum((step_end[None, :] <= s[:, None]).astype(I32), axis=1)
    own = (jnp.arange(N_EXPERTS, dtype=I32)[None, :] == e[:, None]).astype(I32)
    of_e = lambda per_expert: jnp.sum(own * per_expert[None, :], axis=1)
    local = s - of_e(step_end - steps_e)
    te = jnp.maximum(of_e(tiles_e), 1)
    j = (local // te).astype(I32)
    rt = (of_e(tile_start) + local % te).astype(I32)
    valid = step < total
    first = ((local % te) == 0).astype(I32)
    spare = step - total
    rt_out = jnp.where(valid, rt, tile_end[-1] + spare // n_col_tiles).astype(I32)
    j_out = jnp.where(valid, j, spare % n_col_tiles).astype(I32)
    ar = jnp.arange(N_EXPERTS, dtype=I32)
    has = tiles_e > 0
    later = jnp.min(jnp.where((ar[None, :] > ar[:, None]) & has[None, :], ar[None, :], N_EXPERTS), axis=1)
    grp = (of_e(jnp.cumsum(has.astype(I32)) - has.astype(I32)) * n_col_tiles + j).astype(I32)
    last_col = j == n_col_tiles - 1
    next_e = jnp.where(last_col, jnp.minimum(of_e(later), N_EXPERTS - 1), e).astype(I32)
    next_j = jnp.where(last_col, 0, j + 1).astype(I32)
    has_next = (jnp.where(last_col, of_e(later) < N_EXPERTS, True) & valid).astype(I32)
    sched = (e, j, rt, valid.astype(I32), first, rt_out, j_out, grp, next_e, next_j, has_next)
    return sched, tile_start * tr, tile_end[-1] * tr


def _grouped_mlp(x_rows, sched1, sched2, w1, b1, w2, b2, tr, tn1, tn2):
    n_rows, d = x_rows.shape
    de = w2.shape[1]
    nt1 = de // tn1
    b1r = b1.reshape(N_EXPERTS, 1, 2 * de)
    b2r = b2.reshape(N_EXPERTS, 1, d)
    x_map = lambda s, e, j, rt, *_: (rt[s], 0)
    out_map = lambda s, e, j, rt, v, f, ro, jo, *_: (ro[s], jo[s])
    hbm = pl.BlockSpec(memory_space=pl.ANY)
    act = pl.pallas_call(
        _gm1_kernel,
        out_shape=jax.ShapeDtypeStruct((n_rows, de), BF16),
        grid_spec=pltpu.PrefetchScalarGridSpec(
            num_scalar_prefetch=len(sched1),
            grid=(sched1[0].shape[0],),
            in_specs=[
                pl.BlockSpec((tr, d), x_map),
                hbm,
                pl.BlockSpec((1, 1, tn1), lambda s, e, j, *_: (e[s], 0, j[s])),
                pl.BlockSpec((1, 1, tn1), lambda s, e, j, *_: (e[s], 0, nt1 + j[s])),
            ],
            out_specs=pl.BlockSpec((tr, tn1), out_map),
            scratch_shapes=[pltpu.VMEM((2, 2, d, tn1), w1.dtype), pltpu.VMEM((d, tn1), BF16),
                            pltpu.VMEM((d, tn1), BF16), pltpu.SemaphoreType.DMA((2,))],
        ),
        compiler_params=_cp(("arbitrary",)),
        name="moe_up",
    )(*sched1, x_rows, w1, b1r, b1r)
    return pl.pallas_call(
        _gm2_kernel,
        out_shape=jax.ShapeDtypeStruct((n_rows, d), F32),
        grid_spec=pltpu.PrefetchScalarGridSpec(
            num_scalar_prefetch=len(sched2),
            grid=(sched2[0].shape[0],),
            in_specs=[
                pl.BlockSpec((tr, de), x_map),
                hbm,
                pl.BlockSpec((1, 1, tn2), lambda s, e, j, *_: (e[s], 0, j[s])),
            ],
            out_specs=pl.BlockSpec((tr, tn2), out_map),
            scratch_shapes=[pltpu.VMEM((2, de, tn2), w2.dtype), pltpu.VMEM((de, tn2), BF16),
                            pltpu.SemaphoreType.DMA((2,))],
        ),
        compiler_params=_cp(("arbitrary",)),
        name="moe_down",
    )(*sched2, act, w2, b2r)


def _ple_kernel(dest, h_ref, gate_ref, p_ref, y_hbm, g_ref, gw_ref, pw_ref, o_ref, buf, sems):
    tm = h_ref.shape[0]
    i = pl.program_id(0)
    slot = i % 2

    def fetch(step, s):
        def issue(t2, c):
            for u in range(2):
                t = t2 * 2 + u
                for k in range(TOP_K):
                    pltpu.make_async_copy(y_hbm.at[pl.ds(dest[(step * tm + t) * TOP_K + k], 1), :],
                                          buf.at[s, k, pl.ds(t, 1), :], sems.at[s]).start()
            return c

        lax.fori_loop(0, tm // 2, issue, 0)

    @pl.when(i == 0)
    def _():
        fetch(0, 0)

    @pl.when(i + 1 < pl.num_programs(0))
    def _():
        fetch(i + 1, 1 - slot)

    for k in range(TOP_K):
        pltpu.make_async_copy(y_hbm.at[pl.ds(0, tm), :], buf.at[slot, k], sems.at[slot]).wait()
    gate = gate_ref[...]
    h = h_ref[...]
    for k in range(TOP_K):
        h = h + gate[:, k:k + 1] * buf[slot, k]
    ms = jnp.mean(h * h, axis=-1, keepdims=True)
    hn = (h * lax.rsqrt(ms + RMS_EPS) * g_ref[...]).astype(BF16)
    pgate = jax.nn.sigmoid(_dot(hn, gw_ref[...]))
    o_ref[...] = h + pgate * _dot(p_ref[...].astype(BF16), pw_ref[...])


def _combine_ple(dest, h1, gate, p2, y_rows, g, gw_bf, pw_bf, tm=128):
    n, d = h1.shape
    rowblk = lambda w: pl.BlockSpec((tm, w), lambda i, ds: (i, 0))
    full = lambda a: pl.BlockSpec(a.shape, lambda i, ds: (0,) * a.ndim)
    return pl.pallas_call(
        _ple_kernel,
        out_shape=jax.ShapeDtypeStruct((n, d), F32),
        grid_spec=pltpu.PrefetchScalarGridSpec(
            num_scalar_prefetch=1,
            grid=(n // tm,),
            in_specs=[rowblk(d), rowblk(LANES), rowblk(p2.shape[1]), pl.BlockSpec(memory_space=pl.ANY),
                      full(g), full(gw_bf), full(pw_bf)],
            out_specs=rowblk(d),
            scratch_shapes=[pltpu.VMEM((2, TOP_K, tm, d), F32), pltpu.SemaphoreType.DMA((2,))],
        ),
        compiler_params=_cp(("arbitrary",)),
        name="combine_ple",
    )(dest, h1, gate, p2, y_rows, g, gw_bf, pw_bf)


MOE_TR = 512
MOE_TN1 = 512
MOE_TN2 = 1024


def _layer(x, p, mix_norm_g, w_in, rw, nsa, w_out, moe_norm_g, router_w, router_b, moe_w1, moe_b1, moe_w2, moe_b2,
           ple_norm_g, ple_w, ple_gate_w):
    b, t, d = x.shape
    n = b * t
    x2 = x.reshape(n, d)
    row = lambda v: v.reshape(1, -1).astype(F32)
    n_lora = sum(RW_LORA)
    rw_cols = 3 * RW_WIDTH + n_lora
    kv0 = rw_cols + NSA_WIDTH
    n_gate = 3 * NSA_G * NSA_R
    zeros = lambda w: jnp.zeros((d, w), w_in.dtype)
    w_p = jnp.concatenate([
        w_in[:, :3 * RW_WIDTH], w_in[:, rw_cols:kv0], w_in[:, kv0:kv0 + 6 * NSA_KV],
        w_in[:, 3 * RW_WIDTH:rw_cols], zeros(MISC_GATE - n_lora), w_in[:, kv0 + 6 * NSA_KV:],
        zeros(MISC_W - MISC_GATE - n_gate)], axis=1).astype(BF16)
    u, kv_bf = _inproj(x2, row(mix_norm_g), w_p)
    u3 = u.reshape(b, t, IN_COLS)
    y_rw = _rwkv(u3, *_rwkv_params(*rw))
    y_nsa = _nsa(u3, kv_bf.reshape(b, t, 6 * NSA_KV), *nsa)
    out = _tail(x2, y_rw.reshape(n, RW_WIDTH), y_nsa.reshape(n, NSA_WIDTH), p.reshape(n, PLE_DIM), w_out, moe_norm_g,
                router_w, router_b, moe_w1, moe_b1, moe_w2, moe_b2, ple_norm_g, ple_w, ple_gate_w)
    return out.reshape(b, t, d)


def _tail(x2, y_rw, y_nsa, p2, w_out, moe_norm_g, router_w, router_b, moe_w1, moe_b1, moe_w2, moe_b2,
          ple_norm_g, ple_w, ple_gate_w):
    n, d = x2.shape
    row = lambda v: v.reshape(1, -1).astype(F32)
    rw_f = jnp.zeros((d, LANES), F32).at[:, :N_EXPERTS].set(router_w)
    rw_hi = rw_f.astype(BF16)
    rw_p = jnp.stack([rw_hi, (rw_f - rw_hi.astype(F32)).astype(BF16)])
    rb_p = jnp.full((1, LANES), NEG, F32).at[0, :N_EXPERTS].set(router_b)
    h1, xn, top_e, gate, rank, counts = _outproj(x2, y_rw, y_nsa, w_out.astype(BF16), row(moe_norm_g), rw_p, rb_p)
    counts = counts[0, :N_EXPERTS]
    max_tiles = (n * TOP_K) // MOE_TR + N_EXPERTS
    n_rows = max_tiles * MOE_TR
    sched1, pstart, n_act = _schedule(counts, MOE_TR, D_EXPERT // MOE_TN1, max_tiles)
    sched2, _, _ = _schedule(counts, MOE_TR, d // MOE_TN2, max_tiles)
    top_e = top_e[:, :TOP_K]
    own = top_e[:, :, None] == jnp.arange(N_EXPERTS, dtype=I32)[None, None, :]
    dest = (jnp.sum(jnp.where(own, pstart[None, None, :], 0), axis=2) + rank[:, :TOP_K]).astype(I32)
    tok = jnp.broadcast_to(jnp.arange(n, dtype=I32)[:, None], (n, TOP_K))
    row_tok = jnp.zeros((n_rows,), I32).at[dest.reshape(-1)].set(tok.reshape(-1), unique_indices=True,
                                                                mode="promise_in_bounds")
    x_rows = _gather_rows(row_tok, n_act.reshape(1).astype(I32), xn, n_rows)
    y_rows = _grouped_mlp(x_rows, sched1, sched2, moe_w1, moe_b1, moe_w2, moe_b2, MOE_TR, MOE_TN1, MOE_TN2)
    return _combine_ple(dest.reshape(-1), h1, gate, p2, y_rows, row(ple_norm_g),
                        ple_gate_w.astype(BF16), ple_w.astype(BF16))


def kernel(x, p, mix_norm_g, w_in, rw_mu, rw_w0, rw_w2, rw_a0, rw_a2, rw_g2, rw_k_k, rw_k_a, rw_r_k, rw_lnx_w,
           rw_lnx_b, nsa_q_norm, nsa_k_norm, cmp_pos_k, cmp_pos_v, cmp_k_w1, cmp_k_b1, cmp_k_w2, cmp_k_b2, cmp_v_w1,
           cmp_v_b1, cmp_v_w2, cmp_v_b2, w_out, moe_norm_g, router_w, router_b, moe_w1, moe_b1, moe_w2, moe_b2,
           ple_norm_g, ple_w, ple_gate_w):
    h = x
    for i in range(mix_norm_g.shape[0]):
        rw = (rw_mu[i], rw_w0[i], rw_w2[i], rw_a0[i], rw_a2[i], rw_g2[i], rw_k_k[i], rw_k_a[i], rw_r_k[i],
              rw_lnx_w[i], rw_lnx_b[i])
        nsa = (nsa_q_norm[i], nsa_k_norm[i], cmp_pos_k[i], cmp_pos_v[i], cmp_k_w1[i], cmp_k_b1[i], cmp_k_w2[i],
               cmp_k_b2[i], cmp_v_w1[i], cmp_v_b1[i], cmp_v_w2[i], cmp_v_b2[i])
        h = _layer(h, p[i], mix_norm_g[i], w_in[i], rw, nsa, w_out[i], moe_norm_g[i], router_w[i], router_b[i],
                   moe_w1[i], moe_b1[i], moe_w2[i], moe_b2[i], ple_norm_g[i], ple_w[i], ple_gate_w[i])
    return h
```

```python
import functools

import jax
import jax.numpy as jnp
from jax import lax
from jax.experimental import pallas as pl
from jax.experimental.pallas import tpu as pltpu

F32 = jnp.float32
BF16 = jnp.bfloat16
I32 = jnp.int32

LANES = 128
VMEM_LIMIT = 56 * 1024 * 1024

RMS_EPS = 1e-6
RW_WIDTH = 1024
RW_HEAD = 64
RW_LORA = (64, 64, 160)
RW_GN_EPS = 64e-5
NSA_WIDTH = 1024
NSA_HEAD = 64
NSA_G = 4
NSA_R = 4
NSA_KV = NSA_G * NSA_HEAD
CMP_LEN = 32
CMP_STRIDE = 16
SLC_BLOCK = 64
N_SELECT = 16
WINDOW = 512
N_EXPERTS = 32
TOP_K = 4
D_EXPERT = 2048
SWIGLU_LIMIT = 7.0
SWIGLU_ALPHA = 1.702
PLE_DIM = 256

COL_RKV = 0
COL_Q = 3072
COL_KV = 4096
COL_MISC = 5632
MISC_W = 512
MISC_GATE = 384
IN_COLS = 6144

RW_HG = 4
RW_LW = RW_HG * RW_HEAD
RW_C = 64
NSA_TQ = 256
NSA_KT = 512
NSA_KWT = 256
NEG = -1e30
LOG2E = 1.4426950408889634


def _cp(sem, vmem=VMEM_LIMIT):
    return pltpu.CompilerParams(dimension_semantics=sem, vmem_limit_bytes=vmem)


def _dot(a, b, prec=None):
    return jnp.dot(a, b, preferred_element_type=F32, precision=prec)


def _dot_nt(a, b, prec=None):
    return lax.dot_general(a, b, (((1,), (1,)), ((), ())), preferred_element_type=F32, precision=prec)


def _dot_split(x, m, pieces=2, left=False):
    out, rem = None, x
    for _ in range(pieces):
        part = rem.astype(BF16)
        rem = rem - part.astype(F32)
        term = _dot(m, part) if left else _dot(part, m)
        out = term if out is None else out + term
    return out


def _iota(shape, dim):
    return lax.broadcasted_iota(I32, shape, dim)


def _seg_matrix(n, seg, scale):
    same = (_iota((n, n), 0) // seg) == (_iota((n, n), 1) // seg)
    return jnp.where(same, scale, 0.0).astype(BF16)


def _inproj_kernel(x_ref, g_ref, w_ref, o_ref, kv_ref, xn_ref, *, kv_lo, kv_hi):
    j = pl.program_id(1)

    @pl.when(j == 0)
    def _():
        x = x_ref[...]
        ms = jnp.mean(x * x, axis=-1, keepdims=True)
        xn_ref[...] = (x * lax.rsqrt(ms + RMS_EPS) * g_ref[...]).astype(BF16)

    u = _dot(xn_ref[...], w_ref[...])
    o_ref[...] = u

    @pl.when((j >= kv_lo) & (j < kv_hi))
    def _():
        kv_ref[...] = u.astype(kv_ref.dtype)


def _inproj(x2, g, w_bf, tm=1024, tn=512):
    n, d = x2.shape
    nc = w_bf.shape[1]
    kv_lo, kv_hi = COL_KV // tn, (COL_KV + 6 * NSA_KV) // tn
    kv_col = lambda i, j: (i, jnp.clip(j, kv_lo, kv_hi - 1) - kv_lo)
    return pl.pallas_call(
        functools.partial(_inproj_kernel, kv_lo=kv_lo, kv_hi=kv_hi),
        out_shape=(jax.ShapeDtypeStruct((n, nc), F32), jax.ShapeDtypeStruct((n, 6 * NSA_KV), BF16)),
        grid=(n // tm, nc // tn),
        in_specs=[
            pl.BlockSpec((tm, d), lambda i, j: (i, 0)),
            pl.BlockSpec((1, d), lambda i, j: (0, 0)),
            pl.BlockSpec((d, tn), lambda i, j: (0, j)),
        ],
        out_specs=(pl.BlockSpec((tm, tn), lambda i, j: (i, j)), pl.BlockSpec((tm, tn), kv_col)),
        scratch_shapes=[pltpu.VMEM((tm, d), BF16)],
        compiler_params=_cp(("arbitrary", "arbitrary")),
        name="inproj",
    )(x2, g, w_bf)


def _softplus(z):
    return jnp.maximum(z, 0.0) + jnp.log(1.0 + jnp.exp(-jnp.abs(z)))


def _rwkv_kernel(r_ref, k_ref, v_ref, m_ref, mur_ref, muk_ref, muv_ref, mum_ref, w2_ref, a2_ref, g2_ref,
                 w0_ref, a0_ref, kk_ref, ka_ref, rk_ref, lw_ref, lb_ref, o_ref,
                 st_ref, pr_ref, pk_ref, pv_ref, pm_ref, *, nchunk):
    tt = nchunk * RW_C
    lw = RW_LW

    @pl.when(pl.program_id(2) == 0)
    def _():
        st_ref[...] = jnp.zeros_like(st_ref)
        pr_ref[...] = jnp.zeros_like(pr_ref)
        pk_ref[...] = jnp.zeros_like(pk_ref)
        pv_ref[...] = jnp.zeros_like(pv_ref)
        pm_ref[...] = jnp.zeros_like(pm_ref)

    def shift_lerp(x, prev_ref, mu):
        rolled = pltpu.roll(x, 1, 0)
        first = _iota(x.shape, 0) == 0
        sh = jnp.where(first, prev_ref[0:1, :], rolled)
        prev_ref[0:1, :] = x[tt - 1:tt, :]
        return x + (sh - x) * mu

    r = shift_lerp(r_ref[0], pr_ref, mur_ref[...])
    k = shift_lerp(k_ref[0], pk_ref, muk_ref[...])
    v = shift_lerp(v_ref[0], pv_ref, muv_ref[...])
    m = shift_lerp(m_ref[0], pm_ref, mum_ref[...])

    lw_dec = _dot(jnp.tanh(m).astype(BF16), w2_ref[...])
    lw_a = _dot(m.astype(BF16), a2_ref[...])
    g = _dot(jax.nn.sigmoid(m).astype(BF16), g2_ref[...])
    w_log = -_softplus(-(w0_ref[...] + lw_dec)) - 0.5
    ld = -jnp.exp(w_log)
    a = jax.nn.sigmoid(a0_ref[...] + lw_a)

    ones_seg = _seg_matrix(lw, RW_HEAD, 1.0)
    kk = k * kk_ref[...]
    kk = kk / jnp.maximum(jnp.sqrt(_dot_split(kk * kk, ones_seg)), 1e-12)
    k2 = k * (1.0 + (a - 1.0) * ka_ref[...])
    av = -kk
    bv = kk * a

    ri = _iota((lw, lw), 0)
    ci = _iota((lw, lw), 1)
    same_blk = (ri // RW_C) == (ci // RW_C)
    m_strict = same_blk & (ci < ri)
    m_lower = same_blk & (ci <= ri)
    eye = (ri == ci).astype(F32)
    tri = (_iota((RW_C, RW_C), 1) <= _iota((RW_C, RW_C), 0)).astype(BF16)
    lane_head = _iota((1, lw), 1) // RW_HEAD
    hmask = [(lane_head == h).astype(F32) for h in range(RW_HG)]

    def stack_heads(x):
        return jnp.concatenate([x * hmask[h] for h in range(RW_HG)], axis=0)

    def unstack_heads(xs):
        out = xs[0:RW_C] * hmask[0]
        for h in range(1, RW_HG):
            out = out + xs[h * RW_C:(h + 1) * RW_C] * hmask[h]
        return out

    def fold_heads(xs):
        out = xs[0:RW_C]
        for h in range(1, RW_HG):
            out = out + xs[h * RW_C:(h + 1) * RW_C]
        return out

    chunks = []
    for c in range(nchunk):
        sl = slice(c * RW_C, (c + 1) * RW_C)
        ld_c = ld[sl]
        cs = _dot_split(ld_c, tri, pieces=3, left=True)
        w_in = jnp.exp(cs)
        w_inv = jnp.exp(-cs)
        w_prev = jnp.exp(cs - ld_c)
        w_last = w_in[RW_C - 1:RW_C, :]
        bt = bv[sl] * w_inv
        kt = k2[sl] * w_inv
        at_s = stack_heads(av[sl] * w_prev)
        rt_s = stack_heads(r[sl] * w_in)
        v_c = v[sl]
        chunks.append(dict(
            at_s=at_s, rt_s=rt_s, at_b=at_s.astype(BF16), rt_b=rt_s.astype(BF16), w_last=w_last,
            bt4=jnp.concatenate([bt] * RW_HG, axis=0).astype(BF16),
            kt4=jnp.concatenate([kt] * RW_HG, axis=0).astype(BF16),
            v4=jnp.concatenate([v_c] * RW_HG, axis=0).astype(BF16),
            bp=(bt * w_last).astype(BF16),
            sv=jnp.where(same_blk, _dot(v_c.T.astype(BF16), (kt * w_last).astype(BF16)), 0.0)))
    for ch in chunks:
        ch["a_ab"] = jnp.where(m_strict, _dot_nt(ch["at_b"], ch["bt4"]), 0.0)
        ch["a_ak"] = jnp.where(m_strict, _dot_nt(ch["at_b"], ch["kt4"]), 0.0).astype(BF16)
        ch["a_rb"] = jnp.where(m_lower, _dot_nt(ch["rt_b"], ch["bt4"]), 0.0).astype(BF16)
        ch["a_rk"] = jnp.where(m_lower, _dot_nt(ch["rt_b"], ch["kt4"]), 0.0).astype(BF16)
        ch["pw"] = ch["a_ab"]
        ch["inv"] = eye + ch["a_ab"]
    for _ in range(RW_C.bit_length() - 2):
        for ch in chunks:
            pw_b = ch["pw"].astype(BF16)
            ch["pw"] = _dot(pw_b, pw_b)
        for ch in chunks:
            ch["inv"] = ch["inv"] + _dot(ch["inv"].astype(BF16), ch["pw"].astype(BF16))
    for ch in chunks:
        inv_b = ch["inv"].astype(BF16)
        ia = _dot(inv_b, ch["at_b"])
        u_const = _dot(inv_b, _dot(ch["a_ak"], ch["v4"]).astype(BF16))
        g_mat = ch["rt_s"] + _dot(ch["a_rb"], ia.astype(BF16))
        y_const = _dot(ch["a_rb"], u_const.astype(BF16)) + _dot(ch["a_rk"], ch["v4"])
        ch["ia_u"] = fold_heads(ia).astype(BF16)
        ch["g_u"] = fold_heads(g_mat).astype(BF16)
        ch["u_const_t"] = unstack_heads(u_const).T
        ch["y_const"] = unstack_heads(y_const)

    ys = []
    st = st_ref[...]
    for ch in chunks:
        st_b = st.astype(BF16)
        ys.append(_dot_nt(ch["g_u"], st_b) + ch["y_const"])
        u_t = _dot_nt(st_b, ch["ia_u"]) + ch["u_const_t"]
        st = st * ch["w_last"] + jnp.where(same_blk, _dot(u_t.astype(BF16), ch["bp"]), 0.0) + ch["sv"]
    st_ref[...] = st
    y = jnp.concatenate(ys, axis=0) if nchunk > 1 else ys[0]

    avg_seg = _seg_matrix(lw, RW_HEAD, 1.0 / RW_HEAD)
    mean = _dot_split(y, avg_seg)
    yc = y - mean
    var = _dot_split(yc * yc, avg_seg)
    yn = yc * lax.rsqrt(var + RW_GN_EPS) * lw_ref[...] + lb_ref[...]
    bonus = _dot_split(r * k2 * rk_ref[...], ones_seg) * v
    o_ref[0] = ((yn + bonus) * g).astype(o_ref.dtype)


def _rwkv(u3, mu_p, w2p, a2p, g2p, w0, a0, k_k, k_a, r_k, lnx_w, lnx_b, nchunk=8):
    b, t, _ = u3.shape
    tt = nchunk * RW_C
    lw = RW_LW
    nhg = RW_WIDTH // lw
    col = lambda base: (lambda bi, h, ti: (bi, ti, base + h))
    vec = lambda base: (lambda bi, h, ti: (0, base + h))
    par = pl.BlockSpec((1, lw), vec(0))
    return pl.pallas_call(
        functools.partial(_rwkv_kernel, nchunk=nchunk),
        out_shape=jax.ShapeDtypeStruct((b, t, RW_WIDTH), BF16),
        grid=(b, nhg, t // tt),
        in_specs=[
            pl.BlockSpec((1, tt, lw), col(0)),
            pl.BlockSpec((1, tt, lw), col(nhg)),
            pl.BlockSpec((1, tt, lw), col(2 * nhg)),
            pl.BlockSpec((1, tt, MISC_W), lambda bi, h, ti: (bi, ti, COL_MISC // MISC_W)),
            pl.BlockSpec((1, lw), vec(0)),
            pl.BlockSpec((1, lw), vec(nhg)),
            pl.BlockSpec((1, lw), vec(2 * nhg)),
            pl.BlockSpec((1, MISC_W), lambda bi, h, ti: (0, COL_MISC // MISC_W)),
            pl.BlockSpec((MISC_W, lw), lambda bi, h, ti: (0, h)),
            pl.BlockSpec((MISC_W, lw), lambda bi, h, ti: (0, h)),
            pl.BlockSpec((MISC_W, lw), lambda bi, h, ti: (0, h)),
            par, par, par, par, par, par, par,
        ],
        out_specs=pl.BlockSpec((1, tt, lw), lambda bi, h, ti: (bi, ti, h)),
        scratch_shapes=[
            pltpu.VMEM((lw, lw), F32),
            pltpu.VMEM((8, lw), F32),
            pltpu.VMEM((8, lw), F32),
            pltpu.VMEM((8, lw), F32),
            pltpu.VMEM((8, MISC_W), F32),
        ],
        compiler_params=_cp(("arbitrary", "arbitrary", "arbitrary")),
        name="rwkv7",
    )(u3, u3, u3, u3, mu_p, mu_p, mu_p, mu_p, w2p, a2p, g2p, w0, a0, k_k, k_a, r_k, lnx_w, lnx_b)


def _pad_rows(w, rows, offset):
    return jnp.zeros((rows, w.shape[1]), w.dtype).at[offset:offset + w.shape[0]].set(w)


def _rwkv_params(mu, w0, w2, a0, a2, g2, k_k, k_a, r_k, lnx_w, lnx_b):
    mu_p = jnp.zeros((1, IN_COLS), F32)
    mu_p = mu_p.at[0, COL_RKV:COL_RKV + 3 * RW_WIDTH].set(mu[:3 * RW_WIDTH])
    mu_p = mu_p.at[0, COL_MISC:COL_MISC + sum(RW_LORA)].set(mu[3 * RW_WIDTH:])
    o1, o2 = RW_LORA[0], RW_LORA[0] + RW_LORA[1]
    row = lambda p: p.reshape(1, RW_WIDTH).astype(F32)
    return (mu_p, _pad_rows(w2, MISC_W, 0).astype(BF16), _pad_rows(a2, MISC_W, o1).astype(BF16),
            _pad_rows(g2, MISC_W, o2).astype(BF16), row(w0), row(a0), row(k_k), row(k_a), row(r_k),
            row(lnx_w), row(lnx_b))


def _nsa_prep_kernel(q_ref, ks_ref, kw_ref, gt_ref, qg_ref, ksg_ref, kwg_ref, qo_ref, kso_ref, kwo_ref, go_ref):
    avg = _seg_matrix(NSA_KV, NSA_HEAD, 1.0 / NSA_HEAD)

    def head_norm(x, g):
        ms = _dot_split(x * x, avg)
        return x * lax.rsqrt(ms + RMS_EPS) * g

    scale = NSA_HEAD ** -0.5 * LOG2E
    for s in range(NSA_G):
        sl = slice(s * NSA_KV, (s + 1) * NSA_KV)
        q_t = (head_norm(q_ref[:, sl], qg_ref[...]) * scale).T
        for h in range(NSA_R):
            qo_ref[0, s, h] = q_t[h * NSA_HEAD:(h + 1) * NSA_HEAD, :].astype(qo_ref.dtype)
    kso_ref[...] = head_norm(ks_ref[...], ksg_ref[...]).astype(kso_ref.dtype)
    kwo_ref[...] = head_norm(kw_ref[...], kwg_ref[...]).astype(kwo_ref.dtype)
    go_ref[...] = jax.nn.sigmoid(gt_ref[...])


def _nsa_prep(u2, b, q_gain, ks_gain, kw_gain, tm=512):
    n = u2.shape[0]
    tiles = n // b // tm
    kvb = COL_KV // NSA_KV
    gain = pl.BlockSpec((1, NSA_KV), lambda i: (0, 0))
    return pl.pallas_call(
        _nsa_prep_kernel,
        out_shape=(jax.ShapeDtypeStruct((b, NSA_G, NSA_R, NSA_HEAD, n // b), BF16),
                   jax.ShapeDtypeStruct((n, NSA_KV), BF16),
                   jax.ShapeDtypeStruct((n, NSA_KV), BF16), jax.ShapeDtypeStruct((n, LANES), F32)),
        grid=(n // tm,),
        in_specs=[
            pl.BlockSpec((tm, NSA_WIDTH), lambda i: (i, COL_Q // NSA_WIDTH)),
            pl.BlockSpec((tm, NSA_KV), lambda i: (i, kvb + 2)),
            pl.BlockSpec((tm, NSA_KV), lambda i: (i, kvb + 4)),
            pl.BlockSpec((tm, LANES), lambda i: (i, (COL_MISC + MISC_GATE) // LANES)),
            gain, gain, gain,
        ],
        out_specs=(pl.BlockSpec((1, NSA_G, NSA_R, NSA_HEAD, tm), lambda i: (i // tiles, 0, 0, 0, i % tiles)),
                   pl.BlockSpec((tm, NSA_KV), lambda i: (i, 0)),
                   pl.BlockSpec((tm, NSA_KV), lambda i: (i, 0)), pl.BlockSpec((tm, LANES), lambda i: (i, 0))),
        compiler_params=_cp(("arbitrary",)),
        name="nsa_prep",
    )(u2, u2, u2, u2, q_gain, ks_gain, kw_gain)


def _nsa_cmp_kernel(zk_ref, zv_ref, pk_ref, pv_ref, kw1_ref, kb1_ref, kw2_ref, kb2_ref, vw1_ref, vb1_ref,
                    vw2_ref, vb2_ref, kn_ref, ko_ref, vo_ref):
    nz = zk_ref.shape[2]
    half = kw1_ref.shape[0] // 2

    def compress(z, pos, w1_ref, b1, w2, b2):
        top = _dot(z, w1_ref[0:half, :])
        bot = _dot(z, w1_ref[half:, :])
        c1 = _dot(jnp.broadcast_to(pos, (8, pos.shape[1])).astype(BF16), w1_ref[...])[0:1] + b1
        hid = jax.nn.gelu(top + pltpu.roll(bot, nz - 1, 0) + c1)
        return _dot(hid.astype(BF16), w2) + b2

    kc = compress(zk_ref[0, 0], pk_ref[...], kw1_ref, kb1_ref[...], kw2_ref[...], kb2_ref[...])
    ms = jnp.mean(kc * kc, axis=-1, keepdims=True)
    ko_ref[0, 0] = (kc * lax.rsqrt(ms + RMS_EPS) * kn_ref[...]).astype(ko_ref.dtype)
    vc = compress(zv_ref[0, 0], pv_ref[...], vw1_ref, vb1_ref[...], vw2_ref[...], vb2_ref[...])
    vo_ref[0, 0] = vc.astype(vo_ref.dtype)


def _nsa_cmp(zk, zv, pos_k, pos_v, kw1, kb1, kw2, kb2, vw1, vb1, vw2, vb2, kn0):
    b, g, nz, zw = zk.shape
    full = lambda a: pl.BlockSpec(a.shape, lambda bi, gi: (0,) * a.ndim)
    zspec = pl.BlockSpec((1, 1, nz, zw), lambda bi, gi: (bi, gi, 0, 0))
    ospec = pl.BlockSpec((1, 1, nz, NSA_HEAD), lambda bi, gi: (bi, gi, 0, 0))
    params = (pos_k, pos_v, kw1, kb1, kw2, kb2, vw1, vb1, vw2, vb2, kn0)
    return pl.pallas_call(
        _nsa_cmp_kernel,
        out_shape=(jax.ShapeDtypeStruct((b, g, nz, NSA_HEAD), BF16),) * 2,
        grid=(b, g),
        in_specs=[zspec, zspec] + [full(a) for a in params],
        out_specs=(ospec, ospec),
        compiler_params=_cp(("arbitrary", "arbitrary")),
        name="nsa_compress",
    )(zk, zv, *params)


def _nsa_select_kernel(q_ref, kc_ref, vc_ref, oc_ref, sel_ref, *, n_sel):
    tq = q_ref.shape[4]
    nz = kc_ref.shape[2]
    ns = sel_ref.shape[2]
    t0 = pl.program_id(2) * tq
    n_idx = _iota((nz, tq), 0)
    t_idx = t0 + _iota((nz, tq), 1)
    cmask = (n_idx * CMP_STRIDE + (CMP_LEN - 1) <= t_idx) & (n_idx < nz - 1)
    kc = kc_ref[0, 0]
    vc_t = vc_ref[0, 0]
    psum = jnp.zeros((nz, tq), F32)
    for r in range(NSA_R):
        s = jnp.where(cmask, _dot(kc, q_ref[0, 0, r]), -jnp.inf)
        mx = jnp.max(s, axis=0, keepdims=True)
        e = jnp.exp2(s - jnp.where(mx == -jnp.inf, 0.0, mx))
        p = e / jnp.maximum(jnp.sum(e, axis=0, keepdims=True), 1e-30)
        psum = psum + p
        oc_ref[0, 0, r] = _dot(vc_t, p.astype(BF16))
    jb = _iota((ns, nz), 0) * SLC_BLOCK
    cb = _iota((ns, nz), 1) * CMP_STRIDE
    overlap = ((cb < jb + SLC_BLOCK) & (cb + CMP_LEN > jb) & (_iota((ns, nz), 1) < nz - 1)).astype(BF16)
    imp = _dot_split(psum, overlap, pieces=3, left=True)
    j_idx = _iota((ns, tq), 0)
    t_q = t0 + _iota((ns, tq), 1)
    qblk = t_q // SLC_BLOCK
    forced = (j_idx == 0) | (j_idx == qblk) | (j_idx == qblk - 1)
    imp = jnp.where(forced, jnp.inf, imp)
    imp = jnp.where(j_idx * SLC_BLOCK <= t_q, imp, -jnp.inf)
    sub = 8
    groups = [imp[g * sub:(g + 1) * sub, :] for g in range(ns // sub)]
    ranks = [jnp.zeros((sub, tq), F32) for _ in groups]
    for i in range(ns):
        row = imp[i:i + 1, :]
        for g, imp_g in enumerate(groups):
            if (g + 1) * sub - 1 < i:
                before = row > imp_g
            elif g * sub > i:
                before = row >= imp_g
            else:
                j_g = g * sub + _iota((sub, tq), 0)
                before = (row > imp_g) | ((row == imp_g) & (j_g > i))
            ranks[g] = ranks[g] + jnp.where(before, 1.0, 0.0)
    rank = jnp.concatenate(ranks, axis=0)
    sel_ref[0, 0] = ((rank < n_sel) & (imp > -jnp.inf)).astype(F32)


def _nsa_select(q_t, kc, vc_t, tq=NSA_TQ):
    b, g, r, dk, t = q_t.shape
    nz = kc.shape[2]
    ns = t // SLC_BLOCK
    qspec = pl.BlockSpec((1, 1, r, dk, tq), lambda bi, gi, i: (bi, gi, 0, 0, i))
    return pl.pallas_call(
        functools.partial(_nsa_select_kernel, n_sel=min(N_SELECT, ns)),
        out_shape=(jax.ShapeDtypeStruct((b, g, r, dk, t), F32), jax.ShapeDtypeStruct((b, g, ns, t), F32)),
        grid=(b, g, t // tq),
        in_specs=[
            qspec,
            pl.BlockSpec((1, 1, nz, dk), lambda bi, gi, i: (bi, gi, 0, 0)),
            pl.BlockSpec((1, 1, dk, nz), lambda bi, gi, i: (bi, gi, 0, 0)),
        ],
        out_specs=(qspec, pl.BlockSpec((1, 1, ns, tq), lambda bi, gi, i: (bi, gi, 0, i))),
        compiler_params=_cp(("arbitrary", "arbitrary", "arbitrary")),
        name="nsa_select",
    )(q_t, kc, vc_t)


def _nsa_attn_kernel(q_ref, ks_ref, vs_ref, kw_ref, vw_ref, sel_ref, oc_ref, gt_ref, o_ref, m_ref, acc_ref):
    r, dk, tq = q_ref.shape[2:]
    kt = vs_ref.shape[4]
    kwt = vw_ref.shape[4]
    t0 = pl.program_id(2) * tq
    q_t = jnp.concatenate([q_ref[0, 0, h] for h in range(r)], axis=1)
    tpos = t0 + _iota((1, tq), 1)

    def reset():
        m_ref[...] = jnp.full_like(m_ref, NEG)
        acc_ref[...] = jnp.zeros_like(acc_ref)

    def online_update(s, bias, v_aug):
        ps, alphas = [], []
        for h in range(r):
            s_h = s[:, h * tq:(h + 1) * tq] + bias
            m_old = m_ref[h]
            m_new = jnp.maximum(m_old, jnp.max(s_h, axis=0, keepdims=True))
            alphas.append(jnp.exp2(m_old - m_new))
            ps.append(jnp.exp2((s_h - m_new).astype(BF16)))
            m_ref[h] = m_new
        acc_ref[...] = jnp.concatenate(alphas, axis=1) * acc_ref[...] + _dot(v_aug, jnp.concatenate(ps, axis=1))

    def result():
        seen = jnp.concatenate([m_ref[h] for h in range(r)], axis=1) > 0.5 * NEG
        acc = acc_ref[...]
        return jnp.where(seen, acc[0:dk] / jnp.maximum(acc[dk:dk + 1], 1e-30), 0.0)

    reset()
    nb = kt // SLC_BLOCK

    def sel_body(j, carry):
        k = ks_ref[0, 0, pl.ds(pl.multiple_of(j * kt, kt), kt), :]
        s = _dot(k, q_t)
        bm = jnp.concatenate([jnp.broadcast_to(sel_ref[0, 0, pl.ds(j * nb + c, 1), :], (SLC_BLOCK, tq))
                              for c in range(nb)], axis=0)
        kpos = j * kt + _iota((kt, tq), 0)
        online_update(s, jnp.where((bm > 0.5) & (kpos <= tpos), 0.0, NEG), vs_ref[0, 0, j])
        return carry

    lax.fori_loop(0, (t0 + tq + kt - 1) // kt, sel_body, 0)
    o_s = result()

    reset()

    def win_body(j, carry):
        k = kw_ref[0, 0, pl.ds(pl.multiple_of(j * kwt, kwt), kwt), :]
        s = _dot(k, q_t)
        kpos = j * kwt + _iota((kwt, tq), 0)
        online_update(s, jnp.where((kpos <= tpos) & (kpos > tpos - WINDOW), 0.0, NEG), vw_ref[0, 0, j])
        return carry

    lo = jnp.maximum(t0 - (WINDOW - 1), 0) // kwt
    lax.fori_loop(lo, (t0 + tq - 1) // kwt + 1, win_body, 0)
    o_w = result()

    outs = []
    for h in range(r):
        sl = slice(h * tq, (h + 1) * tq)
        gt = gt_ref[0, 0, h]
        outs.append(gt[0:1] * oc_ref[0, 0, h] + gt[1:2] * o_s[:, sl] + gt[2:3] * o_w[:, sl])
    o_ref[0] = jnp.concatenate(outs, axis=0).T.astype(o_ref.dtype)


def _nsa_attn(q_t, ks, vs_t, kw, vw_t, sel_t, oc_t, gates_t, tq=NSA_TQ):
    b, g, r, dk, t = q_t.shape
    ns = sel_t.shape[2]
    qspec = pl.BlockSpec((1, 1, r, dk, tq), lambda bi, gi, i: (bi, gi, 0, 0, i))
    kspec = pl.BlockSpec((1, 1, t, dk), lambda bi, gi, i: (bi, gi, 0, 0))
    vspec = lambda v: pl.BlockSpec((1, 1) + v.shape[2:], lambda bi, gi, i: (bi, gi, 0, 0, 0))
    return pl.pallas_call(
        _nsa_attn_kernel,
        out_shape=jax.ShapeDtypeStruct((b, t, g * r * dk), BF16),
        grid=(b, g, t // tq),
        in_specs=[
            qspec, kspec, vspec(vs_t), kspec, vspec(vw_t),
            pl.BlockSpec((1, 1, ns, tq), lambda bi, gi, i: (bi, gi, 0, i)),
            qspec,
            pl.BlockSpec((1, 1, r, 3, tq), lambda bi, gi, i: (bi, gi, 0, 0, i)),
        ],
        out_specs=pl.BlockSpec((1, tq, r * dk), lambda bi, gi, i: (bi, i, gi)),
        scratch_shapes=[pltpu.VMEM((r, 1, tq), F32), pltpu.VMEM((vs_t.shape[3], r * tq), F32)],
        compiler_params=_cp(("arbitrary", "arbitrary", "arbitrary")),
        name="nsa_attention",
    )(q_t, ks, vs_t, kw, vw_t, sel_t, oc_t, gates_t)


def _nsa(u3, kv_bf, q_norm, k_norm, pos_k, pos_v, ck_w1, ck_b1, ck_w2, ck_b2, cv_w1, cv_b1, cv_w2, cv_b2):
    b, t, _ = u3.shape
    n = b * t
    g, r, dk = NSA_G, NSA_R, NSA_HEAD
    tile_g = lambda p: jnp.tile(p.astype(F32), g).reshape(1, g * dk)
    q_t, ksn, kwn, gsig = _nsa_prep(u3.reshape(n, IN_COLS), b, tile_g(q_norm), tile_g(k_norm[1]), tile_g(k_norm[2]))
    per_group = lambda a: a.reshape(b, t, g, dk).transpose(0, 2, 1, 3)
    kv = lambda idx: kv_bf[..., idx * NSA_KV:(idx + 1) * NSA_KV]
    nz = t // CMP_STRIDE
    zk = per_group(kv(0)).reshape(b, g, nz, CMP_STRIDE * dk)
    zv = per_group(kv(1)).reshape(b, g, nz, CMP_STRIDE * dk)
    row = lambda p: p.reshape(1, -1).astype(F32)
    kc, vc = _nsa_cmp(zk, zv, row(pos_k), row(pos_v), ck_w1.astype(BF16), row(ck_b1), ck_w2.astype(BF16), row(ck_b2),
                      cv_w1.astype(BF16), row(cv_b1), cv_w2.astype(BF16), row(cv_b2), row(k_norm[0]))
    oc_t, sel_t = _nsa_select(q_t, kc, vc.transpose(0, 1, 3, 2))
    gates_t = gsig[:, :g * r * 3].reshape(b, t, g, r, 3).transpose(0, 2, 3, 4, 1)

    def tiled_t(a, tile):
        v_t = a.reshape(b, t // tile, tile, g, dk).transpose(0, 3, 1, 4, 2)
        extra = jnp.zeros(v_t.shape[:3] + (16, tile), BF16).at[:, :, :, 0, :].set(1.0)
        return jnp.concatenate([v_t, extra], axis=3)

    return _nsa_attn(q_t, per_group(ksn), tiled_t(kv(3), min(NSA_KT, t)), per_group(kwn),
                     tiled_t(kv(5), min(NSA_KWT, t)), sel_t, oc_t, gates_t)


def _outproj_kernel(x_ref, yr_ref, yn_ref, w_ref, g_ref, rw_ref, rb_ref,
                    h_ref, xn_ref, e_ref, gate_ref, rank_ref, cnt_ref, cnt_acc):
    tm = x_ref.shape[0]

    @pl.when(pl.program_id(0) == 0)
    def _():
        cnt_acc[...] = jnp.zeros_like(cnt_acc)

    h = x_ref[...] + _dot(yr_ref[...], w_ref[0:RW_WIDTH, :]) + _dot(yn_ref[...], w_ref[RW_WIDTH:, :])
    h_ref[...] = h
    ms = jnp.mean(h * h, axis=-1, keepdims=True)
    xn = h * lax.rsqrt(ms + RMS_EPS) * g_ref[...]
    xn_ref[...] = xn
    xn_hi = xn.astype(BF16)
    xn_lo = (xn - xn_hi.astype(F32)).astype(BF16)
    logits = (_dot(xn_hi, rw_ref[0]) + _dot(xn_lo, rw_ref[0]) + _dot(xn_hi, rw_ref[1])
              + rb_ref[...])
    lane = _iota((tm, LANES), 1).astype(F32)
    work = logits
    top_e, top_v = [], []
    for _ in range(TOP_K):
        mx = jnp.max(work, axis=1, keepdims=True)
        idx = jnp.min(jnp.where(work == mx, lane, float(LANES)), axis=1, keepdims=True)
        top_e.append(idx)
        top_v.append(mx)
        work = jnp.where(lane == idx, -jnp.inf, work)
    ex = [jnp.exp(v - top_v[0]) for v in top_v]
    den = ex[0] + ex[1] + ex[2] + ex[3]
    multihot = jnp.zeros((tm, LANES), F32)
    for e in top_e:
        multihot = multihot + (lane == e).astype(F32)
    strict = (_iota((tm, tm), 1) < _iota((tm, tm), 0)).astype(BF16)
    before = cnt_acc[...] + _dot(strict, multihot.astype(BF16))
    e_out = jnp.zeros((tm, LANES), F32)
    g_out = jnp.zeros((tm, LANES), F32)
    r_out = jnp.zeros((tm, LANES), F32)
    for k in range(TOP_K):
        slot = lane == float(k)
        rank_k = jnp.sum(jnp.where(lane == top_e[k], before, 0.0), axis=1, keepdims=True)
        e_out = jnp.where(slot, top_e[k], e_out)
        g_out = jnp.where(slot, ex[k] / den, g_out)
        r_out = jnp.where(slot, rank_k, r_out)
    e_ref[...] = e_out.astype(I32)
    gate_ref[...] = g_out
    rank_ref[...] = r_out.astype(I32)
    cnt_acc[...] = cnt_acc[...] + jnp.sum(multihot, axis=0, keepdims=True)
    cnt_ref[...] = cnt_acc[...].astype(I32)


def _outproj(x2, y_rw, y_nsa, w_out_bf, g, rw_p, rb_p, tm=256):
    n, d = x2.shape
    rowblk = lambda w: pl.BlockSpec((tm, w), lambda i: (i, 0))
    full = lambda a: pl.BlockSpec(a.shape, lambda i: (0,) * a.ndim)
    return pl.pallas_call(
        _outproj_kernel,
        out_shape=(jax.ShapeDtypeStruct((n, d), F32), jax.ShapeDtypeStruct((n, d), F32),
                   jax.ShapeDtypeStruct((n, LANES), I32), jax.ShapeDtypeStruct((n, LANES), F32),
                   jax.ShapeDtypeStruct((n, LANES), I32), jax.ShapeDtypeStruct((1, LANES), I32)),
        grid=(n // tm,),
        in_specs=[rowblk(d), rowblk(RW_WIDTH), rowblk(NSA_WIDTH), full(w_out_bf), full(g), full(rw_p), full(rb_p)],
        out_specs=(rowblk(d), rowblk(d), rowblk(LANES), rowblk(LANES), rowblk(LANES),
                   pl.BlockSpec((1, LANES), lambda i: (0, 0))),
        scratch_shapes=[pltpu.VMEM((1, LANES), F32)],
        compiler_params=_cp(("arbitrary",)),
        name="outproj_router",
    )(x2, y_rw, y_nsa, w_out_bf, g, rw_p, rb_p)


GATHER_UNROLL = 8


def _gather_kernel(row_tok, n_act, x_hbm, o_ref, buf, sem):
    tg = o_ref.shape[0]
    base = pl.program_id(0) * tg

    @pl.when(base < n_act[0])
    def _():
        def issue(r8, c):
            for u in range(GATHER_UNROLL):
                r = r8 * GATHER_UNROLL + u
                pltpu.make_async_copy(x_hbm.at[pl.ds(row_tok[base + r], 1), :], buf.at[pl.ds(r, 1), :],
                                      sem).start(priority=u % 2)
            return c

        lax.fori_loop(0, tg // GATHER_UNROLL, issue, 0)
        pltpu.make_async_copy(x_hbm.at[pl.ds(0, tg), :], buf, sem).wait()
        o_ref[...] = buf[...].astype(o_ref.dtype)

    @pl.when(base >= n_act[0])
    def _():
        o_ref[...] = jnp.zeros_like(o_ref)


def _gather_rows(row_tok, n_act, xn, n_rows, tg=512):
    d = xn.shape[1]
    return pl.pallas_call(
        _gather_kernel,
        out_shape=jax.ShapeDtypeStruct((n_rows, d), BF16),
        grid_spec=pltpu.PrefetchScalarGridSpec(
            num_scalar_prefetch=2,
            grid=(n_rows // tg,),
            in_specs=[pl.BlockSpec(memory_space=pl.ANY)],
            out_specs=pl.BlockSpec((tg, d), lambda i, rt, na: (i, 0)),
            scratch_shapes=[pltpu.VMEM((tg, d), xn.dtype), pltpu.SemaphoreType.DMA],
        ),
        compiler_params=_cp(("arbitrary",)),
        name="moe_gather",
    )(row_tok, n_act, xn)


def _stream_weights(s, sched, copies, use):
    e_s, j_s, first_s, grp_s, ne_s, nj_s, hn_s = sched

    @pl.when(first_s[s] == 1)
    def _():
        slot = grp_s[s] % 2

        @pl.when(s == 0)
        def _():
            for c in copies(e_s[0], j_s[0], 0):
                c.start()

        for c in copies(e_s[s], j_s[s], slot):
            c.wait()

        @pl.when(hn_s[s] == 1)
        def _():
            for c in copies(ne_s[s], nj_s[s], 1 - slot):
                c.start()

        use(slot)


def _gm1_kernel(e_s, j_s, rt_s, valid_s, first_s, rto_s, jo_s, grp_s, ne_s, nj_s, hn_s,
                x_ref, w_hbm, bg_ref, bl_ref, o_ref, wf, wgb, wlb, sems):
    s = pl.program_id(0)
    tn = wgb.shape[1]
    nt = w_hbm.shape[2] // (2 * tn)

    def copies(e, j, slot):
        return [pltpu.make_async_copy(w_hbm.at[e, :, pl.ds(pl.multiple_of((half * nt + j) * tn, tn), tn)],
                                      wf.at[slot, half], sems.at[slot]) for half in range(2)]

    def use(slot):
        wgb[...] = wf[slot, 0].astype(BF16)
        wlb[...] = wf[slot, 1].astype(BF16)

    @pl.when(valid_s[s] == 1)
    def _():
        _stream_weights(s, (e_s, j_s, first_s, grp_s, ne_s, nj_s, hn_s), copies, use)

        x = x_ref[...]
        gt = jnp.minimum(_dot(x, wgb[...]) + bg_ref[0], SWIGLU_LIMIT)
        lin = jnp.clip(_dot(x, wlb[...]) + bl_ref[0], -SWIGLU_LIMIT, SWIGLU_LIMIT)
        o_ref[...] = (gt * jax.nn.sigmoid(SWIGLU_ALPHA * gt) * (lin + 1.0)).astype(o_ref.dtype)

    @pl.when(valid_s[s] == 0)
    def _():
        o_ref[...] = jnp.zeros_like(o_ref)


def _gm2_kernel(e_s, j_s, rt_s, valid_s, first_s, rto_s, jo_s, grp_s, ne_s, nj_s, hn_s,
                x_ref, w_hbm, b_ref, o_ref, wf, wb, sems):
    s = pl.program_id(0)
    tn = wb.shape[1]

    def copies(e, j, slot):
        return [pltpu.make_async_copy(w_hbm.at[e, :, pl.ds(pl.multiple_of(j * tn, tn), tn)], wf.at[slot],
                                      sems.at[slot])]

    def use(slot):
        wb[...] = wf[slot].astype(BF16)

    @pl.when(valid_s[s] == 1)
    def _():
        _stream_weights(s, (e_s, j_s, first_s, grp_s, ne_s, nj_s, hn_s), copies, use)

        o_ref[...] = _dot(x_ref[...], wb[...]) + b_ref[0]

    @pl.when(valid_s[s] == 0)
    def _():
        o_ref[...] = jnp.zeros_like(o_ref)


def _schedule(counts, tr, n_col_tiles, max_tiles):
    tiles_e = (counts + tr - 1) // tr
    tile_end = jnp.cumsum(tiles_e)
    tile_start = tile_end - tiles_e
    steps_e = tiles_e * n_col_tiles
    step_end = jnp.cumsum(steps_e)
    total = step_end[-1]
    n_steps = max_tiles * n_col_tiles
    step = jnp.arange(n_steps, dtype=I32)
    s = jnp.minimum(step, total - 1)
    e = jnp.sum((step_end[None, :] <= s[:, None]).astype(I32), axis=1)
    own = (jnp.arange(N_EXPERTS, dtype=I32)[None, :] == e[:, None]).astype(I32)
    of_e = lambda per_expert: jnp.sum(own * per_expert[None, :], axis=1)
    local = s - of_e(step_end - steps_e)
    te = jnp.maximum(of_e(tiles_e), 1)
    j = (local // te).astype(I32)
    rt = (of_e(tile_start) + local % te).astype(I32)
    valid = step < total
    first = ((local % te) == 0).astype(I32)
    spare = step - total
    rt_out = jnp.where(valid, rt, tile_end[-1] + spare // n_col_tiles).astype(I32)
    j_out = jnp.where(valid, j, spare % n_col_tiles).astype(I32)
    ar = jnp.arange(N_EXPERTS, dtype=I32)
    has = tiles_e > 0
    later = jnp.min(jnp.where((ar[None, :] > ar[:, None]) & has[None, :], ar[None, :], N_EXPERTS), axis=1)
    grp = (of_e(jnp.cumsum(has.astype(I32)) - has.astype(I32)) * n_col_tiles + j).astype(I32)
    last_col = j == n_col_tiles - 1
    next_e = jnp.where(last_col, jnp.minimum(of_e(later), N_EXPERTS - 1), e).astype(I32)
    next_j = jnp.where(last_col, 0, j + 1).astype(I32)
    has_next = (jnp.where(last_col, of_e(later) < N_EXPERTS, True) & valid).astype(I32)
    sched = (e, j, rt, valid.astype(I32), first, rt_out, j_out, grp, next_e, next_j, has_next)
    return sched, tile_start * tr, tile_end[-1] * tr


def _grouped_mlp(x_rows, sched1, sched2, w1, b1, w2, b2, tr, tn1, tn2):
    n_rows, d = x_rows.shape
    de = w2.shape[1]
    nt1 = de // tn1
    b1r = b1.reshape(N_EXPERTS, 1, 2 * de)
    b2r = b2.reshape(N_EXPERTS, 1, d)
    x_map = lambda s, e, j, rt, *_: (rt[s], 0)
    out_map = lambda s, e, j, rt, v, f, ro, jo, *_: (ro[s], jo[s])
    hbm = pl.BlockSpec(memory_space=pl.ANY)
    act = pl.pallas_call(
        _gm1_kernel,
        out_shape=jax.ShapeDtypeStruct((n_rows, de), BF16),
        grid_spec=pltpu.PrefetchScalarGridSpec(
            num_scalar_prefetch=len(sched1),
            grid=(sched1[0].shape[0],),
            in_specs=[
                pl.BlockSpec((tr, d), x_map),
                hbm,
                pl.BlockSpec((1, 1, tn1), lambda s, e, j, *_: (e[s], 0, j[s])),
                pl.BlockSpec((1, 1, tn1), lambda s, e, j, *_: (e[s], 0, nt1 + j[s])),
            ],
            out_specs=pl.BlockSpec((tr, tn1), out_map),
            scratch_shapes=[pltpu.VMEM((2, 2, d, tn1), w1.dtype), pltpu.VMEM((d, tn1), BF16),
                            pltpu.VMEM((d, tn1), BF16), pltpu.SemaphoreType.DMA((2,))],
        ),
        compiler_params=_cp(("arbitrary",)),
        name="moe_up",
    )(*sched1, x_rows, w1, b1r, b1r)
    return pl.pallas_call(
        _gm2_kernel,
        out_shape=jax.ShapeDtypeStruct((n_rows, d), F32),
        grid_spec=pltpu.PrefetchScalarGridSpec(
            num_scalar_prefetch=len(sched2),
            grid=(sched2[0].shape[0],),
            in_specs=[
                pl.BlockSpec((tr, de), x_map),
                hbm,
                pl.BlockSpec((1, 1, tn2), lambda s, e, j, *_: (e[s], 0, j[s])),
            ],
            out_specs=pl.BlockSpec((tr, tn2), out_map),
            scratch_shapes=[pltpu.VMEM((2, de, tn2), w2.dtype), pltpu.VMEM((de, tn2), BF16),
                            pltpu.SemaphoreType.DMA((2,))],
        ),
        compiler_params=_cp(("arbitrary",)),
        name="moe_down",
    )(*sched2, act, w2, b2r)


def _ple_kernel(dest, h_ref, gate_ref, p_ref, y_hbm, g_ref, gw_ref, pw_ref, o_ref, buf, sems):
    tm = h_ref.shape[0]
    i = pl.program_id(0)
    slot = i % 2

    def fetch(step, s):
        def issue(t2, c):
            for u in range(2):
                t = t2 * 2 + u
                for k in range(TOP_K):
                    pltpu.make_async_copy(y_hbm.at[pl.ds(dest[(step * tm + t) * TOP_K + k], 1), :],
                                          buf.at[s, k, pl.ds(t, 1), :], sems.at[s]).start(priority=k % 2)
            return c

        lax.fori_loop(0, tm // 2, issue, 0)

    @pl.when(i == 0)
    def _():
        fetch(0, 0)

    @pl.when(i + 1 < pl.num_programs(0))
    def _():
        fetch(i + 1, 1 - slot)

    for k in range(TOP_K):
        pltpu.make_async_copy(y_hbm.at[pl.ds(0, tm), :], buf.at[slot, k], sems.at[slot]).wait()
    gate = gate_ref[...]
    h = h_ref[...]
    for k in range(TOP_K):
        h = h + gate[:, k:k + 1] * buf[slot, k]
    ms = jnp.mean(h * h, axis=-1, keepdims=True)
    hn = (h * lax.rsqrt(ms + RMS_EPS) * g_ref[...]).astype(BF16)
    pgate = jax.nn.sigmoid(_dot(hn, gw_ref[...]))
    o_ref[...] = h + pgate * _dot(p_ref[...].astype(BF16), pw_ref[...])


def _combine_ple(dest, h1, gate, p2, y_rows, g, gw_bf, pw_bf, tm=128):
    n, d = h1.shape
    rowblk = lambda w: pl.BlockSpec((tm, w), lambda i, ds: (i, 0))
    full = lambda a: pl.BlockSpec(a.shape, lambda i, ds: (0,) * a.ndim)
    return pl.pallas_call(
        _ple_kernel,
        out_shape=jax.ShapeDtypeStruct((n, d), F32),
        grid_spec=pltpu.PrefetchScalarGridSpec(
            num_scalar_prefetch=1,
            grid=(n // tm,),
            in_specs=[rowblk(d), rowblk(LANES), rowblk(p2.shape[1]), pl.BlockSpec(memory_space=pl.ANY),
                      full(g), full(gw_bf), full(pw_bf)],
            out_specs=rowblk(d),
            scratch_shapes=[pltpu.VMEM((2, TOP_K, tm, d), F32), pltpu.SemaphoreType.DMA((2,))],
        ),
        compiler_params=_cp(("arbitrary",)),
        name="combine_ple",
    )(dest, h1, gate, p2, y_rows, g, gw_bf, pw_bf)


MOE_TR = 512
MOE_TN1 = 512
MOE_TN2 = 1024


def _layer(x, p, mix_norm_g, w_in, rw, nsa, w_out, moe_norm_g, router_w, router_b, moe_w1, moe_b1, moe_w2, moe_b2,
           ple_norm_g, ple_w, ple_gate_w):
    b, t, d = x.shape
    n = b * t
    x2 = x.reshape(n, d)
    row = lambda v: v.reshape(1, -1).astype(F32)
    n_lora = sum(RW_LORA)
    rw_cols = 3 * RW_WIDTH + n_lora
    kv0 = rw_cols + NSA_WIDTH
    n_gate = 3 * NSA_G * NSA_R
    zeros = lambda w: jnp.zeros((d, w), w_in.dtype)
    w_p = jnp.concatenate([
        w_in[:, :3 * RW_WIDTH], w_in[:, rw_cols:kv0], w_in[:, kv0:kv0 + 6 * NSA_KV],
        w_in[:, 3 * RW_WIDTH:rw_cols], zeros(MISC_GATE - n_lora), w_in[:, kv0 + 6 * NSA_KV:],
        zeros(MISC_W - MISC_GATE - n_gate)], axis=1).astype(BF16)
    u, kv_bf = _inproj(x2, row(mix_norm_g), w_p)
    u3 = u.reshape(b, t, IN_COLS)
    y_rw = _rwkv(u3, *_rwkv_params(*rw))
    y_nsa = _nsa(u3, kv_bf.reshape(b, t, 6 * NSA_KV), *nsa)
    out = _tail(x2, y_rw.reshape(n, RW_WIDTH), y_nsa.reshape(n, NSA_WIDTH), p.reshape(n, PLE_DIM), w_out, moe_norm_g,
                router_w, router_b, moe_w1, moe_b1, moe_w2, moe_b2, ple_norm_g, ple_w, ple_gate_w)
    return out.reshape(b, t, d)


def _tail(x2, y_rw, y_nsa, p2, w_out, moe_norm_g, router_w, router_b, moe_w1, moe_b1, moe_w2, moe_b2,
          ple_norm_g, ple_w, ple_gate_w):
    n, d = x2.shape
    row = lambda v: v.reshape(1, -1).astype(F32)
    rw_f = jnp.zeros((d, LANES), F32).at[:, :N_EXPERTS].set(router_w)
    rw_hi = rw_f.astype(BF16)
    rw_p = jnp.stack([rw_hi, (rw_f - rw_hi.astype(F32)).astype(BF16)])
    rb_p = jnp.full((1, LANES), NEG, F32).at[0, :N_EXPERTS].set(router_b)
    h1, xn, top_e, gate, rank, counts = _outproj(x2, y_rw, y_nsa, w_out.astype(BF16), row(moe_norm_g), rw_p, rb_p)
    counts = counts[0, :N_EXPERTS]
    max_tiles = (n * TOP_K) // MOE_TR + N_EXPERTS
    n_rows = max_tiles * MOE_TR
    sched1, pstart, n_act = _schedule(counts, MOE_TR, D_EXPERT // MOE_TN1, max_tiles)
    sched2, _, _ = _schedule(counts, MOE_TR, d // MOE_TN2, max_tiles)
    top_e = top_e[:, :TOP_K]
    own = top_e[:, :, None] == jnp.arange(N_EXPERTS, dtype=I32)[None, None, :]
    dest = (jnp.sum(jnp.where(own, pstart[None, None, :], 0), axis=2) + rank[:, :TOP_K]).astype(I32)
    tok = jnp.broadcast_to(jnp.arange(n, dtype=I32)[:, None], (n, TOP_K))
    row_tok = jnp.zeros((n_rows,), I32).at[dest.reshape(-1)].set(tok.reshape(-1), unique_indices=True,
                                                                mode="promise_in_bounds")
    x_rows = _gather_rows(row_tok, n_act.reshape(1).astype(I32), xn, n_rows)
    y_rows = _grouped_mlp(x_rows, sched1, sched2, moe_w1, moe_b1, moe_w2, moe_b2, MOE_TR, MOE_TN1, MOE_TN2)
    return _combine_ple(dest.reshape(-1), h1, gate, p2, y_rows, row(ple_norm_g),
                        ple_gate_w.astype(BF16), ple_w.astype(BF16))


def kernel(x, p, mix_norm_g, w_in, rw_mu, rw_w0, rw_w2, rw_a0, rw_a2, rw_g2, rw_k_k, rw_k_a, rw_r_k, rw_lnx_w,
           rw_lnx_b, nsa_q_norm, nsa_k_norm, cmp_pos_k, cmp_pos_v, cmp_k_w1, cmp_k_b1, cmp_k_w2, cmp_k_b2, cmp_v_w1,
           cmp_v_b1, cmp_v_w2, cmp_v_b2, w_out, moe_norm_g, router_w, router_b, moe_w1, moe_b1, moe_w2, moe_b2,
           ple_norm_g, ple_w, ple_gate_w):
    h = x
    for i in range(mix_norm_g.shape[0]):
        rw = (rw_mu[i], rw_w0[i], rw_w2[i], rw_a0[i], rw_a2[i], rw_g2[i], rw_k_k[i], rw_k_a[i], rw_r_k[i],
              rw_lnx_w[i], rw_lnx_b[i])
        nsa = (nsa_q_norm[i], nsa_k_norm[i], cmp_pos_k[i], cmp_pos_v[i], cmp_k_w1[i], cmp_k_b1[i], cmp_k_w2[i],
               cmp_k_b2[i], cmp_v_w1[i], cmp_v_b1[i], cmp_v_w2[i], cmp_v_b2[i])
        h = _layer(h, p[i], mix_norm_g[i], w_in[i], rw, nsa, w_out[i], moe_norm_g[i], router_w[i], router_b[i],
                   moe_w1[i], moe_b1[i], moe_w2[i], moe_b2[i], ple_norm_g[i], ple_w[i], ple_gate_w[i])
    return h
```
